```python
import math
import jax
import jax.numpy as jnp
from jax import lax
import numpy as np

D_MODEL = 2048
BATCH = 4
SEQ = 2048
DEPTH = 2
DEC_BATCH = 128
DEC_SEQ = 4
PAST_LEN = 2048
PAGE_SIZE = 128

F32 = jnp.float32
EPS = 1e-6
N_BRANCH = 4
BRANCH_W = D_MODEL // N_BRANCH

DN_DK = 128
DN_DV = 128
DN_HEADS = BRANCH_W // DN_DV
DN_CONV = 4
DN_CHUNK = 64
DN_KW = DN_HEADS * DN_DK
DN_W = DN_HEADS * DN_DV
DN_CONV_CH = 2 * DN_KW + DN_W
DN_COLS = DN_CONV_CH + DN_W + 2 * DN_HEADS

RW_N = 64
RW_HEADS = BRANCH_W // RW_N
RW_W = RW_HEADS * RW_N
RW_LORA_W = 96
RW_LORA_A = 96
RW_LORA_G = 256
RW_GN_EPS = 64e-5
RW_COLS = 3 * RW_W + RW_LORA_W + RW_LORA_A + RW_LORA_G

GLA_DK = 64
GLA_DV = 128
GLA_HEADS = BRANCH_W // GLA_DV
GLA_KW = GLA_HEADS * GLA_DK
GLA_W = GLA_HEADS * GLA_DV
GLA_LORA = 16
GLA_TAU = 16.0
GLA_CHUNK = 16
GLA_COLS = 2 * GLA_KW + 2 * GLA_W + GLA_LORA

FOX_HD = 128
FOX_HEADS = BRANCH_W // FOX_HD
FOX_W = FOX_HEADS * FOX_HD
FOX_QBLOCK = 128
FOX_COLS = 3 * FOX_W + FOX_HEADS

IN_COLS = DN_COLS + RW_COLS + GLA_COLS + FOX_COLS
IN_SPLITS = [DN_COLS, DN_COLS + RW_COLS, DN_COLS + RW_COLS + GLA_COLS]
DN_SPLITS = [DN_CONV_CH, DN_CONV_CH + DN_W, DN_CONV_CH + DN_W + DN_HEADS]
RW_SPLITS = [RW_W, 2 * RW_W, 3 * RW_W, 3 * RW_W + RW_LORA_W, 3 * RW_W + RW_LORA_W + RW_LORA_A]
GLA_SPLITS = [GLA_KW, 2 * GLA_KW, 2 * GLA_KW + GLA_W, 2 * GLA_KW + 2 * GLA_W]
FOX_SPLITS = [FOX_W, 2 * FOX_W, 3 * FOX_W]

MEM_LEN = 256
MEM_HEADS = 4
MEM_HD = 128
MEM_W = MEM_HEADS * MEM_HD

N_GROUPS = 4
EXPERTS_PER_GROUP = 8
N_EXPERTS = N_GROUPS * EXPERTS_PER_GROUP
TOP_K = 2
D_EXPERT = 512
MOE_BLOCK = 128

kernel_name = 'hybrid_parallel_gated_decoder_step'


def rmsnorm(x, g):
    xf = x.astype(F32)
    return xf * lax.rsqrt(jnp.mean(xf * xf, axis=-1, keepdims=True) + EPS) * g.astype(F32)


def l2norm(x):
    return x * lax.rsqrt(jnp.sum(x * x, axis=-1, keepdims=True) + EPS)


def heads(t, n, d):
    return t.reshape(t.shape[:-1] + (n, d))


def causal_dwconv(u, buf, w):
    full = jnp.concatenate([buf.astype(F32), u], axis=1)
    out = lax.conv_general_dilated(full, w.astype(F32)[:, None, :], window_strides=(1,), padding='VALID',
                                   dimension_numbers=('NWC', 'WIO', 'NWC'), feature_group_count=u.shape[-1])
    return out, full[:, full.shape[1] - (w.shape[0] - 1):]


def to_chunks(t, c):
    b, l, h = t.shape[:3]
    t = t.reshape((b, l // c, c, h) + t.shape[3:])
    return jnp.moveaxis(t, (1, 3), (0, 2))


def from_chunks(t):
    n, b, h, c = t.shape[:4]
    return jnp.moveaxis(t, (0, 2), (1, 3)).reshape((b, n * c, h) + t.shape[4:])


def gated_delta_chunked(q, k, v, g, beta, s0):
    L = q.shape[1]
    c = DN_CHUNK if L % DN_CHUNK == 0 else L
    qc, kc, vc = to_chunks(q, c), to_chunks(k, c), to_chunks(v, c)
    gc = jnp.cumsum(to_chunks(g, c), axis=-1)
    bc = to_chunks(beta, c)
    idx = jnp.arange(c)
    strict = idx[:, None] > idx[None, :]
    incl = idx[:, None] >= idx[None, :]
    gamma = jnp.exp(jnp.where(incl, gc[..., :, None] - gc[..., None, :], -jnp.inf))
    kk = jnp.einsum('nbhid,nbhjd->nbhij', kc, kc)
    a_mat = jnp.where(strict, bc[..., :, None] * kk * gamma, 0.0) + jnp.eye(c, dtype=F32)
    u = lax.linalg.triangular_solve(a_mat, bc[..., None] * vc, left_side=True, lower=True, unit_diagonal=True)
    w = lax.linalg.triangular_solve(a_mat, (bc * jnp.exp(gc))[..., None] * kc, left_side=True, lower=True,
                                    unit_diagonal=True)
    qk = jnp.where(incl, jnp.einsum('nbhid,nbhjd->nbhij', qc, kc) * gamma, 0.0)
    qg = qc * jnp.exp(gc)[..., None]
    kd = kc * jnp.exp(gc[..., -1:] - gc)[..., None]
    glast = jnp.exp(gc[..., -1])

    def step(s, xs):
        u_i, w_i, qk_i, qg_i, kd_i, gl_i = xs
        v_new = u_i - jnp.einsum('bhcd,bhde->bhce', w_i, s)
        o = jnp.einsum('bhcd,bhde->bhce', qg_i, s) + jnp.einsum('bhij,bhje->bhie', qk_i, v_new)
        s = gl_i[..., None, None] * s + jnp.einsum('bhcd,bhce->bhde', kd_i, v_new)
        return s, o

    s, o = lax.scan(step, s0, (u, w, qk, qg, kd, glast))
    return from_chunks(o), s


def gla_chunked(q, k, v, gk, s0):
    L = q.shape[1]
    c = GLA_CHUNK if L % GLA_CHUNK == 0 else L
    qc, kc, vc = to_chunks(q, c), to_chunks(k, c), to_chunks(v, c)
    b = jnp.cumsum(to_chunks(gk, c), axis=-2)
    qt = qc * jnp.exp(b)
    kt = kc * jnp.exp(-b)
    kd = kc * jnp.exp(b[..., -1:, :] - b)
    blast = jnp.exp(b[..., -1, :])
    idx = jnp.arange(c)
    incl = idx[:, None] >= idx[None, :]
    att = jnp.where(incl, jnp.einsum('nbhid,nbhjd->nbhij', qt, kt), 0.0)
    o_intra = jnp.einsum('nbhij,nbhje->nbhie', att, vc)

    def step(s, xs):
        qt_i, kd_i, v_i, bl_i = xs
        o = jnp.einsum('bhcd,bhde->bhce', qt_i, s)
        s = bl_i[..., None] * s + jnp.einsum('bhcd,bhce->bhde', kd_i, v_i)
        return s, o

    s, o_inter = lax.scan(step, s0, (qt, kd, vc, blast))
    return from_chunks(o_intra + o_inter), s


def rwkv7_scan(r, logw, k, v, kk, a, s0):
    def step(s, xs):
        r_t, lw_t, k_t, v_t, kk_t, a_t = xs
        sa = jnp.einsum('bhvk,bhk->bhv', s, -kk_t)
        s = (s * jnp.exp(lw_t)[:, :, None, :] + sa[..., None] * (kk_t * a_t)[:, :, None, :]
             + v_t[..., None] * k_t[:, :, None, :])
        return s, jnp.einsum('bhvk,bhk->bhv', s, r_t)

    xs = tuple(jnp.moveaxis(t, 1, 0) for t in (r, logw, k, v, kk, a))
    s, y = lax.scan(step, s0, xs)
    return jnp.moveaxis(y, 0, 1), s


def fox_attention(q, k_all, v_all, c_all, past_len):
    b, L, h, dh = q.shape
    t_len = k_all.shape[1]
    qb = FOX_QBLOCK if L % FOX_QBLOCK == 0 else L
    nb = L // qb
    kpos = jnp.arange(t_len)
    c_k = jnp.swapaxes(c_all, 1, 2)
    q_blocks = jnp.moveaxis(q.reshape(b, nb, qb, h, dh), 1, 0)
    cq_blocks = jnp.moveaxis(c_all[:, past_len:].reshape(b, nb, qb, h), 1, 0)
    qpos = (past_len + jnp.arange(L)).reshape(nb, qb)

    def block(args):
        q_i, cq_i, pos_i = args
        s = (jnp.einsum('bqhd,bkhd->bhqk', q_i, k_all) * (dh ** -0.5)
             + jnp.swapaxes(cq_i, 1, 2)[..., :, None] - c_k[..., None, :])
        s = jnp.where(kpos[None, :] <= pos_i[:, None], s, -jnp.inf)
        p = jax.nn.softmax(s, axis=-1)
        return jnp.einsum('bhqk,bkhd->bqhd', p, v_all)

    o = lax.map(block, (q_blocks, cq_blocks, qpos))
    return jnp.moveaxis(o, 0, 1).reshape(b, L, h, dh)


def grouped_experts(h, eid, wts, w1, w3, w2):
    t, d = h.shape
    n_slots = t * TOP_K
    flat_e = eid.reshape(n_slots)
    order = jnp.argsort(flat_e)
    sorted_e = flat_e[order]
    counts = jnp.bincount(flat_e, length=N_EXPERTS)
    padded = (counts + MOE_BLOCK - 1) // MOE_BLOCK * MOE_BLOCK
    pad_end = jnp.cumsum(padded)
    pad_start = pad_end - padded
    start = jnp.cumsum(counts) - counts
    dest = pad_start[sorted_e] + jnp.arange(n_slots) - start[sorted_e]
    n_blocks = -(-n_slots // MOE_BLOCK) + N_EXPERTS
    n_rows = n_blocks * MOE_BLOCK
    slot_tok = jnp.full((n_rows,), t, jnp.int32).at[dest].set((order // TOP_K).astype(jnp.int32))
    block_e = jnp.minimum(jnp.searchsorted(pad_end, jnp.arange(n_blocks) * MOE_BLOCK, side='right'), N_EXPERTS - 1)
    hpad = jnp.concatenate([h, jnp.zeros((1, d), h.dtype)], axis=0)

    def block(args):
        tok, e = args
        xb = hpad[tok]
        return (jax.nn.silu(xb @ w1[e]) * (xb @ w3[e])) @ w2[e]

    yb = lax.map(block, (slot_tok.reshape(n_blocks, MOE_BLOCK), block_e))
    y_sorted = yb.reshape(n_rows, d)[dest]
    y_slot = jnp.zeros((n_slots, d), yb.dtype).at[order].set(y_sorted)
    return jnp.einsum('tkd,tk->td', y_slot.reshape(t, TOP_K, d), wts.astype(yb.dtype))


def hier_moe(h, lp):
    t = h.shape[0]
    pg = jax.nn.softmax(h @ lp['router_g'] + lp['router_g_b'], axis=-1)
    grp = jnp.argmax(pg, axis=-1)
    pg_top = jnp.max(pg, axis=-1)
    le = (h @ lp['router_e'] + lp['router_e_b']).reshape(t, N_GROUPS, EXPERTS_PER_GROUP)
    pe = jax.nn.softmax(le[jnp.arange(t), grp], axis=-1)
    top_p, top_i = lax.top_k(pe, TOP_K)
    wts = pg_top[:, None] * top_p / jnp.sum(top_p, axis=-1, keepdims=True)
    eid = grp[:, None] * EXPERTS_PER_GROUP + top_i
    return grouped_experts(h, eid, wts, lp['moe_w1'], lp['moe_w3'], lp['moe_w2'])


def parallel_mixer(h, dn_conv0, dn_s0, rw_shift0, rw_s0, gla_s0, fox_past, lp):
    b, L, _ = h.shape
    proj = h @ lp['w_in']
    p_dn, p_rw, p_gla, p_fox = jnp.split(proj, IN_SPLITS, axis=-1)

    qkv, z, a_dn, b_dn = jnp.split(p_dn, DN_SPLITS, axis=-1)
    conv, dn_conv1 = causal_dwconv(qkv, dn_conv0, lp['dn_conv_w'])
    conv = jax.nn.silu(conv)
    q, k, v = jnp.split(conv, [DN_KW, 2 * DN_KW], axis=-1)
    q = l2norm(heads(q, DN_HEADS, DN_DK)) * (DN_DK ** -0.5)
    k = l2norm(heads(k, DN_HEADS, DN_DK))
    g = -jnp.exp(lp['dn_A_log']) * jax.nn.softplus(a_dn + lp['dn_dt_bias'])
    o, dn_s1 = gated_delta_chunked(q, k, heads(v, DN_HEADS, DN_DV), g, jax.nn.sigmoid(b_dn), dn_s0.astype(F32))
    o_dn = (rmsnorm(o, lp['dn_norm_w']) * jax.nn.silu(heads(z, DN_HEADS, DN_DV))).reshape(b, L, BRANCH_W)

    prev = jnp.concatenate([rw_shift0[:, None].astype(F32), p_rw[:, :-1]], axis=1)
    rw_shift1 = p_rw[:, -1]
    xr = p_rw + lp['rw_mu'] * (prev - p_rw)
    r, k, v, wd, ad, gd = jnp.split(xr, RW_SPLITS, axis=-1)
    logw = -jnp.exp(-jax.nn.softplus(-(lp['rw_w0'] + jnp.tanh(wd) @ lp['rw_w_up'])) - 0.5)
    a = jax.nn.sigmoid(lp['rw_a0'] + ad @ lp['rw_a_up'])
    gate = jax.nn.sigmoid(gd) @ lp['rw_g_up']
    kk = l2norm(heads(k * lp['rw_k_k'], RW_HEADS, RW_N))
    k = k * (1.0 + (a - 1.0) * lp['rw_k_a'])
    r_h, k_h, v_h = heads(r, RW_HEADS, RW_N), heads(k, RW_HEADS, RW_N), heads(v, RW_HEADS, RW_N)
    y, rw_s1 = rwkv7_scan(r_h, heads(logw, RW_HEADS, RW_N), k_h, v_h, kk, heads(a, RW_HEADS, RW_N),
                          rw_s0.astype(F32))
    mu = jnp.mean(y, axis=-1, keepdims=True)
    var = jnp.mean(jnp.square(y - mu), axis=-1, keepdims=True)
    y = ((y - mu) * lax.rsqrt(var + RW_GN_EPS)).reshape(b, L, RW_W) * lp['rw_ln_w'] + lp['rw_ln_b']
    y = y + (jnp.sum(r_h * k_h * lp['rw_r_k'], axis=-1, keepdims=True) * v_h).reshape(b, L, RW_W)
    o_rw = y * gate

    q, k, v, gg, gd = jnp.split(p_gla, GLA_SPLITS, axis=-1)
    gk = jax.nn.log_sigmoid(gd @ lp['gla_a_up'] + lp['gla_a_b']) / GLA_TAU
    o, gla_s1 = gla_chunked(heads(q, GLA_HEADS, GLA_DK) * (GLA_DK ** -0.5), heads(k, GLA_HEADS, GLA_DK),
                            heads(v, GLA_HEADS, GLA_DV), heads(gk, GLA_HEADS, GLA_DK), gla_s0.astype(F32))
    o_gla = (rmsnorm(o, lp['gla_norm_w']) * jax.nn.silu(heads(gg, GLA_HEADS, GLA_DV))).reshape(b, L, BRANCH_W)

    q, k, v, fl = jnp.split(p_fox, FOX_SPLITS, axis=-1)
    q = rmsnorm(heads(q, FOX_HEADS, FOX_HD), lp['fox_qn'])
    k = rmsnorm(heads(k, FOX_HEADS, FOX_HD), lp['fox_kn'])
    v = heads(v, FOX_HEADS, FOX_HD)
    lf = jax.nn.log_sigmoid(fl + lp['fox_fb'])
    if fox_past is None:
        k_all, v_all, lf_all, past_len = k, v, lf, 0
    else:
        k_past, v_past, lf_past = fox_past
        past_len = k_past.shape[1]
        k_all = jnp.concatenate([k_past.astype(F32), k], axis=1)
        v_all = jnp.concatenate([v_past.astype(F32), v], axis=1)
        lf_all = jnp.concatenate([lf_past.astype(F32), lf], axis=1)
    o_fox = fox_attention(q, k_all, v_all, jnp.cumsum(lf_all, axis=1), past_len).reshape(b, L, BRANCH_W)

    o_cat = jnp.stack([o_dn, o_rw, o_gla, o_fox], axis=2)
    branch = jnp.einsum('blnc,ncd->blnd', o_cat, lp['w_branch'])
    gates = jax.nn.sigmoid(h @ lp['w_gate'] + lp['b_gate']).reshape(b, L, N_BRANCH, D_MODEL)
    out = jnp.sum(gates * branch, axis=2) @ lp['w_out']
    return out, (dn_conv1, dn_s1, rw_shift1, rw_s1, gla_s1, k, v, lf)


def memory_kv(mem, lp):
    b, m, _ = mem.shape
    mn = rmsnorm(mem, lp['norm_memkv'])
    return ((mn @ lp['mem_wk']).reshape(b, m, MEM_HEADS, MEM_HD),
            (mn @ lp['mem_wv']).reshape(b, m, MEM_HEADS, MEM_HD))


def trunk_layer(x, mem_k, mem_v, states, fox_past, lp):
    b, L, _ = x.shape
    h = rmsnorm(x, lp['norm_mix'])
    mix, new_states = parallel_mixer(h, states[0], states[1], states[2], states[3], states[4], fox_past, lp)
    x = x + mix.astype(x.dtype)
    h = rmsnorm(x, lp['norm_mem'])
    q = (h @ lp['mem_wq']).reshape(b, L, MEM_HEADS, MEM_HD)
    s = jnp.einsum('blhd,bmhd->bhlm', q, mem_k.astype(F32)) * (MEM_HD ** -0.5)
    att = jnp.einsum('bhlm,bmhd->blhd', jax.nn.softmax(s, axis=-1), mem_v.astype(F32)).reshape(b, L, MEM_W)
    x = x + (att @ lp['mem_wo']).astype(x.dtype)
    h = rmsnorm(x, lp['norm_ffn'])
    x = x + hier_moe(h.reshape(b * L, D_MODEL), lp).reshape(b, L, D_MODEL).astype(x.dtype)
    return x, new_states


def setup_inputs(seed: int = 0) -> dict:
    key = jax.random.key(seed)
    ks = iter(jax.random.split(key, 96))
    D = D_MODEL

    def nrm(shape, scale):
        return scale * jax.random.normal(next(ks), shape, F32)

    def gain(shape):
        return 1.0 + nrm(shape, 0.02)

    def unif(shape, lo, hi):
        return jax.random.uniform(next(ks), shape, F32, lo, hi)

    n_pages = PAST_LEN // PAGE_SIZE
    n_used = DEC_BATCH * n_pages
    n_phys = n_used + n_used // 4
    page_table = jax.random.permutation(next(ks), n_phys)[:n_used].reshape(DEC_BATCH, n_pages).astype(jnp.int32)
    dt = jnp.exp(unif((DEPTH, DN_HEADS), math.log(1e-3), math.log(1e-1)))
    return dict(
        x_prompt=nrm((BATCH, SEQ, D), 1.0),
        x_sample=nrm((DEC_BATCH, DEC_SEQ, D), 1.0),
        cache_fox_k=nrm((DEPTH, n_phys, PAGE_SIZE, FOX_HEADS, FOX_HD), 1.0),
        cache_fox_v=nrm((DEPTH, n_phys, PAGE_SIZE, FOX_HEADS, FOX_HD), 1.0),
        cache_fox_logf=jax.nn.log_sigmoid(nrm((DEPTH, n_phys, PAGE_SIZE, FOX_HEADS), 1.0) + 3.0),
        state_dn_conv=nrm((DEPTH, DEC_BATCH, DN_CONV - 1, DN_CONV_CH), 1.0),
        state_dn=nrm((DEPTH, DEC_BATCH, DN_HEADS, DN_DK, DN_DV), 0.1),
        state_rw_shift=nrm((DEPTH, DEC_BATCH, RW_COLS), 1.0),
        state_rw=nrm((DEPTH, DEC_BATCH, RW_HEADS, RW_N, RW_N), 0.1),
        state_gla=nrm((DEPTH, DEC_BATCH, GLA_HEADS, GLA_DK, GLA_DV), 0.1),
        cache_mem_k=nrm((DEPTH, DEC_BATCH, MEM_LEN, MEM_HEADS, MEM_HD), 1.0),
        cache_mem_v=nrm((DEPTH, DEC_BATCH, MEM_LEN, MEM_HEADS, MEM_HD), 1.0),
        page_table=page_table,
        mem_prompt=nrm((BATCH, MEM_LEN, D), 1.0),
        norm_mix=gain((DEPTH, D)),
        w_in=nrm((DEPTH, D, IN_COLS), D ** -0.5),
        dn_conv_w=nrm((DEPTH, DN_CONV, DN_CONV_CH), DN_CONV ** -0.5),
        dn_A_log=jnp.log(unif((DEPTH, DN_HEADS), 1.0, 16.0)),
        dn_dt_bias=dt + jnp.log(-jnp.expm1(-dt)),
        dn_norm_w=gain((DEPTH, DN_DV)),
        rw_mu=unif((DEPTH, RW_COLS), 0.0, 1.0),
        rw_w0=unif((DEPTH, RW_W), -6.0, -1.0),
        rw_w_up=nrm((DEPTH, RW_LORA_W, RW_W), 0.1 * RW_LORA_W ** -0.5),
        rw_a0=nrm((DEPTH, RW_W), 0.1),
        rw_a_up=nrm((DEPTH, RW_LORA_A, RW_W), RW_LORA_A ** -0.5),
        rw_g_up=nrm((DEPTH, RW_LORA_G, RW_W), RW_LORA_G ** -0.5),
        rw_k_k=0.85 + nrm((DEPTH, RW_W), 0.02),
        rw_k_a=1.0 + nrm((DEPTH, RW_W), 0.02),
        rw_r_k=nrm((DEPTH, RW_HEADS, RW_N), 0.1),
        rw_ln_w=gain((DEPTH, RW_W)),
        rw_ln_b=nrm((DEPTH, RW_W), 0.01),
        gla_a_up=nrm((DEPTH, GLA_LORA, GLA_KW), GLA_LORA ** -0.5),
        gla_a_b=nrm((DEPTH, GLA_KW), 0.1),
        gla_norm_w=gain((DEPTH, GLA_DV)),
        fox_qn=gain((DEPTH, FOX_HD)),
        fox_kn=gain((DEPTH, FOX_HD)),
        fox_fb=3.0 + nrm((DEPTH, FOX_HEADS), 0.1),
        w_gate=nrm((DEPTH, D, N_BRANCH * D), D ** -0.5),
        b_gate=nrm((DEPTH, N_BRANCH * D), 0.01),
        w_branch=nrm((DEPTH, N_BRANCH, BRANCH_W, D), BRANCH_W ** -0.5),
        w_out=nrm((DEPTH, D, D), D ** -0.5),
        norm_mem=gain((DEPTH, D)),
        norm_memkv=gain((DEPTH, D)),
        mem_wq=nrm((DEPTH, D, MEM_W), D ** -0.5),
        mem_wk=nrm((DEPTH, D, MEM_W), D ** -0.5),
        mem_wv=nrm((DEPTH, D, MEM_W), D ** -0.5),
        mem_wo=nrm((DEPTH, MEM_W, D), MEM_W ** -0.5),
        norm_ffn=gain((DEPTH, D)),
        router_g=nrm((DEPTH, D, N_GROUPS), D ** -0.5),
        router_g_b=nrm((DEPTH, N_GROUPS), 0.01),
        router_e=nrm((DEPTH, D, N_EXPERTS), D ** -0.5),
        router_e_b=nrm((DEPTH, N_EXPERTS), 0.01),
        moe_w1=nrm((DEPTH, N_EXPERTS, D, D_EXPERT), D ** -0.5),
        moe_w3=nrm((DEPTH, N_EXPERTS, D, D_EXPERT), D ** -0.5),
        moe_w2=nrm((DEPTH, N_EXPERTS, D_EXPERT, D), D_EXPERT ** -0.5),
        norm_final=gain((D,)),
    )


def reference(x_prompt, x_sample, cache_fox_k, cache_fox_v, cache_fox_logf, state_dn_conv, state_dn,
              state_rw_shift, state_rw, state_gla, cache_mem_k, cache_mem_v, page_table, mem_prompt,
              norm_mix, w_in, dn_conv_w, dn_A_log, dn_dt_bias, dn_norm_w, rw_mu, rw_w0, rw_w_up, rw_a0,
              rw_a_up, rw_g_up, rw_k_k, rw_k_a, rw_r_k, rw_ln_w, rw_ln_b, gla_a_up, gla_a_b, gla_norm_w,
              fox_qn, fox_kn, fox_fb, w_gate, b_gate, w_branch, w_out, norm_mem, norm_memkv, mem_wq, mem_wk,
              mem_wv, mem_wo, norm_ffn, router_g, router_g_b, router_e, router_e_b, moe_w1, moe_w3, moe_w2,
              norm_final):
    stacked = dict(norm_mix=norm_mix, w_in=w_in, dn_conv_w=dn_conv_w, dn_A_log=dn_A_log, dn_dt_bias=dn_dt_bias,
                   dn_norm_w=dn_norm_w, rw_mu=rw_mu, rw_w0=rw_w0, rw_w_up=rw_w_up, rw_a0=rw_a0, rw_a_up=rw_a_up,
                   rw_g_up=rw_g_up, rw_k_k=rw_k_k, rw_k_a=rw_k_a, rw_r_k=rw_r_k, rw_ln_w=rw_ln_w, rw_ln_b=rw_ln_b,
                   gla_a_up=gla_a_up, gla_a_b=gla_a_b, gla_norm_w=gla_norm_w, fox_qn=fox_qn, fox_kn=fox_kn,
                   fox_fb=fox_fb, w_gate=w_gate, b_gate=b_gate, w_branch=w_branch, w_out=w_out, norm_mem=norm_mem,
                   norm_memkv=norm_memkv, mem_wq=mem_wq, mem_wk=mem_wk, mem_wv=mem_wv, mem_wo=mem_wo,
                   norm_ffn=norm_ffn, router_g=router_g, router_g_b=router_g_b, router_e=router_e,
                   router_e_b=router_e_b, moe_w1=moe_w1, moe_w3=moe_w3, moe_w2=moe_w2)
    b = x_prompt.shape[0]
    db = x_sample.shape[0]
    past_len = page_table.shape[1] * PAGE_SIZE
    xp, xs = x_prompt, x_sample
    prompt_rows, sample_rows = [], []
    for l in range(DEPTH):
        lp = {name: arr[l] for name, arr in stacked.items()}
        zero_states = (jnp.zeros((b, DN_CONV - 1, DN_CONV_CH), F32), jnp.zeros((b, DN_HEADS, DN_DK, DN_DV), F32),
                       jnp.zeros((b, RW_COLS), F32), jnp.zeros((b, RW_HEADS, RW_N, RW_N), F32),
                       jnp.zeros((b, GLA_HEADS, GLA_DK, GLA_DV), F32))
        mk, mv = memory_kv(mem_prompt, lp)
        xp, new_p = trunk_layer(xp, mk, mv, zero_states, None, lp)
        prompt_rows.append(new_p + (mk, mv))
        fox_past = (cache_fox_k[l][page_table].reshape(db, past_len, FOX_HEADS, FOX_HD),
                    cache_fox_v[l][page_table].reshape(db, past_len, FOX_HEADS, FOX_HD),
                    cache_fox_logf[l][page_table].reshape(db, past_len, FOX_HEADS))
        st = (state_dn_conv[l], state_dn[l], state_rw_shift[l], state_rw[l], state_gla[l])
        xs, new_s = trunk_layer(xs, cache_mem_k[l], cache_mem_v[l], st, fox_past, lp)
        sample_rows.append(new_s)
    (p_dn_conv, p_dn, p_rw_shift, p_rw, p_gla, p_fox_k, p_fox_v, p_fox_logf, p_mem_k,
     p_mem_v) = [jnp.stack(rows) for rows in zip(*prompt_rows)]
    (s_dn_conv, s_dn, s_rw_shift, s_rw, s_gla, s_fox_k, s_fox_v,
     s_fox_logf) = [jnp.stack(rows) for rows in zip(*sample_rows)]
    y_prompt = rmsnorm(xp, norm_final).astype(x_prompt.dtype)
    y_sample = rmsnorm(xs, norm_final).astype(x_sample.dtype)
    return (y_prompt, y_sample, p_fox_k, p_fox_v, p_fox_logf, p_dn_conv, p_dn, p_rw_shift, p_rw, p_gla, p_mem_k,
            p_mem_v, s_fox_k, s_fox_v, s_fox_logf, s_dn_conv, s_dn, s_rw_shift, s_rw, s_gla)
```

```python
import functools
import math

import jax
import jax.numpy as jnp
from jax import lax
from jax.experimental import pallas as pl
from jax.experimental.pallas import tpu as pltpu

F32 = jnp.float32
BF16 = jnp.bfloat16
EPS = 1e-6
HP = lax.Precision.HIGHEST

D_MODEL = 2048
N_BRANCH = 4
BRANCH_W = 512
PAGE_SIZE = 128

DN_HEADS, DN_DK, DN_DV, DN_CONV, DN_CHUNK = 4, 128, 128, 4, 64
DN_CONV_CH = 1536
RW_HEADS, RW_N, RW_W = 8, 64, 512
RW_LORA_W, RW_LORA_A, RW_LORA_G = 96, 96, 256
RW_GN_EPS = 64e-5
RW_COLS = 1984
GLA_HEADS, GLA_DK, GLA_DV, GLA_LORA, GLA_TAU = 4, 64, 128, 16, 16.0
FOX_HEADS, FOX_HD = 4, 128
MEM_HEADS, MEM_HD, MEM_LEN = 4, 128, 256
N_GROUPS, EXPERTS_PER_GROUP, N_EXPERTS, TOP_K, D_EXPERT = 4, 8, 32, 2, 512

LANES = 128
SUBLANES = 8
VMEM_LIMIT = 56 * 1024 * 1024


def _cparams(*sem):
    return pltpu.CompilerParams(dimension_semantics=sem, vmem_limit_bytes=VMEM_LIMIT)


def _dot(a, b):
    return jnp.dot(a.astype(BF16), b.astype(BF16), preferred_element_type=F32)


def _dot_nt(a, b):
    return lax.dot_general(a.astype(BF16), b.astype(BF16), (((1,), (1,)), ((), ())), preferred_element_type=F32)


def _dot_tn(a, b):
    return lax.dot_general(a.astype(BF16), b.astype(BF16), (((0,), (0,)), ((), ())), preferred_element_type=F32)


def _split3(a):
    a1 = a.astype(BF16)
    r1 = a - a1.astype(F32)
    a2 = r1.astype(BF16)
    a3 = (r1 - a2.astype(F32)).astype(BF16)
    return a1, a2, a3


def _dot_hp(a, b):
    a1, a2, a3 = _split3(a)
    b1, b2, b3 = _split3(b)
    d = lambda x, y: jnp.dot(x, y, preferred_element_type=F32)
    return ((d(a3, b1) + d(a1, b3)) + d(a2, b2)) + ((d(a2, b1) + d(a1, b2)) + d(a1, b1))


def _dot_hp_exact_rhs(a, b_bf16):
    a1, a2, a3 = _split3(a)
    d = lambda x: jnp.dot(x, b_bf16, preferred_element_type=F32)
    return (d(a3) + d(a2)) + d(a1)


def _dot_hp_exact_lhs(a_bf16, b):
    b1, b2, b3 = _split3(b)
    d = lambda y: jnp.dot(a_bf16, y, preferred_element_type=F32)
    return (d(b3) + d(b2)) + d(b1)


def _iota2(shape, axis):
    return lax.broadcasted_iota(jnp.int32, shape, axis)


def _sigmoid(x):
    return 1.0 / (1.0 + jnp.exp(-x))


def _silu(x):
    return x * _sigmoid(x)


def _softplus(x):
    return jnp.maximum(x, 0.0) + jnp.log(1.0 + jnp.exp(-jnp.abs(x)))


def _log_sigmoid(x):
    return -_softplus(-x)


def _rms_matmul_body(x_ref, g_ref, w_ref, o_ref, h_ref):
    @pl.when(pl.program_id(1) == 0)
    def _():
        x = x_ref[...]
        ms = jnp.mean(x * x, axis=-1, keepdims=True)
        h_ref[...] = (x * lax.rsqrt(ms + EPS) * g_ref[...]).astype(BF16)

    o_ref[...] = jnp.dot(h_ref[...], w_ref[...], preferred_element_type=F32).astype(o_ref.dtype)


def rms_matmul(x, g, w, *, tm, tn, out_dtype=F32, name="rms_matmul"):
    m, k = x.shape
    n = w.shape[1]
    assert m % tm == 0 and n % tn == 0
    return pl.pallas_call(
        _rms_matmul_body,
        grid=(m // tm, n // tn),
        in_specs=[pl.BlockSpec((tm, k), lambda i, j: (i, 0)),
                  pl.BlockSpec((1, k), lambda i, j: (0, 0)),
                  pl.BlockSpec((k, tn), lambda i, j: (0, j))],
        out_specs=pl.BlockSpec((tm, tn), lambda i, j: (i, j)),
        out_shape=jax.ShapeDtypeStruct((m, n), out_dtype),
        scratch_shapes=[pltpu.VMEM((tm, k), BF16)],
        compiler_params=_cparams("parallel", "arbitrary"),
        name=name,
    )(x, g.reshape(1, k), w)


def _matmul_res_body(a_ref, w_ref, r_ref, o_ref):
    o_ref[...] = r_ref[...] + jnp.dot(a_ref[...].astype(BF16), w_ref[...], preferred_element_type=F32)


def matmul_res(a, w, res, *, tm, tn, name="matmul_res"):
    m, k = a.shape
    n = w.shape[1]
    assert m % tm == 0 and n % tn == 0
    return pl.pallas_call(
        _matmul_res_body,
        grid=(m // tm, n // tn),
        in_specs=[pl.BlockSpec((tm, k), lambda i, j: (i, 0)),
                  pl.BlockSpec((k, tn), lambda i, j: (0, j)),
                  pl.BlockSpec((tm, tn), lambda i, j: (i, j))],
        out_specs=pl.BlockSpec((tm, tn), lambda i, j: (i, j)),
        out_shape=jax.ShapeDtypeStruct((m, n), F32),
        compiler_params=_cparams("parallel", "parallel"),
        name=name,
    )(a, w, res)


def _merge_body(x_ref, g_ref, o0, o1, o2, o3, wg0, wg1, wg2, wg3, bg_ref, wb_ref, wo_ref, y_ref, h_ref):
    @pl.when(pl.program_id(1) == 0)
    def _():
        x = x_ref[...]
        ms = jnp.mean(x * x, axis=-1, keepdims=True)
        h_ref[...] = (x * lax.rsqrt(ms + EPS) * g_ref[...]).astype(BF16)
        y_ref[...] = x

    h = h_ref[...]
    merged = None
    for n, (o_ref, wg) in enumerate(zip((o0, o1, o2, o3), (wg0, wg1, wg2, wg3))):
        gate = _sigmoid(jnp.dot(h, wg[...], preferred_element_type=F32) + bg_ref[n:n + 1, :])
        br = jnp.dot(o_ref[...].astype(BF16), wb_ref[n], preferred_element_type=F32)
        merged = gate * br if merged is None else merged + gate * br
    y_ref[...] += jnp.dot(merged.astype(BF16), wo_ref[...], preferred_element_type=F32)


def merge_out(x, g, branch_outs, w_gate, b_gate, w_branch, w_out, *, tm, tn):
    m, d = x.shape
    nj = d // tn
    wg_specs = [pl.BlockSpec((d, tn), functools.partial(lambda i, j, n: (0, n * nj + j), n=n))
                for n in range(N_BRANCH)]
    return pl.pallas_call(
        _merge_body,
        grid=(m // tm, nj),
        in_specs=[pl.BlockSpec((tm, d), lambda i, j: (i, 0)),
                  pl.BlockSpec((1, d), lambda i, j: (0, 0))]
                 + [pl.BlockSpec((tm, BRANCH_W), lambda i, j: (i, 0))] * N_BRANCH
                 + wg_specs
                 + [pl.BlockSpec((N_BRANCH, tn), lambda i, j: (0, j)),
                    pl.BlockSpec((N_BRANCH, BRANCH_W, tn), lambda i, j: (0, 0, j)),
                    pl.BlockSpec((tn, d), lambda i, j: (j, 0))],
        out_specs=pl.BlockSpec((tm, d), lambda i, j: (i, 0)),
        out_shape=jax.ShapeDtypeStruct((m, d), F32),
        scratch_shapes=[pltpu.VMEM((tm, d), BF16)],
        compiler_params=_cparams("parallel", "arbitrary"),
        name="merge_out",
    )(x, g.reshape(1, d), *branch_outs, w_gate, w_gate, w_gate, w_gate, b_gate.reshape(N_BRANCH, d), w_branch,
      w_out)


DN_Z0 = DN_CONV_CH
DN_AB0 = DN_CONV_CH + BRANCH_W
DN_SLAB = DN_AB0 + 256
INV_BLOCK = 16


def _inv_unit_lower(nmat, c):
    row = _iota2((c, c), 0)
    col = _iota2((c, c), 1)
    eye = (row == col).astype(F32)
    blk = min(INV_BLOCK, c)
    if c > blk:
        same = (row // blk) == (col // blk)
        d = jnp.where(same, nmat, 0.0)
        r = jnp.where(same, 0.0, nmat)
    else:
        d, r = nmat, None
    t = eye - d
    p = d
    k = 2
    while k < blk:
        p = _dot_hp(p, p)
        t = _dot_hp(t, eye + p)
        k *= 2
    if r is None:
        return t
    pm = _dot_hp(t, r)
    t2 = eye - pm
    q = pm
    k = 2
    while k < c // blk:
        q = _dot_hp(q, q)
        t2 = _dot_hp(t2, eye + q)
        k *= 2
    return _dot_hp(t2, t)


def _dn_body(p_ref, abt_ref, cw_ref, alog_ref, dtb_ref, alogt_ref, dtbt_ref, nw_ref, conv0_ref, s0_ref,
             o_ref, s1_ref, cbuf, s_scr, *, c, valid):
    ci = pl.program_id(1)

    @pl.when(ci == 0)
    def _():
        s_scr[...] = s0_ref[0]
        cbuf[0:SUBLANES, :] = conv0_ref[0]

    cbuf[SUBLANES:SUBLANES + c, :] = p_ref[:, 0:DN_CONV_CH]
    conv = cw_ref[0:1, :] * cbuf[5:5 + c, :]
    for j in range(1, DN_CONV):
        conv = conv + cw_ref[j:j + 1, :] * cbuf[5 + j:5 + j + c, :]
    cbuf[0:SUBLANES, :] = cbuf[c:c + SUBLANES, :]
    act = _silu(conv)

    row = _iota2((c, c), 0)
    col = _iota2((c, c), 1)
    incl = row >= col
    strict = row > col
    ltri = incl.astype(BF16)
    utri = (row <= col).astype(BF16)
    vcol = _iota2((c, 1), 0) < valid
    vrow = _iota2((SUBLANES, c), 1) < valid

    ab = p_ref[:, DN_AB0:DN_AB0 + LANES]
    g_all = jnp.where(vcol, -jnp.exp(alog_ref[...]) * _softplus(ab + dtb_ref[...]), 0.0)
    beta_all = jnp.where(vcol, _sigmoid(ab), 0.0)
    gc_all = _dot_hp_exact_lhs(ltri, g_all)
    abt = abt_ref[0]
    g_t = jnp.where(vrow, -jnp.exp(alogt_ref[...]) * _softplus(abt + dtbt_ref[...]), 0.0)
    gc_t = _dot_hp_exact_rhs(g_t, utri)

    for h in range(DN_HEADS):
        q = act[:, h * DN_DK:(h + 1) * DN_DK]
        k = act[:, 512 + h * DN_DK:512 + (h + 1) * DN_DK]
        v = jnp.where(vcol, act[:, 1024 + h * DN_DV:1024 + (h + 1) * DN_DV], 0.0)
        q = q * lax.rsqrt(jnp.sum(q * q, axis=-1, keepdims=True) + EPS) * (DN_DK ** -0.5)
        k = jnp.where(vcol, k * lax.rsqrt(jnp.sum(k * k, axis=-1, keepdims=True) + EPS), 0.0)
        gc = gc_all[:, h:h + 1]
        beta = beta_all[:, DN_HEADS + h:DN_HEADS + h + 1]
        gcr = gc_t[h:h + 1, :]
        gc_last = gc_all[c - 1:c, h:h + 1]
        gamma = jnp.exp(jnp.where(incl, gc - gcr, -jnp.inf))
        kk = _dot_nt(k, k)
        tmat = _inv_unit_lower(jnp.where(strict, beta * kk * gamma, 0.0), c)
        egc = jnp.exp(gc)
        u = _dot(tmat, beta * v)
        w = _dot(tmat, (beta * egc) * k)
        qk = jnp.where(incl, _dot_nt(q, k) * gamma, 0.0)
        s = s_scr[h]
        v_new = u - _dot(w, s)
        o = _dot(q * egc, s) + _dot(qk, v_new)
        s_scr[h] = jnp.exp(gc_last) * s + _dot_tn(k * jnp.exp(gc_last - gc), v_new)
        z = p_ref[:, DN_Z0 + h * DN_DV:DN_Z0 + (h + 1) * DN_DV]
        on = o * lax.rsqrt(jnp.mean(o * o, axis=-1, keepdims=True) + EPS) * nw_ref[...]
        o_ref[:, h * DN_DV:(h + 1) * DN_DV] = (on * _silu(z)).astype(o_ref.dtype)

    @pl.when(ci == pl.num_programs(1) - 1)
    def _():
        s1_ref[0] = s_scr[...]


def deltanet(p_dn, abt, conv_w, a_log, dt_bias, norm_w, conv0, s0, *, nseq, seq_len, c, valid):
    m = nseq * seq_len
    n = seq_len // c
    pad = jnp.zeros((LANES - DN_HEADS,), F32)
    alog = jnp.concatenate([a_log, pad]).reshape(1, LANES)
    dtb = jnp.concatenate([dt_bias, pad]).reshape(1, LANES)
    alog_t = jnp.broadcast_to(jnp.concatenate([a_log, jnp.zeros((4,), F32)])[:, None], (SUBLANES, c))
    dtb_t = jnp.broadcast_to(jnp.concatenate([dt_bias, jnp.zeros((4,), F32)])[:, None], (SUBLANES, c))
    full = lambda shape: pl.BlockSpec(shape, lambda b, i: (0,) * len(shape))
    return pl.pallas_call(
        functools.partial(_dn_body, c=c, valid=valid),
        grid=(nseq, n),
        in_specs=[pl.BlockSpec((c, DN_SLAB), lambda b, i: (b * n + i, 0)),
                  pl.BlockSpec((1, SUBLANES, c), lambda b, i: (b * n + i, 0, 0)),
                  full((DN_CONV, DN_CONV_CH)), full((1, LANES)), full((1, LANES)),
                  full((SUBLANES, c)), full((SUBLANES, c)), full((1, DN_DV)),
                  pl.BlockSpec((1, SUBLANES, DN_CONV_CH), lambda b, i: (b, 0, 0)),
                  pl.BlockSpec((1, DN_HEADS, DN_DK, DN_DV), lambda b, i: (b, 0, 0, 0))],
        out_specs=[pl.BlockSpec((c, BRANCH_W), lambda b, i: (b * n + i, 0)),
                   pl.BlockSpec((1, DN_HEADS, DN_DK, DN_DV), lambda b, i: (b, 0, 0, 0))],
        out_shape=[jax.ShapeDtypeStruct((m, BRANCH_W), F32),
                   jax.ShapeDtypeStruct((nseq, DN_HEADS, DN_DK, DN_DV), F32)],
        scratch_shapes=[pltpu.VMEM((c + 2 * SUBLANES, DN_CONV_CH), F32),
                        pltpu.VMEM((DN_HEADS, DN_DK, DN_DV), F32)],
        compiler_params=_cparams("parallel", "arbitrary"),
        name="deltanet",
    )(p_dn, abt, conv_w, alog, dtb, alog_t, dtb_t, norm_w.reshape(1, DN_DV), conv0, s0)


GLA_K0, GLA_V0, GLA_G0, GLA_GD0 = 256, 512, 1024, 1536
GLA_SLAB = 1792


def _gla_body(q_ref, k_ref, v_ref, g_ref, gd_ref, aup_ref, ab_ref, nw_ref, s0_ref,
              o_ref, s1_ref, st_scr, *, c, valid):
    ci = pl.program_id(2)

    @pl.when(ci == 0)
    def _():
        st_scr[...] = s0_ref[0]

    row = _iota2((c, c), 0)
    col = _iota2((c, c), 1)
    incl = row >= col
    ltri = incl.astype(BF16)
    vcol = _iota2((c, 1), 0) < valid
    lane = _iota2((1, LANES), 1)

    gk = _log_sigmoid(_dot(gd_ref[...], aup_ref[...]) + ab_ref[...]) * (1.0 / GLA_TAU)
    gk = jnp.where(vcol, gk, 0.0)
    b = _dot_hp_exact_lhs(ltri, gk)
    b_last = b[c - 1:c, :]
    q = q_ref[...] * (GLA_DK ** -0.5)
    k = jnp.where(vcol, k_ref[...], 0.0)
    qt = q * jnp.exp(b)
    kt = k * jnp.exp(-b)
    kd = k * jnp.exp(b_last - b)
    blast = jnp.exp(b_last)
    for hh in range(2):
        lm = (lane >= hh * GLA_DK) & (lane < (hh + 1) * GLA_DK)
        qth = jnp.where(lm, qt, 0.0)
        v = jnp.where(vcol, v_ref[:, hh * GLA_DV:(hh + 1) * GLA_DV], 0.0)
        att = jnp.where(incl, _dot_nt(qth, kt), 0.0)
        st = st_scr[hh]
        o = _dot(att, v) + _dot_nt(qth, st)
        st_scr[hh] = st * blast + _dot_tn(v, jnp.where(lm, kd, 0.0))
        on = o * lax.rsqrt(jnp.mean(o * o, axis=-1, keepdims=True) + EPS) * nw_ref[...]
        o_ref[:, hh * GLA_DV:(hh + 1) * GLA_DV] = (on * _silu(g_ref[:, hh * GLA_DV:(hh + 1) * GLA_DV])).astype(o_ref.dtype)

    @pl.when(ci == pl.num_programs(2) - 1)
    def _():
        s1_ref[0] = st_scr[...]


def gla(p_gla, a_up, a_b, norm_w, s0t, *, nseq, seq_len, c, valid):
    m = nseq * seq_len
    n = seq_len // c
    aup = jnp.zeros((LANES, GLA_HEADS * GLA_DK), F32).at[:GLA_LORA].set(a_up).astype(BF16)
    return pl.pallas_call(
        functools.partial(_gla_body, c=c, valid=valid),
        grid=(nseq, 2, n),
        in_specs=[pl.BlockSpec((c, LANES), lambda b, p, i: (b * n + i, p)),
                  pl.BlockSpec((c, LANES), lambda b, p, i: (b * n + i, GLA_K0 // LANES + p)),
                  pl.BlockSpec((c, 2 * GLA_DV), lambda b, p, i: (b * n + i, GLA_V0 // (2 * GLA_DV) + p)),
                  pl.BlockSpec((c, 2 * GLA_DV), lambda b, p, i: (b * n + i, GLA_G0 // (2 * GLA_DV) + p)),
                  pl.BlockSpec((c, LANES), lambda b, p, i: (b * n + i, GLA_GD0 // LANES)),
                  pl.BlockSpec((LANES, LANES), lambda b, p, i: (0, p)),
                  pl.BlockSpec((1, LANES), lambda b, p, i: (0, p)),
                  pl.BlockSpec((1, GLA_DV), lambda b, p, i: (0, 0)),
                  pl.BlockSpec((1, 2, GLA_DV, LANES), lambda b, p, i: (b, p, 0, 0))],
        out_specs=[pl.BlockSpec((c, 2 * GLA_DV), lambda b, p, i: (b * n + i, p)),
                   pl.BlockSpec((1, 2, GLA_DV, LANES), lambda b, p, i: (b, p, 0, 0))],
        out_shape=[jax.ShapeDtypeStruct((m, BRANCH_W), F32),
                   jax.ShapeDtypeStruct((nseq, GLA_HEADS, GLA_DV, LANES), F32)],
        scratch_shapes=[pltpu.VMEM((2, GLA_DV, LANES), F32)],
        compiler_params=_cparams("parallel", "parallel", "arbitrary"),
        name="gla",
    )(p_gla, p_gla, p_gla, p_gla, p_gla, aup, a_b.reshape(1, -1), norm_w.reshape(1, GLA_DV), s0t)


RW_WD0, RW_AD0, RW_GD0 = 1536, 1664, 1792
RW_SLAB = 2048


def _seg_ones(n, seg):
    r = _iota2((n, n), 0) // seg
    c = _iota2((n, n), 1) // seg
    return (r == c).astype(BF16)


def _rw_prep_body(p_ref, prev_ref, mu_ref, w0_ref, a0_ref, kk_ref, ka_ref, rk_ref, wup_ref, aup_ref, gup_ref,
                  r_o, w_o, k_o, v_o, nkk_o, kka_o, gate_o, bonus_o):
    p = p_ref[...]
    xr = p + mu_ref[...] * (prev_ref[...] - p)
    r = xr[:, 0:RW_W]
    k = xr[:, RW_W:2 * RW_W]
    v = xr[:, 2 * RW_W:3 * RW_W]
    wd = xr[:, RW_WD0:RW_WD0 + LANES]
    ad = xr[:, RW_AD0:RW_AD0 + LANES]
    gd = xr[:, RW_GD0:RW_GD0 + RW_LORA_G]
    logw = -jnp.exp(-_softplus(-(w0_ref[...] + _dot(jnp.tanh(wd), wup_ref[...]))) - 0.5)
    a = _sigmoid(a0_ref[...] + _dot(ad, aup_ref[...]))
    seg = _seg_ones(RW_W, RW_N)
    kr = k * kk_ref[...]
    kk = kr * lax.rsqrt(_dot_hp_exact_rhs(kr * kr, seg) + EPS)
    k2 = k * (1.0 + (a - 1.0) * ka_ref[...])
    r_o[...] = r
    w_o[...] = jnp.exp(logw)
    k_o[...] = k2
    v_o[...] = v
    nkk_o[...] = -kk
    kka_o[...] = kk * a
    gate_o[...] = _dot(_sigmoid(gd), gup_ref[...])
    bonus_o[...] = _dot_hp_exact_rhs(r * k2 * rk_ref[...], seg) * v


def rwkv_prep(p_rw, prev, mu, w0, a0, k_k, k_a, r_k, w_up, a_up, g_up, *, tm):
    m = p_rw.shape[0]
    row = lambda w: pl.BlockSpec((tm, w), lambda i: (i, 0))
    full = lambda a: pl.BlockSpec(a.shape, lambda i: (0,) * a.ndim)
    args = (mu.reshape(1, -1), w0.reshape(1, -1), a0.reshape(1, -1), k_k.reshape(1, -1), k_a.reshape(1, -1),
            r_k.reshape(1, -1), w_up, a_up, g_up)
    return pl.pallas_call(
        _rw_prep_body,
        grid=(m // tm,),
        in_specs=[row(RW_SLAB), row(RW_SLAB)] + [full(a) for a in args],
        out_specs=[row(RW_W)] * 8,
        out_shape=[jax.ShapeDtypeStruct((m, RW_W), F32)] * 8,
        compiler_params=_cparams("parallel"),
        name="rwkv_prep",
    )(p_rw, prev, *args)


def _rw_scan_body(r_ref, w_ref, k_ref, v_ref, nkk_ref, kka_ref, s0_ref, y_ref, s1_ref, s_scr, *, nb, t_blk):
    ti = pl.program_id(1)

    @pl.when(ti == 0)
    def _():
        s_scr[...] = s0_ref[...]

    npair = RW_HEADS // 2
    ones = _seg_ones(LANES, RW_N)
    diag = _iota2((RW_N, LANES), 0) == (_iota2((RW_N, LANES), 1) % RW_N)

    def seg_sum(x):
        x1 = x.astype(BF16)
        x2 = (x - x1.astype(F32)).astype(BF16)
        return jnp.dot(x2, ones, preferred_element_type=F32) + jnp.dot(x1, ones, preferred_element_type=F32)

    def step(t, carry):
        for b in range(nb):
            for p in range(npair):
                ls = slice(p * LANES, (p + 1) * LANES)
                rd = lambda ref: ref[b, t, :, ls]
                s = s_scr[b, p]
                sa = seg_sum(s * rd(nkk_ref))
                vb = _dot_hp_exact_rhs(jnp.where(diag, rd(v_ref), 0.0), ones)
                s = s * rd(w_ref) + sa * rd(kka_ref) + vb * rd(k_ref)
                s_scr[b, p] = s
                yb = seg_sum(s * rd(r_ref))
                y_ref[b, t, :, ls] = jnp.sum(jnp.where(diag, yb, 0.0), axis=0, keepdims=True)
        return carry

    lax.fori_loop(0, t_blk, step, 0)

    @pl.when(ti == pl.num_programs(1) - 1)
    def _():
        s1_ref[...] = s_scr[...]


def rwkv_scan(r, w, k, v, nkk, kka, s0p, *, nb, t_blk):
    nseq, L, _ = r.shape
    tok = pl.BlockSpec((nb, t_blk, 1, RW_W), lambda b, t: (b, t, 0, 0))
    st = pl.BlockSpec((nb, RW_HEADS // 2, RW_N, LANES), lambda b, t: (b, 0, 0, 0))
    r, w, k, v, nkk, kka = (a.reshape(nseq, L, 1, RW_W) for a in (r, w, k, v, nkk, kka))
    y, s1 = _rwkv_scan_call(r, w, k, v, nkk, kka, s0p, tok=tok, st=st, nb=nb, t_blk=t_blk)
    return y.reshape(nseq, L, RW_W), s1


def _rwkv_scan_call(r, w, k, v, nkk, kka, s0p, *, tok, st, nb, t_blk):
    nseq, L = r.shape[:2]
    return pl.pallas_call(
        functools.partial(_rw_scan_body, nb=nb, t_blk=t_blk),
        grid=(nseq // nb, L // t_blk),
        in_specs=[tok] * 6 + [st],
        out_specs=[tok, st],
        out_shape=[jax.ShapeDtypeStruct((nseq, L, 1, RW_W), F32),
                   jax.ShapeDtypeStruct((nseq, RW_HEADS // 2, RW_N, LANES), F32)],
        scratch_shapes=[pltpu.VMEM((nb, RW_HEADS // 2, RW_N, LANES), F32)],
        compiler_params=_cparams("parallel", "arbitrary"),
        name="rwkv_scan",
    )(r, w, k, v, nkk, kka, s0p)


def rw_state_to_tiles(s):
    n = s.shape[0]
    return s.reshape(n, RW_HEADS // 2, 2, RW_N, RW_N).transpose(0, 1, 3, 2, 4).reshape(n, RW_HEADS // 2, RW_N, LANES)


def rw_tiles_to_state(t):
    n = t.shape[0]
    return t.reshape(n, RW_HEADS // 2, RW_N, 2, RW_N).transpose(0, 1, 3, 2, 4).reshape(n, RW_HEADS, RW_N, RW_N)


def _rw_post_body(y_ref, bonus_ref, gate_ref, lw_ref, lb_ref, o_ref):
    y = y_ref[...]
    seg = _seg_ones(RW_W, RW_N)
    mu = _dot_hp_exact_rhs(y, seg) * (1.0 / RW_N)
    d = y - mu
    var = _dot_hp_exact_rhs(d * d, seg) * (1.0 / RW_N)
    yn = d * lax.rsqrt(var + RW_GN_EPS) * lw_ref[...] + lb_ref[...]
    o_ref[...] = ((yn + bonus_ref[...]) * gate_ref[...]).astype(o_ref.dtype)


def rwkv_post(y, bonus, gate, ln_w, ln_b, *, tm):
    m = y.shape[0]
    row = pl.BlockSpec((tm, RW_W), lambda i: (i, 0))
    par = pl.BlockSpec((1, RW_W), lambda i: (0, 0))
    return pl.pallas_call(
        _rw_post_body,
        grid=(m // tm,),
        in_specs=[row, row, row, par, par],
        out_specs=row,
        out_shape=jax.ShapeDtypeStruct((m, RW_W), F32),
        compiler_params=_cparams("parallel"),
        name="rwkv_post",
    )(y, bonus, gate, ln_w.reshape(1, -1), ln_b.reshape(1, -1))


FOX_K0, FOX_V0, FOX_F0 = 512, 1024, 1536
FOX_SLAB = 1792
NEG = -1e30


def _fox_prep_body(p_ref, ft_ref, qn_ref, kn_ref, fb_ref, fbt_ref,
                   q_o, k_o, kb_o, vb_o, lf_o, c_o, ct_o, carry, carry_t, *, tm):
    @pl.when(pl.program_id(1) == 0)
    def _():
        carry[...] = jnp.zeros_like(carry)
        carry_t[...] = jnp.zeros_like(carry_t)

    for h in range(FOX_HEADS):
        ls = slice(h * FOX_HD, (h + 1) * FOX_HD)
        q = p_ref[:, ls]
        k = p_ref[:, FOX_K0 + h * FOX_HD:FOX_K0 + (h + 1) * FOX_HD]
        qn = q * lax.rsqrt(jnp.mean(q * q, axis=-1, keepdims=True) + EPS) * qn_ref[...]
        kn = k * lax.rsqrt(jnp.mean(k * k, axis=-1, keepdims=True) + EPS) * kn_ref[...]
        q_o[:, ls] = (qn * (FOX_HD ** -0.5)).astype(BF16)
        k_o[:, ls] = kn
        kb_o[:, ls] = kn.astype(BF16)
    vb_o[...] = p_ref[:, FOX_V0:FOX_V0 + BRANCH_W].astype(BF16)
    lf = _log_sigmoid(p_ref[:, FOX_F0:FOX_F0 + LANES] + fb_ref[...])
    lf_o[...] = lf
    row = _iota2((tm, tm), 0)
    col = _iota2((tm, tm), 1)
    c = _dot_hp_exact_lhs((row >= col).astype(BF16), lf) + carry[...]
    c_o[...] = c
    carry[...] = c[tm - 1:tm, :]
    lft = _log_sigmoid(ft_ref[0] + fbt_ref[...])
    ct = _dot_hp_exact_rhs(lft, (row <= col).astype(BF16)) + carry_t[:, 0:1]
    ct_o[0, 0] = ct
    carry_t[...] = jnp.broadcast_to(ct[:, tm - 1:tm], carry_t.shape)


def fox_prep(p_fox, ft, qn_w, kn_w, fb, *, nseq, seq_len, tm):
    n = seq_len // tm
    m = nseq * seq_len
    fbp = jnp.concatenate([fb, jnp.zeros((LANES - FOX_HEADS,), F32)]).reshape(1, LANES)
    fbt = jnp.broadcast_to(jnp.concatenate([fb, jnp.zeros((4,), F32)])[:, None], (SUBLANES, tm))
    row = lambda w, dt: (pl.BlockSpec((tm, w), lambda b, i: (b * n + i, 0)), jax.ShapeDtypeStruct((m, w), dt))
    outs = [row(BRANCH_W, BF16), row(BRANCH_W, F32), row(BRANCH_W, BF16), row(BRANCH_W, BF16),
            row(LANES, F32), row(LANES, F32),
            (pl.BlockSpec((1, 1, SUBLANES, tm), lambda b, i: (b, i, 0, 0)),
             jax.ShapeDtypeStruct((nseq, n, SUBLANES, tm), F32))]
    return pl.pallas_call(
        functools.partial(_fox_prep_body, tm=tm),
        grid=(nseq, n),
        in_specs=[pl.BlockSpec((tm, FOX_SLAB), lambda b, i: (b * n + i, 0)),
                  pl.BlockSpec((1, SUBLANES, tm), lambda b, i: (b, 0, i)),
                  pl.BlockSpec((1, FOX_HD), lambda b, i: (0, 0)),
                  pl.BlockSpec((1, FOX_HD), lambda b, i: (0, 0)),
                  pl.BlockSpec((1, LANES), lambda b, i: (0, 0)),
                  pl.BlockSpec((SUBLANES, tm), lambda b, i: (0, 0))],
        out_specs=[o[0] for o in outs],
        out_shape=[o[1] for o in outs],
        scratch_shapes=[pltpu.VMEM((1, LANES), F32), pltpu.VMEM((SUBLANES, LANES), F32)],
        compiler_params=_cparams("parallel", "arbitrary"),
        name="fox_prep",
    )(p_fox, ft, qn_w.reshape(1, FOX_HD), kn_w.reshape(1, FOX_HD), fbp, fbt)


def _fox_prompt_body(q_ref, k_ref, v_ref, c_ref, ct_ref, o_ref, *, tq):
    h = pl.program_id(1)
    qi = pl.program_id(2)
    q = q_ref[...]
    lane = _iota2((1, LANES), 1)
    cq = jnp.sum(jnp.where(lane == h, c_ref[...], 0.0), axis=-1, keepdims=True)
    rowh = _iota2((SUBLANES, 1), 0) == h
    qpos = qi * tq + _iota2((tq, tq), 0)

    def kv_step(j, carry):
        m, l, acc = carry
        k = k_ref[pl.ds(pl.multiple_of(j * tq, tq), tq), :]
        v = v_ref[pl.ds(pl.multiple_of(j * tq, tq), tq), :]
        ck = jnp.sum(jnp.where(rowh, ct_ref[0, j], 0.0), axis=0, keepdims=True)
        s = _dot_nt(q, k) + (cq - ck)
        s = jnp.where(qpos >= j * tq + _iota2((tq, tq), 1), s, NEG)
        m_new = jnp.maximum(m, jnp.max(s, axis=-1, keepdims=True))
        p = jnp.exp(s - m_new)
        alpha = jnp.exp(m - m_new)
        return m_new, alpha * l + jnp.sum(p, axis=-1, keepdims=True), alpha * acc + _dot(p, v)

    init = (jnp.full((tq, 1), NEG, F32), jnp.zeros((tq, 1), F32), jnp.zeros((tq, FOX_HD), F32))
    m, l, acc = lax.fori_loop(0, qi + 1, kv_step, init)
    o_ref[...] = (acc / l).astype(o_ref.dtype)


def fox_prompt(qb, kb, vb, c, ct, *, nseq, L, tq):
    nq = L // tq
    return pl.pallas_call(
        functools.partial(_fox_prompt_body, tq=tq),
        grid=(nseq, FOX_HEADS, nq),
        in_specs=[pl.BlockSpec((tq, FOX_HD), lambda b, h, i: (b * nq + i, h)),
                  pl.BlockSpec((L, FOX_HD), lambda b, h, i: (b, h)),
                  pl.BlockSpec((L, FOX_HD), lambda b, h, i: (b, h)),
                  pl.BlockSpec((tq, LANES), lambda b, h, i: (b * nq + i, 0)),
                  pl.BlockSpec((1, nq, SUBLANES, tq), lambda b, h, i: (b, 0, 0, 0))],
        out_specs=pl.BlockSpec((tq, FOX_HD), lambda b, h, i: (b * nq + i, h)),
        out_shape=jax.ShapeDtypeStruct((nseq * L, BRANCH_W), F32),
        compiler_params=_cparams("parallel", "parallel", "arbitrary"),
        name="fox_prompt",
    )(qb, kb, vb, c, ct)


def _fox_sample_body(pt_ref, q_ref, *refs, pps):
    kc = refs[0:pps]
    vc = refs[pps:2 * pps]
    lfc = refs[2 * pps:3 * pps]
    kn_ref, vn_ref, lfn_ref, o_ref, m_scr, l_scr, acc_scr, carry = refs[3 * pps:]
    s_id = pl.program_id(1)
    nq = q_ref.shape[1]

    @pl.when(s_id == 0)
    def _():
        m_scr[...] = jnp.full_like(m_scr, NEG)
        l_scr[...] = jnp.zeros_like(l_scr)
        acc_scr[...] = jnp.zeros_like(acc_scr)
        carry[...] = jnp.zeros_like(carry)

    row = _iota2((PAGE_SIZE, PAGE_SIZE), 0)
    col = _iota2((PAGE_SIZE, PAGE_SIZE), 1)
    utri = (row <= col).astype(BF16)

    def update(h, s, v):
        m = m_scr[h]
        m_new = jnp.maximum(m, jnp.max(s, axis=-1, keepdims=True))
        p = jnp.exp(s - m_new)
        alpha = jnp.exp(m - m_new)
        l_scr[h] = alpha * l_scr[h] + jnp.sum(p, axis=-1, keepdims=True)
        acc_scr[h] = alpha * acc_scr[h] + _dot(p, v)
        m_scr[h] = m_new

    for i in range(pps):
        ct = _dot_hp_exact_rhs(lfc[i][...], utri) + carry[...]
        carry[...] = jnp.broadcast_to(ct[:, PAGE_SIZE - 1:PAGE_SIZE], carry.shape)
        for h in range(FOX_HEADS):
            ls = slice(h * FOX_HD, (h + 1) * FOX_HD)
            s = _dot_nt(q_ref[0, :, ls], kc[i][:, ls]) - ct[h:h + 1, :]
            update(h, s, vc[i][:, ls])

    @pl.when(s_id == pl.num_programs(1) - 1)
    def _():
        ct = _dot_hp_exact_rhs(lfn_ref[0], utri) + carry[...]
        qpos = _iota2((nq, nq), 0)
        kpos = _iota2((nq, nq), 1)
        for h in range(FOX_HEADS):
            ls = slice(h * FOX_HD, (h + 1) * FOX_HD)
            s = _dot_nt(q_ref[0, :, ls], kn_ref[0, :, ls]) - ct[h:h + 1, 0:nq]
            update(h, jnp.where(kpos <= qpos, s, NEG), vn_ref[0, :, ls])
            o_ref[0, :, ls] = acc_scr[h] / l_scr[h]


def fox_sample(page_table, qb, cache_k, cache_v, cache_lft, kn, vn, lfn_t, *, pps):
    nb, nq, _ = qb.shape
    n_pages = page_table.shape[1]
    steps = n_pages // pps
    page = lambda shape, i: pl.BlockSpec(
        (None,) + shape, functools.partial(lambda b, s, pt, i: (pt[b * n_pages + s * pps + i], 0, 0), i=i))
    per_b = lambda shape: pl.BlockSpec((1,) + shape, lambda b, s, pt: (b, 0, 0))
    in_specs = ([per_b((nq, BRANCH_W))]
                + [page((PAGE_SIZE, BRANCH_W), i) for i in range(pps)]
                + [page((PAGE_SIZE, BRANCH_W), i) for i in range(pps)]
                + [page((SUBLANES, PAGE_SIZE), i) for i in range(pps)]
                + [per_b((nq, BRANCH_W)), per_b((nq, BRANCH_W)), per_b((SUBLANES, PAGE_SIZE))])
    grid_spec = pltpu.PrefetchScalarGridSpec(
        num_scalar_prefetch=1, grid=(nb, steps), in_specs=in_specs,
        out_specs=per_b((nq, BRANCH_W)),
        scratch_shapes=[pltpu.VMEM((FOX_HEADS, nq, 1), F32), pltpu.VMEM((FOX_HEADS, nq, 1), F32),
                        pltpu.VMEM((FOX_HEADS, nq, FOX_HD), F32), pltpu.VMEM((SUBLANES, PAGE_SIZE), F32)])
    return pl.pallas_call(
        functools.partial(_fox_sample_body, pps=pps),
        grid_spec=grid_spec,
        out_shape=jax.ShapeDtypeStruct((nb, nq, BRANCH_W), F32),
        compiler_params=_cparams("parallel", "arbitrary"),
        name="fox_sample",
    )(page_table.reshape(-1), qb, *([cache_k] * pps), *([cache_v] * pps), *([cache_lft] * pps), kn, vn, lfn_t)


def _mem_attn_body(q_ref, k_ref, v_ref, o_ref):
    for h in range(MEM_HEADS):
        ls = slice(h * MEM_HD, (h + 1) * MEM_HD)
        s = _dot_nt(q_ref[:, ls], k_ref[0, :, ls]) * (MEM_HD ** -0.5)
        p = jnp.exp(s - jnp.max(s, axis=-1, keepdims=True))
        o_ref[:, ls] = _dot(p, v_ref[0, :, ls]) / jnp.sum(p, axis=-1, keepdims=True)


def mem_attn(q, mk, mv, *, nseq, seq_len, tq):
    m = nseq * seq_len
    n = seq_len // tq
    return pl.pallas_call(
        _mem_attn_body,
        grid=(nseq, n),
        in_specs=[pl.BlockSpec((tq, BRANCH_W), lambda b, i: (b * n + i, 0)),
                  pl.BlockSpec((1, MEM_LEN, BRANCH_W), lambda b, i: (b, 0, 0)),
                  pl.BlockSpec((1, MEM_LEN, BRANCH_W), lambda b, i: (b, 0, 0))],
        out_specs=pl.BlockSpec((tq, BRANCH_W), lambda b, i: (b * n + i, 0)),
        out_shape=jax.ShapeDtypeStruct((m, BRANCH_W), F32),
        compiler_params=_cparams("parallel", "parallel"),
        name="mem_attn",
    )(q, mk, mv)


def gla_state_to_tiles(s):
    st = jnp.swapaxes(s, -1, -2)
    z = jnp.zeros_like(st)
    even = jnp.concatenate([st, z], axis=-1)
    odd = jnp.concatenate([z, st], axis=-1)
    sel = (jnp.arange(GLA_HEADS) % 2 == 1)[None, :, None, None]
    return jnp.where(sel, odd, even)


def gla_tiles_to_state(t):
    even = t[..., :GLA_DK]
    odd = t[..., GLA_DK:]
    sel = (jnp.arange(GLA_HEADS) % 2 == 1)[None, :, None, None]
    return jnp.swapaxes(jnp.where(sel, odd, even), -1, -2)


ROUTER_W = LANES
MOE_TB = 256


def _router_body(x_ref, g_ref, w1_ref, w2_ref, b_ref, lo_ref, h_ref):
    x = x_ref[...]
    h = x * lax.rsqrt(jnp.mean(x * x, axis=-1, keepdims=True) + EPS) * g_ref[...]
    h1 = h.astype(BF16)
    h2 = (h - h1.astype(F32)).astype(BF16)
    d = lambda a, b: jnp.dot(a, b, preferred_element_type=F32)
    lo_ref[...] = (d(h2, w1_ref[...]) + d(h1, w2_ref[...])) + d(h1, w1_ref[...]) + b_ref[...]
    h_ref[...] = h1


def router(x, g, w_hi, w_lo, b, *, tm):
    m, d = x.shape
    full = lambda a: pl.BlockSpec(a.shape, lambda i: (0,) * a.ndim)
    g2 = g.reshape(1, d)
    return pl.pallas_call(
        _router_body,
        grid=(m // tm,),
        in_specs=[pl.BlockSpec((tm, d), lambda i: (i, 0)), full(g2), full(w_hi), full(w_lo), full(b)],
        out_specs=[pl.BlockSpec((tm, ROUTER_W), lambda i: (i, 0)), pl.BlockSpec((tm, d), lambda i: (i, 0))],
        out_shape=[jax.ShapeDtypeStruct((m, ROUTER_W), F32), jax.ShapeDtypeStruct((m, d), BF16)],
        compiler_params=_cparams("parallel"),
        name="router",
    )(x, g2, w_hi, w_lo, b)


def _moe_body(be_ref, nv_ref, xs_ref, w1_ref, w3_ref, w2_ref, o_ref, w1b, w3b, w2b):
    i = pl.program_id(0)

    @pl.when(i < nv_ref[0])
    def _():
        @pl.when((i == 0) | (be_ref[i] != be_ref[jnp.maximum(i - 1, 0)]))
        def _():
            w1b[...] = w1_ref[...].astype(BF16)
            w3b[...] = w3_ref[...].astype(BF16)
            w2b[...] = w2_ref[...].astype(BF16)

        x = xs_ref[...]
        a = jnp.dot(x, w1b[...], preferred_element_type=F32)
        b = jnp.dot(x, w3b[...], preferred_element_type=F32)
        o_ref[...] = jnp.dot((_silu(a) * b).astype(BF16), w2b[...], preferred_element_type=F32)

    @pl.when(i >= nv_ref[0])
    def _():
        o_ref[...] = jnp.zeros_like(o_ref)


def moe_experts(block_e, n_valid, xs, w1, w3, w2):
    n_rows, d = xs.shape
    f = w1.shape[-1]
    grid_spec = pltpu.PrefetchScalarGridSpec(
        num_scalar_prefetch=2, grid=(n_rows // MOE_TB,),
        in_specs=[pl.BlockSpec((MOE_TB, d), lambda i, be, nv: (i, 0)),
                  pl.BlockSpec((None, d, f), lambda i, be, nv: (be[i], 0, 0)),
                  pl.BlockSpec((None, d, f), lambda i, be, nv: (be[i], 0, 0)),
                  pl.BlockSpec((None, f, d), lambda i, be, nv: (be[i], 0, 0))],
        out_specs=pl.BlockSpec((MOE_TB, d), lambda i, be, nv: (i, 0)),
        scratch_shapes=[pltpu.VMEM((d, f), BF16), pltpu.VMEM((d, f), BF16), pltpu.VMEM((f, d), BF16)])
    return pl.pallas_call(
        _moe_body,
        grid_spec=grid_spec,
        out_shape=jax.ShapeDtypeStruct((n_rows, d), F32),
        compiler_params=_cparams("arbitrary"),
        name="moe_experts",
    )(block_e, n_valid, xs, w1, w3, w2)


def _route(logits):
    m = logits.shape[0]
    pg = jax.nn.softmax(logits[:, :N_GROUPS], axis=-1)
    grp = jnp.argmax(pg, axis=-1)
    pg_top = jnp.max(pg, axis=-1)
    le = logits[:, N_GROUPS:N_GROUPS + N_EXPERTS].reshape(m, N_GROUPS, EXPERTS_PER_GROUP)
    pe = jax.nn.softmax(le[jnp.arange(m), grp], axis=-1)
    top_p, top_i = lax.top_k(pe, TOP_K)
    wts = pg_top[:, None] * top_p / jnp.sum(top_p, axis=-1, keepdims=True)
    eid = grp[:, None] * EXPERTS_PER_GROUP + top_i
    return eid, wts


def _sorted_layout(eid, m):
    n_slots = m * TOP_K
    flat_e = eid.reshape(n_slots)
    order = jnp.argsort(flat_e)
    sorted_e = flat_e[order]
    counts = jnp.bincount(flat_e, length=N_EXPERTS)
    padded = (counts + MOE_TB - 1) // MOE_TB * MOE_TB
    pad_end = jnp.cumsum(padded)
    pad_start = pad_end - padded
    start = jnp.cumsum(counts) - counts
    dest = (pad_start[sorted_e] + jnp.arange(n_slots) - start[sorted_e]).astype(jnp.int32)
    n_blocks = -(-n_slots // MOE_TB) + N_EXPERTS
    slot_tok = jnp.full((n_blocks * MOE_TB,), m, jnp.int32).at[dest].set((order // TOP_K).astype(jnp.int32))
    block_e = jnp.minimum(jnp.searchsorted(pad_end, jnp.arange(n_blocks) * MOE_TB, side='right'),
                          N_EXPERTS - 1).astype(jnp.int32)
    n_valid = (pad_end[-1:] // MOE_TB).astype(jnp.int32)
    pos = jnp.zeros((n_slots,), jnp.int32).at[order].set(dest).reshape(m, TOP_K)
    return slot_tok, block_e, n_valid, pos


def _rms_rows_body(x_ref, g_ref, o_ref):
    x = x_ref[...]
    o_ref[...] = x * lax.rsqrt(jnp.mean(x * x, axis=-1, keepdims=True) + EPS) * g_ref[...]


def rms_rows(x, g, *, tm):
    m, d = x.shape
    return pl.pallas_call(
        _rms_rows_body,
        grid=(m // tm,),
        in_specs=[pl.BlockSpec((tm, d), lambda i: (i, 0)), pl.BlockSpec((1, d), lambda i: (0, 0))],
        out_specs=pl.BlockSpec((tm, d), lambda i: (i, 0)),
        out_shape=jax.ShapeDtypeStruct((m, d), F32),
        compiler_params=_cparams("parallel"),
        name="rms_rows",
    )(x, g.reshape(1, d))


def _slab(w, pieces, width):
    cols = []
    for start, stop, padded in pieces:
        cols.append(w[..., start:stop])
        if padded > stop - start:
            cols.append(jnp.zeros(w.shape[:-1] + (padded - (stop - start),), w.dtype))
    out = jnp.concatenate(cols, axis=-1)
    assert out.shape[-1] == width, (out.shape, width)
    return out


_DN0, _RW0, _GLA0, _FOX0 = 0, 2056, 4040, 5592
_DN_PIECES = [(_DN0, _DN0 + 2048, 2048), (_DN0 + 2048, _DN0 + 2056, 256)]
_RW_REL = [(0, 1536, 1536), (1536, 1632, 128), (1632, 1728, 128), (1728, 1984, 256)]
_RW_PIECES = [(_RW0 + a, _RW0 + b, p) for a, b, p in _RW_REL]
_GLA_PIECES = [(_GLA0, _GLA0 + 1536, 1536), (_GLA0 + 1536, _GLA0 + 1552, 256)]
_FOX_PIECES = [(_FOX0, _FOX0 + 1536, 1536), (_FOX0 + 1536, _FOX0 + 1540, 256)]


def _rw_unslab(x):
    return jnp.concatenate([x[..., :1632], x[..., RW_AD0:RW_AD0 + 96], x[..., RW_GD0:]], axis=-1)


def _pad_rows8(x, nseq, seq_len):
    return jnp.pad(x.reshape(nseq, seq_len, x.shape[-1]), ((0, 0), (0, SUBLANES - seq_len), (0, 0)))


def kernel(x_prompt, x_sample, cache_fox_k, cache_fox_v, cache_fox_logf, state_dn_conv, state_dn, state_rw_shift, state_rw, state_gla, cache_mem_k, cache_mem_v, page_table, mem_prompt, norm_mix, w_in, dn_conv_w, dn_A_log, dn_dt_bias, dn_norm_w, rw_mu, rw_w0, rw_w_up, rw_a0, rw_a_up, rw_g_up, rw_k_k, rw_k_a, rw_r_k, rw_ln_w, rw_ln_b, gla_a_up, gla_a_b, gla_norm_w, fox_qn, fox_kn, fox_fb, w_gate, b_gate, w_branch, w_out, norm_mem, norm_memkv, mem_wq, mem_wk, mem_wv, mem_wo, norm_ffn, router_g, router_g_b, router_e, router_e_b, moe_w1, moe_w3, moe_w2, norm_final):
    bp, lp, d = x_prompt.shape
    bs, ls, _ = x_sample.shape
    depth = w_in.shape[0]
    n_p = bp * lp
    n_s = bs * ls
    m = n_p + n_s
    n_phys = cache_fox_k.shape[1]
    n_mem = mem_prompt.shape[1]

    w_dn = _slab(w_in, _DN_PIECES, DN_SLAB).astype(BF16)
    w_rw = _slab(w_in, _RW_PIECES, RW_SLAB).astype(BF16)
    w_gla = _slab(w_in, _GLA_PIECES, GLA_SLAB).astype(BF16)
    w_fox = _slab(w_in, _FOX_PIECES, FOX_SLAB).astype(BF16)
    mu_r = _slab(rw_mu, _RW_REL, RW_SLAB)
    pad_lora = lambda w: jnp.pad(w, ((0, 0), (0, LANES - w.shape[1]), (0, 0))).astype(BF16)
    rw_w_up_b, rw_a_up_b, rw_g_up_b = pad_lora(rw_w_up), pad_lora(rw_a_up), rw_g_up.astype(BF16)
    w_gate_b, w_branch_b, w_out_b = w_gate.astype(BF16), w_branch.astype(BF16), w_out.astype(BF16)
    mem_wq_b, mem_wo_b = mem_wq.astype(BF16), mem_wo.astype(BF16)
    mem_wkv_b = jnp.concatenate([mem_wk, mem_wv], axis=-1).astype(BF16)
    w_router = jnp.pad(jnp.concatenate([router_g, router_e], axis=-1),
                       ((0, 0), (0, 0), (0, ROUTER_W - N_GROUPS - N_EXPERTS)))
    w_router_hi = w_router.astype(BF16)
    w_router_lo = (w_router - w_router_hi.astype(F32)).astype(BF16)
    b_router = jnp.pad(jnp.concatenate([router_g_b, router_e_b], axis=-1),
                       ((0, 0), (0, ROUTER_W - N_GROUPS - N_EXPERTS))).reshape(depth, 1, ROUTER_W)

    x = jnp.concatenate([x_prompt.reshape(n_p, d), x_sample.reshape(n_s, d)], axis=0)
    mem_x = mem_prompt.reshape(bp * n_mem, d)
    zeros = lambda *shape: jnp.zeros(shape, F32)
    rows = lambda a_p, a_s: jnp.concatenate([a_p, a_s], axis=0)
    unpad8 = lambda a: a.reshape(bs, SUBLANES, a.shape[-1])[:, :ls].reshape(n_s, a.shape[-1])
    p_out, s_out = [], []

    for l in range(depth):
        g_mix = norm_mix[l]
        p_dn = rms_matmul(x, g_mix, w_dn[l], tm=512, tn=768, name="in_proj_dn")
        p_rw = rms_matmul(x, g_mix, w_rw[l], tm=512, tn=1024, name="in_proj_rw")
        p_gla = rms_matmul(x, g_mix, w_gla[l], tm=512, tn=896, name="in_proj_gla")
        p_fox = rms_matmul(x, g_mix, w_fox[l], tm=512, tn=896, name="in_proj_fox")

        dn_args = (dn_conv_w[l], dn_A_log[l], dn_dt_bias[l], dn_norm_w[l])
        ab_p = p_dn[:n_p, DN_AB0:DN_AB0 + SUBLANES].reshape(n_p // DN_CHUNK, DN_CHUNK, SUBLANES).transpose(0, 2, 1)
        o_dn_p, dn_p = deltanet(p_dn, ab_p, *dn_args, zeros(bp, SUBLANES, DN_CONV_CH),
                                zeros(bp, DN_HEADS, DN_DK, DN_DV), nseq=bp, seq_len=lp, c=DN_CHUNK, valid=DN_CHUNK)
        dn_slab_s = _pad_rows8(p_dn[n_p:], bs, ls)
        ab_s = dn_slab_s[:, :, DN_AB0:DN_AB0 + SUBLANES].transpose(0, 2, 1)
        conv0_s = jnp.pad(state_dn_conv[l], ((0, 0), (SUBLANES - DN_CONV + 1, 0), (0, 0)))
        o_dn_s, dn_s = deltanet(dn_slab_s.reshape(bs * SUBLANES, DN_SLAB), ab_s, *dn_args, conv0_s, state_dn[l],
                                nseq=bs, seq_len=SUBLANES, c=SUBLANES, valid=ls)
        dn_conv_p = p_dn[:n_p].reshape(bp, lp, DN_SLAB)[:, lp - (DN_CONV - 1):, :DN_CONV_CH]
        dn_conv_s = dn_slab_s[:, ls - (DN_CONV - 1):ls, :DN_CONV_CH]

        rw_p = p_rw[:n_p].reshape(bp, lp, RW_SLAB)
        rw_s = p_rw[n_p:].reshape(bs, ls, RW_SLAB)
        prev_p = jnp.concatenate([zeros(bp, 1, RW_SLAB), rw_p[:, :-1]], axis=1).reshape(n_p, RW_SLAB)
        prev_s = jnp.concatenate([_slab(state_rw_shift[l], _RW_REL, RW_SLAB)[:, None], rw_s[:, :-1]],
                                 axis=1).reshape(n_s, RW_SLAB)
        rw_tok = rwkv_prep(p_rw, rows(prev_p, prev_s), mu_r[l], rw_w0[l], rw_a0[l], rw_k_k[l], rw_k_a[l],
                           rw_r_k[l].reshape(-1), rw_w_up_b[l], rw_a_up_b[l], rw_g_up_b[l], tm=256)
        scan_in, (rw_gate, rw_bonus) = rw_tok[:6], rw_tok[6:]
        y_p, rw_tiles_p = rwkv_scan(*(a[:n_p].reshape(bp, lp, RW_W) for a in scan_in),
                                    zeros(bp, RW_HEADS // 2, RW_N, LANES), nb=bp, t_blk=64)
        y_s, rw_tiles_s = rwkv_scan(*(a[n_p:].reshape(bs, ls, RW_W) for a in scan_in),
                                    rw_state_to_tiles(state_rw[l]), nb=8, t_blk=ls)
        o_rw = rwkv_post(rows(y_p.reshape(n_p, RW_W), y_s.reshape(n_s, RW_W)), rw_bonus, rw_gate,
                         rw_ln_w[l], rw_ln_b[l], tm=512)

        gla_args = (gla_a_up[l], gla_a_b[l], gla_norm_w[l])
        o_gla_p, gla_tiles_p = gla(p_gla, *gla_args, zeros(bp, GLA_HEADS, GLA_DV, LANES),
                                   nseq=bp, seq_len=lp, c=DN_CHUNK, valid=DN_CHUNK)
        gla_slab_s = _pad_rows8(p_gla[n_p:], bs, ls).reshape(bs * SUBLANES, GLA_SLAB)
        o_gla_s, gla_tiles_s = gla(gla_slab_s, *gla_args, gla_state_to_tiles(state_gla[l]),
                                   nseq=bs, seq_len=SUBLANES, c=SUBLANES, valid=ls)

        fox_args = (fox_qn[l], fox_kn[l], fox_fb[l])
        f_cols = p_fox[:, FOX_F0:FOX_F0 + SUBLANES]
        ft_p = f_cols[:n_p].reshape(bp, lp, SUBLANES).transpose(0, 2, 1)
        qb_p, kn_p, kb_p, vb_p, lf_p, c_p, ct_p = fox_prep(p_fox, ft_p, *fox_args, nseq=bp, seq_len=lp, tm=256)
        o_fox_p = fox_prompt(qb_p, kb_p, vb_p, c_p, ct_p, nseq=bp, L=lp, tq=256)
        pf_s = p_fox[n_p:]
        ft_s = f_cols[n_p:].T[None]
        qb_s, kn_s, _, _, lf_s, _, _ = fox_prep(pf_s, ft_s, *fox_args, nseq=1, seq_len=n_s, tm=256)
        lfn_t = jnp.pad(lf_s[:, :SUBLANES].reshape(bs, ls, SUBLANES).transpose(0, 2, 1),
                        ((0, 0), (0, 0), (0, PAGE_SIZE - ls)))
        cache_lft = jnp.pad(cache_fox_logf[l].transpose(0, 2, 1), ((0, 0), (0, SUBLANES - FOX_HEADS), (0, 0)))
        o_fox_s = fox_sample(page_table, _pad_rows8(qb_s, bs, ls),
                             cache_fox_k[l].reshape(n_phys, PAGE_SIZE, BRANCH_W),
                             cache_fox_v[l].reshape(n_phys, PAGE_SIZE, BRANCH_W), cache_lft,
                             _pad_rows8(kn_s, bs, ls), _pad_rows8(pf_s[:, FOX_V0:FOX_V0 + BRANCH_W], bs, ls),
                             lfn_t, pps=4)

        branch_outs = [rows(o_dn_p, unpad8(o_dn_s)), o_rw, rows(o_gla_p, unpad8(o_gla_s)),
                       rows(o_fox_p, o_fox_s[:, :ls].reshape(n_s, BRANCH_W))]
        x = merge_out(x, g_mix, branch_outs, w_gate_b[l], b_gate[l], w_branch_b[l], w_out_b[l], tm=512, tn=256)

        q_mem = rms_matmul(x, norm_mem[l], mem_wq_b[l], tm=512, tn=512, name="mem_q")
        mkv = rms_matmul(mem_x, norm_memkv[l], mem_wkv_b[l], tm=512, tn=512, name="mem_kv")
        mk = mkv[:, :BRANCH_W].reshape(bp, n_mem, BRANCH_W)
        mv = mkv[:, BRANCH_W:].reshape(bp, n_mem, BRANCH_W)
        att_p = mem_attn(q_mem, mk, mv, nseq=bp, seq_len=lp, tq=512)
        att_s = mem_attn(_pad_rows8(q_mem[n_p:], bs, ls).reshape(bs * SUBLANES, BRANCH_W),
                         cache_mem_k[l].reshape(bs, n_mem, BRANCH_W), cache_mem_v[l].reshape(bs, n_mem, BRANCH_W),
                         nseq=bs, seq_len=SUBLANES, tq=SUBLANES)
        x = matmul_res(rows(att_p, unpad8(att_s)), mem_wo_b[l], x, tm=512, tn=1024, name="mem_out")

        logits, h_ffn = router(x, norm_ffn[l], w_router_hi[l], w_router_lo[l], b_router[l], tm=512)
        eid, wts = _route(logits)
        slot_tok, block_e, n_valid, pos = _sorted_layout(eid, m)
        xs = jnp.concatenate([h_ffn, jnp.zeros((1, d), BF16)], axis=0)[slot_tok]
        ys = moe_experts(block_e, n_valid, xs, moe_w1[l], moe_w3[l], moe_w2[l])
        x = x + wts[:, 0:1] * ys[pos[:, 0]] + wts[:, 1:2] * ys[pos[:, 1]]

        p_out.append((dn_conv_p, dn_p, _rw_unslab(rw_p[:, -1]), rw_tiles_to_state(rw_tiles_p),
                      gla_tiles_to_state(gla_tiles_p),
                      kn_p.reshape(bp, lp, FOX_HEADS, FOX_HD),
                      p_fox[:n_p, FOX_V0:FOX_V0 + BRANCH_W].reshape(bp, lp, FOX_HEADS, FOX_HD),
                      lf_p[:, :FOX_HEADS].reshape(bp, lp, FOX_HEADS),
                      mk.reshape(bp, n_mem, MEM_HEADS, MEM_HD), mv.reshape(bp, n_mem, MEM_HEADS, MEM_HD)))
        s_out.append((dn_conv_s, dn_s, _rw_unslab(rw_s[:, -1]), rw_tiles_to_state(rw_tiles_s),
                      gla_tiles_to_state(gla_tiles_s),
                      kn_s.reshape(bs, ls, FOX_HEADS, FOX_HD),
                      pf_s[:, FOX_V0:FOX_V0 + BRANCH_W].reshape(bs, ls, FOX_HEADS, FOX_HD),
                      lf_s[:, :FOX_HEADS].reshape(bs, ls, FOX_HEADS)))

    (p_dn_conv, p_dn_st, p_rw_shift, p_rw_st, p_gla_st, p_fox_k, p_fox_v, p_fox_logf, p_mem_k,
     p_mem_v) = [jnp.stack(r) for r in zip(*p_out)]
    (s_dn_conv, s_dn_st, s_rw_shift, s_rw_st, s_gla_st, s_fox_k, s_fox_v,
     s_fox_logf) = [jnp.stack(r) for r in zip(*s_out)]
    y = rms_rows(x, norm_final, tm=512)
    y_prompt = y[:n_p].reshape(bp, lp, d)
    y_sample = y[n_p:].reshape(bs, ls, d)
    return (y_prompt, y_sample, p_fox_k, p_fox_v, p_fox_logf, p_dn_conv, p_dn_st, p_rw_shift, p_rw_st, p_gla_st,
            p_mem_k, p_mem_v, s_fox_k, s_fox_v, s_fox_logf, s_dn_conv, s_dn_st, s_rw_shift, s_rw_st, s_gla_st)
```

```python
import functools
import math

import jax
import jax.numpy as jnp
from jax import lax
from jax.experimental import pallas as pl
from jax.experimental.pallas import tpu as pltpu

F32 = jnp.float32
BF16 = jnp.bfloat16
EPS = 1e-6
HP = lax.Precision.HIGHEST

D_MODEL = 2048
N_BRANCH = 4
BRANCH_W = 512
PAGE_SIZE = 128

DN_HEADS, DN_DK, DN_DV, DN_CONV, DN_CHUNK = 4, 128, 128, 4, 64
DN_CONV_CH = 1536
RW_HEADS, RW_N, RW_W = 8, 64, 512
RW_LORA_W, RW_LORA_A, RW_LORA_G = 96, 96, 256
RW_GN_EPS = 64e-5
RW_COLS = 1984
GLA_HEADS, GLA_DK, GLA_DV, GLA_LORA, GLA_TAU = 4, 64, 128, 16, 16.0
FOX_HEADS, FOX_HD = 4, 128
MEM_HEADS, MEM_HD, MEM_LEN = 4, 128, 256
N_GROUPS, EXPERTS_PER_GROUP, N_EXPERTS, TOP_K, D_EXPERT = 4, 8, 32, 2, 512

LANES = 128
SUBLANES = 8
VMEM_LIMIT = 56 * 1024 * 1024


def _cparams(*sem):
    return pltpu.CompilerParams(dimension_semantics=sem, vmem_limit_bytes=VMEM_LIMIT)


def _dot(a, b):
    return jnp.dot(a.astype(BF16), b.astype(BF16), preferred_element_type=F32)


def _dot_nt(a, b):
    return lax.dot_general(a.astype(BF16), b.astype(BF16), (((1,), (1,)), ((), ())), preferred_element_type=F32)


def _dot_tn(a, b):
    return lax.dot_general(a.astype(BF16), b.astype(BF16), (((0,), (0,)), ((), ())), preferred_element_type=F32)


def _split3(a):
    a1 = a.astype(BF16)
    r1 = a - a1.astype(F32)
    a2 = r1.astype(BF16)
    a3 = (r1 - a2.astype(F32)).astype(BF16)
    return a1, a2, a3


def _dot_hp3(a, b):
    a1 = a.astype(BF16)
    a2 = (a - a1.astype(F32)).astype(BF16)
    b1 = b.astype(BF16)
    b2 = (b - b1.astype(F32)).astype(BF16)
    d = lambda x, y: jnp.dot(x, y, preferred_element_type=F32)
    return (d(a2, b1) + d(a1, b2)) + d(a1, b1)


def _dot_hp_exact_rhs(a, b_bf16):
    a1, a2, a3 = _split3(a)
    d = lambda x: jnp.dot(x, b_bf16, preferred_element_type=F32)
    return (d(a3) + d(a2)) + d(a1)


def _dot_hp_exact_lhs(a_bf16, b):
    b1, b2, b3 = _split3(b)
    d = lambda y: jnp.dot(a_bf16, y, preferred_element_type=F32)
    return (d(b3) + d(b2)) + d(b1)


def _iota2(shape, axis):
    return lax.broadcasted_iota(jnp.int32, shape, axis)


def _sigmoid(x):
    return 1.0 / (1.0 + jnp.exp(-x))


def _silu(x):
    return x * _sigmoid(x)


def _softplus(x):
    return jnp.maximum(x, 0.0) + jnp.log(1.0 + jnp.exp(-jnp.abs(x)))


def _log_sigmoid(x):
    return -_softplus(-x)


def _rms_matmul_body(x_ref, g_ref, w_ref, o_ref, h_ref):
    @pl.when(pl.program_id(1) == 0)
    def _():
        x = x_ref[...]
        ms = jnp.mean(x * x, axis=-1, keepdims=True)
        h_ref[...] = (x * lax.rsqrt(ms + EPS) * g_ref[...]).astype(BF16)

    o_ref[...] = jnp.dot(h_ref[...], w_ref[...], preferred_element_type=F32).astype(o_ref.dtype)


def rms_matmul(x, g, w, *, tm, tn, out_dtype=F32, name="rms_matmul"):
    m, k = x.shape
    n = w.shape[1]
    assert m % tm == 0 and n % tn == 0
    return pl.pallas_call(
        _rms_matmul_body,
        grid=(m // tm, n // tn),
        in_specs=[pl.BlockSpec((tm, k), lambda i, j: (i, 0)),
                  pl.BlockSpec((1, k), lambda i, j: (0, 0)),
                  pl.BlockSpec((k, tn), lambda i, j: (0, j))],
        out_specs=pl.BlockSpec((tm, tn), lambda i, j: (i, j)),
        out_shape=jax.ShapeDtypeStruct((m, n), out_dtype),
        scratch_shapes=[pltpu.VMEM((tm, k), BF16)],
        compiler_params=_cparams("parallel", "arbitrary"),
        name=name,
    )(x, g.reshape(1, k), w)


def _matmul_res_body(a_ref, w_ref, r_ref, o_ref):
    o_ref[...] = r_ref[...] + jnp.dot(a_ref[...].astype(BF16), w_ref[...], preferred_element_type=F32)


def matmul_res(a, w, res, *, tm, tn, name="matmul_res"):
    m, k = a.shape
    n = w.shape[1]
    assert m % tm == 0 and n % tn == 0
    return pl.pallas_call(
        _matmul_res_body,
        grid=(m // tm, n // tn),
        in_specs=[pl.BlockSpec((tm, k), lambda i, j: (i, 0)),
                  pl.BlockSpec((k, tn), lambda i, j: (0, j)),
                  pl.BlockSpec((tm, tn), lambda i, j: (i, j))],
        out_specs=pl.BlockSpec((tm, tn), lambda i, j: (i, j)),
        out_shape=jax.ShapeDtypeStruct((m, n), F32),
        compiler_params=_cparams("parallel", "parallel"),
        name=name,
    )(a, w, res)


def _merge_body(x_ref, g_ref, o0, o1, o2, o3, wg0, wg1, wg2, wg3, bg_ref, wb_ref, wo_ref, y_ref, h_ref):
    @pl.when(pl.program_id(1) == 0)
    def _():
        x = x_ref[...]
        ms = jnp.mean(x * x, axis=-1, keepdims=True)
        h_ref[...] = (x * lax.rsqrt(ms + EPS) * g_ref[...]).astype(BF16)
        y_ref[...] = x

    h = h_ref[...]
    merged = None
    for n, (o_ref, wg) in enumerate(zip((o0, o1, o2, o3), (wg0, wg1, wg2, wg3))):
        gate = _sigmoid(jnp.dot(h, wg[...], preferred_element_type=F32) + bg_ref[n:n + 1, :])
        br = jnp.dot(o_ref[...].astype(BF16), wb_ref[n], preferred_element_type=F32)
        merged = gate * br if merged is None else merged + gate * br
    y_ref[...] += jnp.dot(merged.astype(BF16), wo_ref[...], preferred_element_type=F32)


def merge_out(x, g, branch_outs, w_gate, b_gate, w_branch, w_out, *, tm, tn):
    m, d = x.shape
    nj = d // tn
    wg_specs = [pl.BlockSpec((d, tn), functools.partial(lambda i, j, n: (0, n * nj + j), n=n))
                for n in range(N_BRANCH)]
    return pl.pallas_call(
        _merge_body,
        grid=(m // tm, nj),
        in_specs=[pl.BlockSpec((tm, d), lambda i, j: (i, 0)),
                  pl.BlockSpec((1, d), lambda i, j: (0, 0))]
                 + [pl.BlockSpec((tm, BRANCH_W), lambda i, j: (i, 0))] * N_BRANCH
                 + wg_specs
                 + [pl.BlockSpec((N_BRANCH, tn), lambda i, j: (0, j)),
                    pl.BlockSpec((N_BRANCH, BRANCH_W, tn), lambda i, j: (0, 0, j)),
                    pl.BlockSpec((tn, d), lambda i, j: (j, 0))],
        out_specs=pl.BlockSpec((tm, d), lambda i, j: (i, 0)),
        out_shape=jax.ShapeDtypeStruct((m, d), F32),
        scratch_shapes=[pltpu.VMEM((tm, d), BF16)],
        compiler_params=_cparams("parallel", "arbitrary"),
        name="merge_out",
    )(x, g.reshape(1, d), *branch_outs, w_gate, w_gate, w_gate, w_gate, b_gate.reshape(N_BRANCH, d), w_branch,
      w_out)


DN_Z0 = DN_CONV_CH
DN_AB0 = DN_CONV_CH + BRANCH_W
DN_SLAB = DN_AB0 + 256
INV_BLOCK = 16


def _inv_unit_lower(nmat, c):
    row = _iota2((c, c), 0)
    col = _iota2((c, c), 1)
    eye = (row == col).astype(F32)
    blk = min(INV_BLOCK, c)
    if c > blk:
        same = (row // blk) == (col // blk)
        d = jnp.where(same, nmat, 0.0)
        r = jnp.where(same, 0.0, nmat)
    else:
        d, r = nmat, None
    t = eye - d
    p = d
    k = 2
    while k < blk:
        p = _dot_hp3(p, p)
        t = _dot_hp3(t, eye + p)
        k *= 2
    if r is None:
        return t
    pm = _dot_hp3(t, r)
    t2 = eye - pm
    q = pm
    k = 2
    while k < c // blk:
        q = _dot_hp3(q, q)
        t2 = _dot_hp3(t2, eye + q)
        k *= 2
    return _dot_hp3(t2, t)


def _dn_body(*refs, nb, c, valid):
    p_refs = refs[0:nb]
    (abt_ref, cw_ref, alog_ref, dtb_ref, alogt_ref, dtbt_ref, nw_ref, conv0_ref, s0_ref,
     o_ref, s1_ref, cbuf, s_scr) = refs[nb:]
    ci = pl.program_id(1)

    @pl.when(ci == 0)
    def _():
        s_scr[...] = s0_ref[...]
        cbuf[:, 0:SUBLANES, :] = conv0_ref[...]

    for sq in range(nb):
        _dn_sequence(p_refs[sq], abt_ref.at[sq], cw_ref, alog_ref, dtb_ref, alogt_ref, dtbt_ref, nw_ref,
                     o_ref.at[sq], cbuf.at[sq], s_scr.at[sq], c=c, valid=valid)

    @pl.when(ci == pl.num_programs(1) - 1)
    def _():
        s1_ref[...] = s_scr[...]


def _dn_sequence(p_ref, abt_ref, cw_ref, alog_ref, dtb_ref, alogt_ref, dtbt_ref, nw_ref, o_ref, cbuf, s_scr,
                 *, c, valid):
    cbuf[SUBLANES:SUBLANES + c, :] = p_ref[:, 0:DN_CONV_CH]
    conv = cw_ref[0:1, :] * cbuf[5:5 + c, :]
    for j in range(1, DN_CONV):
        conv = conv + cw_ref[j:j + 1, :] * cbuf[5 + j:5 + j + c, :]
    cbuf[0:SUBLANES, :] = cbuf[c:c + SUBLANES, :]
    act = _silu(conv)

    row = _iota2((c, c), 0)
    col = _iota2((c, c), 1)
    incl = row >= col
    strict = row > col
    ltri = incl.astype(BF16)
    utri = (row <= col).astype(BF16)
    vcol = _iota2((c, 1), 0) < valid
    vrow = _iota2((SUBLANES, c), 1) < valid

    ab = p_ref[:, DN_AB0:DN_AB0 + LANES]
    g_all = jnp.where(vcol, -jnp.exp(alog_ref[...]) * _softplus(ab + dtb_ref[...]), 0.0)
    beta_all = jnp.where(vcol, _sigmoid(ab), 0.0)
    gc_all = _dot_hp_exact_lhs(ltri, g_all)
    abt = abt_ref[...]
    g_t =jnp.where(vrow, -jnp.exp(alogt_ref[...]) * _softplus(abt + dtbt_ref[...]), 0.0)
    gc_t = _dot_hp_exact_rhs(g_t, utri)

    for h in range(DN_HEADS):
        q = act[:, h * DN_DK:(h + 1) * DN_DK]
        k = act[:, 512 + h * DN_DK:512 + (h + 1) * DN_DK]
        v = jnp.where(vcol, act[:, 1024 + h * DN_DV:1024 + (h + 1) * DN_DV], 0.0)
        q = q * lax.rsqrt(jnp.sum(q * q, axis=-1, keepdims=True) + EPS) * (DN_DK ** -0.5)
        k = jnp.where(vcol, k * lax.rsqrt(jnp.sum(k * k, axis=-1, keepdims=True) + EPS), 0.0)
        gc = gc_all[:, h:h + 1]
        beta = beta_all[:, DN_HEADS + h:DN_HEADS + h + 1]
        gcr = gc_t[h:h + 1, :]
        gc_last = gc_all[c - 1:c, h:h + 1]
        gamma = jnp.exp(jnp.where(incl, gc - gcr, -jnp.inf))
        kk = _dot_nt(k, k)
        tmat = _inv_unit_lower(jnp.where(strict, beta * kk * gamma, 0.0), c)
        egc = jnp.exp(gc)
        u = _dot(tmat, beta * v)
        w = _dot(tmat, (beta * egc) * k)
        qk = jnp.where(incl, _dot_nt(q, k) * gamma, 0.0)
        s = s_scr[h]
        v_new = u - _dot(w, s)
        o = _dot(q * egc, s) + _dot(qk, v_new)
        s_scr[h] = jnp.exp(gc_last) * s + _dot_tn(k * jnp.exp(gc_last - gc), v_new)
        z = p_ref[:, DN_Z0 + h * DN_DV:DN_Z0 + (h + 1) * DN_DV]
        on = o * lax.rsqrt(jnp.mean(o * o, axis=-1, keepdims=True) + EPS) * nw_ref[...]
        o_ref[:, h * DN_DV:(h + 1) * DN_DV] = (on * _silu(z)).astype(o_ref.dtype)


def deltanet(p_dn, abt, conv_w, a_log, dt_bias, norm_w, conv0, s0, *, nseq, seq_len, c, valid, nb):
    m = nseq * seq_len
    n = seq_len // c
    pad = jnp.zeros((LANES - DN_HEADS,), F32)
    alog = jnp.concatenate([a_log, pad]).reshape(1, LANES)
    dtb = jnp.concatenate([dt_bias, pad]).reshape(1, LANES)
    alog_t = jnp.broadcast_to(jnp.concatenate([a_log, jnp.zeros((4,), F32)])[:, None], (SUBLANES, c))
    dtb_t = jnp.broadcast_to(jnp.concatenate([dt_bias, jnp.zeros((4,), F32)])[:, None], (SUBLANES, c))
    full = lambda shape: pl.BlockSpec(shape, lambda b, i: (0,) * len(shape))
    seq_rows = lambda sq: functools.partial(lambda b, i, sq: ((b * nb + sq) * n + i, 0), sq=sq)
    o, s1 = pl.pallas_call(
        functools.partial(_dn_body, nb=nb, c=c, valid=valid),
        grid=(nseq // nb, n),
        in_specs=[pl.BlockSpec((c, DN_SLAB), seq_rows(sq)) for sq in range(nb)]
                 + [pl.BlockSpec((nb, None, SUBLANES, c), lambda b, i: (b, i, 0, 0)),
                    full((DN_CONV, DN_CONV_CH)), full((1, LANES)), full((1, LANES)),
                    full((SUBLANES, c)), full((SUBLANES, c)), full((1, DN_DV)),
                    pl.BlockSpec((nb, SUBLANES, DN_CONV_CH), lambda b, i: (b, 0, 0)),
                    pl.BlockSpec((nb, DN_HEADS, DN_DK, DN_DV), lambda b, i: (b, 0, 0, 0))],
        out_specs=[pl.BlockSpec((nb, c, BRANCH_W), lambda b, i: (b, i, 0)),
                   pl.BlockSpec((nb, DN_HEADS, DN_DK, DN_DV), lambda b, i: (b, 0, 0, 0))],
        out_shape=[jax.ShapeDtypeStruct((nseq, seq_len, BRANCH_W), F32),
                   jax.ShapeDtypeStruct((nseq, DN_HEADS, DN_DK, DN_DV), F32)],
        scratch_shapes=[pltpu.VMEM((nb, c + 2 * SUBLANES, DN_CONV_CH), F32),
                        pltpu.VMEM((nb, DN_HEADS, DN_DK, DN_DV), F32)],
        compiler_params=_cparams("parallel", "arbitrary"),
        name="deltanet",
    )(*([p_dn] * nb), abt.reshape(nseq, n, SUBLANES, c), conv_w, alog, dtb, alog_t, dtb_t,
      norm_w.reshape(1, DN_DV), conv0, s0)
    return o.reshape(m, BRANCH_W), s1


GLA_K0, GLA_V0, GLA_G0, GLA_GD0 = 256, 512, 1024, 1536
GLA_SLAB = 1792


def _gla_body(q_ref, k_ref, v_ref, g_ref, gd_ref, aup_ref, ab_ref, nw_ref, s0_ref,
              o_ref, s1_ref, st_scr, *, c, valid):
    ci = pl.program_id(2)

    @pl.when(ci == 0)
    def _():
        st_scr[...] = s0_ref[0]

    row = _iota2((c, c), 0)
    col = _iota2((c, c), 1)
    incl = row >= col
    ltri = incl.astype(BF16)
    vcol = _iota2((c, 1), 0) < valid
    lane = _iota2((1, LANES), 1)

    gk = _log_sigmoid(_dot(gd_ref[...], aup_ref[...]) + ab_ref[...]) * (1.0 / GLA_TAU)
    gk = jnp.where(vcol, gk, 0.0)
    b = _dot_hp_exact_lhs(ltri, gk)
    b_last = b[c - 1:c, :]
    q = q_ref[...] * (GLA_DK ** -0.5)
    k = jnp.where(vcol, k_ref[...], 0.0)
    qt = q * jnp.exp(b)
    kt = k * jnp.exp(-b)
    kd = k * jnp.exp(b_last - b)
    blast = jnp.exp(b_last)
    for hh in range(2):
        lm = (lane >= hh * GLA_DK) & (lane < (hh + 1) * GLA_DK)
        qth = jnp.where(lm, qt, 0.0)
        v = jnp.where(vcol, v_ref[:, hh * GLA_DV:(hh + 1) * GLA_DV], 0.0)
        att = jnp.where(incl, _dot_nt(qth, kt), 0.0)
        st = st_scr[hh]
        o = _dot(att, v) + _dot_nt(qth, st)
        st_scr[hh] = st * blast + _dot_tn(v, jnp.where(lm, kd, 0.0))
        on = o * lax.rsqrt(jnp.mean(o * o, axis=-1, keepdims=True) + EPS) * nw_ref[...]
        o_ref[:, hh * GLA_DV:(hh + 1) * GLA_DV] = (on * _silu(g_ref[:, hh * GLA_DV:(hh + 1) * GLA_DV])).astype(o_ref.dtype)

    @pl.when(ci == pl.num_programs(2) - 1)
    def _():
        s1_ref[0] = st_scr[...]


def gla(p_gla, a_up, a_b, norm_w, s0t, *, nseq, seq_len, c, valid):
    m = nseq * seq_len
    n = seq_len // c
    aup = jnp.zeros((LANES, GLA_HEADS * GLA_DK), F32).at[:GLA_LORA].set(a_up).astype(BF16)
    return pl.pallas_call(
        functools.partial(_gla_body, c=c, valid=valid),
        grid=(nseq, 2, n),
        in_specs=[pl.BlockSpec((c, LANES), lambda b, p, i: (b * n + i, p)),
                  pl.BlockSpec((c, LANES), lambda b, p, i: (b * n + i, GLA_K0 // LANES + p)),
                  pl.BlockSpec((c, 2 * GLA_DV), lambda b, p, i: (b * n + i, GLA_V0 // (2 * GLA_DV) + p)),
                  pl.BlockSpec((c, 2 * GLA_DV), lambda b, p, i: (b * n + i, GLA_G0 // (2 * GLA_DV) + p)),
                  pl.BlockSpec((c, LANES), lambda b, p, i: (b * n + i, GLA_GD0 // LANES)),
                  pl.BlockSpec((LANES, LANES), lambda b, p, i: (0, p)),
                  pl.BlockSpec((1, LANES), lambda b, p, i: (0, p)),
                  pl.BlockSpec((1, GLA_DV), lambda b, p, i: (0, 0)),
                  pl.BlockSpec((1, 2, GLA_DV, LANES), lambda b, p, i: (b, p, 0, 0))],
        out_specs=[pl.BlockSpec((c, 2 * GLA_DV), lambda b, p, i: (b * n + i, p)),
                   pl.BlockSpec((1, 2, GLA_DV, LANES), lambda b, p, i: (b, p, 0, 0))],
        out_shape=[jax.ShapeDtypeStruct((m, BRANCH_W), F32),
                   jax.ShapeDtypeStruct((nseq, GLA_HEADS, GLA_DV, LANES), F32)],
        scratch_shapes=[pltpu.VMEM((2, GLA_DV, LANES), F32)],
        compiler_params=_cparams("parallel", "parallel", "arbitrary"),
        name="gla",
    )(p_gla, p_gla, p_gla, p_gla, p_gla, aup, a_b.reshape(1, -1), norm_w.reshape(1, GLA_DV), s0t)


RW_WD0, RW_AD0, RW_GD0 = 1536, 1664, 1792
RW_SLAB = 2048
RW_QW = 256


def _seg_ones(n, seg):
    r = _iota2((n, n), 0) // seg
    c = _iota2((n, n), 1) // seg
    return (r == c).astype(BF16)


def _rw_prep_body(p_ref, prev_ref, mu_ref, w0_ref, a0_ref, kk_ref, ka_ref, rk_ref, wup_ref, aup_ref, gup_ref,
                  r_o, w_o, k_o, v_o, nkk_o, kka_o, gate_o, bonus_o):
    p = p_ref[...]
    xr = p + mu_ref[...] * (prev_ref[...] - p)
    r = xr[:, 0:RW_W]
    k = xr[:, RW_W:2 * RW_W]
    v = xr[:, 2 * RW_W:3 * RW_W]
    wd = xr[:, RW_WD0:RW_WD0 + LANES]
    ad = xr[:, RW_AD0:RW_AD0 + LANES]
    gd = xr[:, RW_GD0:RW_GD0 + RW_LORA_G]
    logw = -jnp.exp(-_softplus(-(w0_ref[...] + _dot(jnp.tanh(wd), wup_ref[...]))) - 0.5)
    a = _sigmoid(a0_ref[...] + _dot(ad, aup_ref[...]))
    seg = _seg_ones(RW_W, RW_N)
    kr = k * kk_ref[...]
    kk = kr * lax.rsqrt(_dot_hp_exact_rhs(kr * kr, seg) + EPS)
    k2 = k * (1.0 + (a - 1.0) * ka_ref[...])
    r_o[...] = r
    w_o[...] = jnp.exp(logw)
    k_o[...] = k2
    v_o[...] = v
    nkk_o[...] = -kk
    kka_o[...] = kk * a
    gate_o[...] = _dot(_sigmoid(gd), gup_ref[...])
    bonus_o[...] = _dot_hp_exact_rhs(r * k2 * rk_ref[...], seg) * v


def rwkv_prep(p_rw, prev, mu, w0, a0, k_k, k_a, r_k, w_up, a_up, g_up, *, row0, tm):
    m = prev.shape[0]
    row = lambda w: pl.BlockSpec((tm, w), lambda i: (i, 0))
    full = lambda a: pl.BlockSpec(a.shape, lambda i: (0,) * a.ndim)
    args = (mu.reshape(1, -1), w0.reshape(1, -1), a0.reshape(1, -1), k_k.reshape(1, -1), k_a.reshape(1, -1),
            r_k.reshape(1, -1), w_up, a_up, g_up)
    return pl.pallas_call(
        _rw_prep_body,
        grid=(m // tm,),
        in_specs=[pl.BlockSpec((tm, RW_SLAB), lambda i: (row0 // tm + i, 0)), row(RW_SLAB)]
                 + [full(a) for a in args],
        out_specs=[row(RW_W)] * 8,
        out_shape=[jax.ShapeDtypeStruct((m, RW_W), F32)] * 8,
        compiler_params=_cparams("parallel"),
        name="rwkv_prep",
    )(p_rw, prev, *args)


def _rw_scan_body(r_ref, w_ref, k_ref, v_ref, nkk_ref, kka_ref, s0_ref, y_ref, s1_ref, s_scr, *, nb, t_blk):
    ti = pl.program_id(1)

    @pl.when(ti == 0)
    def _():
        s_scr[...] = s0_ref[...].reshape(s_scr.shape)

    nq = RW_W // RW_QW
    tiles = [(b, q) for b in range(nb) for q in range(nq)]
    ones = _seg_ones(RW_QW, RW_N)
    diag = _iota2((RW_N, RW_QW), 0) == (_iota2((RW_N, RW_QW), 1) % RW_N)
    diag_b = diag.astype(BF16)
    tg = SUBLANES if t_blk % SUBLANES == 0 else t_blk
    dot = lambda a: jnp.dot(a, ones, preferred_element_type=F32)
    stack = lambda xs: jnp.concatenate(xs, axis=0)
    piece = lambda a, i: a[i * RW_N:(i + 1) * RW_N]

    def group(g, carry):
        base = pl.multiple_of(g * tg, tg)
        ld = lambda ref: [[ref[b, pl.ds(base, tg), q * RW_QW:(q + 1) * RW_QW] for q in range(nq)] for b in range(nb)]
        rb, wb, kb, vb, nkkb, kkab = (ld(ref) for ref in (r_ref, w_ref, k_ref, v_ref, nkk_ref, kka_ref))
        yrows = [[] for _ in tiles]
        for j in range(tg):
            row = lambda blk, b, q: blk[b][q][j:j + 1, :]
            s = [s_scr[i] for i in range(len(tiles))]
            prod = stack([s[i] * row(nkkb, b, q) for i, (b, q) in enumerate(tiles)])
            p1 = prod.astype(BF16)
            sa = dot((prod - p1.astype(F32)).astype(BF16)) + dot(p1)
            v1 = [row(vb, b, q).astype(BF16) for b, q in tiles]
            v2 = [(row(vb, b, q) - v1[i].astype(F32)).astype(BF16) for i, (b, q) in enumerate(tiles)]
            vcol = dot(stack([diag_b * x for x in v2])) + dot(stack([diag_b * x for x in v1]))
            s = [s[i] * row(wb, b, q) + piece(sa, i) * row(kkab, b, q) + piece(vcol, i) * row(kb, b, q)
                 for i, (b, q) in enumerate(tiles)]
            for i in range(len(tiles)):
                s_scr[i] = s[i]
            yb = dot(stack([s[i] * row(rb, b, q) for i, (b, q) in enumerate(tiles)]).astype(BF16))
            for i in range(len(tiles)):
                yrows[i].append(jnp.sum(jnp.where(diag, piece(yb, i), 0.0), axis=0, keepdims=True))
        for i, (b, q) in enumerate(tiles):
            y_ref[b, pl.ds(base, tg), q * RW_QW:(q + 1) * RW_QW] = stack(yrows[i])
        return carry

    lax.fori_loop(0, t_blk // tg, group, 0)

    @pl.when(ti == pl.num_programs(1) - 1)
    def _():
        s1_ref[...] = s_scr[...].reshape(s1_ref.shape)


def rwkv_scan(r, w, k, v, nkk, kka, s0q, *, nb, t_blk):
    nseq, L, _ = r.shape
    nq = RW_W // RW_QW
    tok = pl.BlockSpec((nb, t_blk, RW_W), lambda b, t: (b, t, 0))
    st = pl.BlockSpec((nb, nq, RW_N, RW_QW), lambda b, t: (b, 0, 0, 0))
    return pl.pallas_call(
        functools.partial(_rw_scan_body, nb=nb, t_blk=t_blk),
        grid=(nseq // nb, L // t_blk),
        in_specs=[tok] * 6 + [st],
        out_specs=[tok, st],
        out_shape=[jax.ShapeDtypeStruct((nseq, L, RW_W), F32),
                   jax.ShapeDtypeStruct((nseq, nq, RW_N, RW_QW), F32)],
        scratch_shapes=[pltpu.VMEM((nb * nq, RW_N, RW_QW), F32)],
        compiler_params=_cparams("parallel", "arbitrary"),
        name="rwkv_scan",
    )(r, w, k, v, nkk, kka, s0q)


def rw_state_to_tiles(s):
    n = s.shape[0]
    nq = RW_W // RW_QW
    hq = RW_QW // RW_N
    return s.reshape(n, nq, hq, RW_N, RW_N).transpose(0, 1, 3, 2, 4).reshape(n, nq, RW_N, RW_QW)


def rw_tiles_to_state(t):
    n = t.shape[0]
    nq = RW_W // RW_QW
    hq = RW_QW // RW_N
    return t.reshape(n, nq, RW_N, hq, RW_N).transpose(0, 1, 3, 2, 4).reshape(n, RW_HEADS, RW_N, RW_N)


def _rw_post_body(y_ref, bonus_ref, gate_ref, lw_ref, lb_ref, o_ref):
    y = y_ref[...]
    seg = _seg_ones(RW_W, RW_N)
    mu = _dot_hp_exact_rhs(y, seg) * (1.0 / RW_N)
    d = y - mu
    var = _dot_hp_exact_rhs(d * d, seg) * (1.0 / RW_N)
    yn = d * lax.rsqrt(var + RW_GN_EPS) * lw_ref[...] + lb_ref[...]
    o_ref[...] = ((yn + bonus_ref[...]) * gate_ref[...]).astype(o_ref.dtype)


def rwkv_post(y, bonus, gate, ln_w, ln_b, *, tm):
    m = y.shape[0]
    row = pl.BlockSpec((tm, RW_W), lambda i: (i, 0))
    par = pl.BlockSpec((1, RW_W), lambda i: (0, 0))
    return pl.pallas_call(
        _rw_post_body,
        grid=(m // tm,),
        in_specs=[row, row, row, par, par],
        out_specs=row,
        out_shape=jax.ShapeDtypeStruct((m, RW_W), F32),
        compiler_params=_cparams("parallel"),
        name="rwkv_post",
    )(y, bonus, gate, ln_w.reshape(1, -1), ln_b.reshape(1, -1))


FOX_K0, FOX_V0, FOX_F0 = 512, 1024, 1536
FOX_SLAB = 1792
NEG = -1e30


def _fox_prep_body(p_ref, ft_ref, qn_ref, kn_ref, fb_ref, fbt_ref,
                   q_o, k_o, kb_o, vb_o, lf_o, c_o, ct_o, carry, carry_t, *, tm):
    @pl.when(pl.program_id(1) == 0)
    def _():
        carry[...] = jnp.zeros_like(carry)
        carry_t[...] = jnp.zeros_like(carry_t)

    for h in range(FOX_HEADS):
        ls = slice(h * FOX_HD, (h + 1) * FOX_HD)
        q = p_ref[:, ls]
        k = p_ref[:, FOX_K0 + h * FOX_HD:FOX_K0 + (h + 1) * FOX_HD]
        qn = q * lax.rsqrt(jnp.mean(q * q, axis=-1, keepdims=True) + EPS) * qn_ref[...]
        kn = k * lax.rsqrt(jnp.mean(k * k, axis=-1, keepdims=True) + EPS) * kn_ref[...]
        q_o[:, ls] = (qn * (FOX_HD ** -0.5)).astype(BF16)
        k_o[:, ls] = kn
        kb_o[:, ls] = kn.astype(BF16)
    vb_o[...] = p_ref[:, FOX_V0:FOX_V0 + BRANCH_W].astype(BF16)
    lf = _log_sigmoid(p_ref[:, FOX_F0:FOX_F0 + LANES] + fb_ref[...])
    lf_o[...] = lf
    row = _iota2((tm, tm), 0)
    col = _iota2((tm, tm), 1)
    c = _dot_hp_exact_lhs((row >= col).astype(BF16), lf) + carry[...]
    c_o[...] = c
    carry[...] = c[tm - 1:tm, :]
    lft = _log_sigmoid(ft_ref[0] + fbt_ref[...])
    ct = _dot_hp_exact_rhs(lft, (row <= col).astype(BF16)) + carry_t[:, 0:1]
    ct_o[0, 0] = ct
    carry_t[...] = jnp.broadcast_to(ct[:, tm - 1:tm], carry_t.shape)


def fox_prep(p_fox, ft, qn_w, kn_w, fb, *, nseq, seq_len, tm):
    n = seq_len // tm
    m = nseq * seq_len
    fbp = jnp.concatenate([fb, jnp.zeros((LANES - FOX_HEADS,), F32)]).reshape(1, LANES)
    fbt = jnp.broadcast_to(jnp.concatenate([fb, jnp.zeros((4,), F32)])[:, None], (SUBLANES, tm))
    row = lambda w, dt: (pl.BlockSpec((tm, w), lambda b, i: (b * n + i, 0)), jax.ShapeDtypeStruct((m, w), dt))
    outs = [row(BRANCH_W, BF16), row(BRANCH_W, F32), row(BRANCH_W, BF16), row(BRANCH_W, BF16),
            row(LANES, F32), row(LANES, F32),
            (pl.BlockSpec((1, 1, SUBLANES, tm), lambda b, i: (b, i, 0, 0)),
             jax.ShapeDtypeStruct((nseq, n, SUBLANES, tm), F32))]
    return pl.pallas_call(
        functools.partial(_fox_prep_body, tm=tm),
        grid=(nseq, n),
        in_specs=[pl.BlockSpec((tm, FOX_SLAB), lambda b, i: (b * n + i, 0)),
                  pl.BlockSpec((1, SUBLANES, tm), lambda b, i: (b, 0, i)),
                  pl.BlockSpec((1, FOX_HD), lambda b, i: (0, 0)),
                  pl.BlockSpec((1, FOX_HD), lambda b, i: (0, 0)),
                  pl.BlockSpec((1, LANES), lambda b, i: (0, 0)),
                  pl.BlockSpec((SUBLANES, tm), lambda b, i: (0, 0))],
        out_specs=[o[0] for o in outs],
        out_shape=[o[1] for o in outs],
        scratch_shapes=[pltpu.VMEM((1, LANES), F32), pltpu.VMEM((SUBLANES, LANES), F32)],
        compiler_params=_cparams("parallel", "arbitrary"),
        name="fox_prep",
    )(p_fox, ft, qn_w.reshape(1, FOX_HD), kn_w.reshape(1, FOX_HD), fbp, fbt)


def _fox_prompt_body(q_ref, k_ref, v_ref, c_ref, ct_ref, o_ref, *, tq):
    h = pl.program_id(1)
    qi = pl.program_id(2)
    q = q_ref[...]
    lane = _iota2((1, LANES), 1)
    cq = jnp.sum(jnp.where(lane == h, c_ref[...], 0.0), axis=-1, keepdims=True)
    rowh = _iota2((SUBLANES, 1), 0) == h
    qpos = qi * tq + _iota2((tq, tq), 0)

    def kv_step(j, carry):
        m, l, acc = carry
        k = k_ref[pl.ds(pl.multiple_of(j * tq, tq), tq), :]
        v = v_ref[pl.ds(pl.multiple_of(j * tq, tq), tq), :]
        ck = jnp.sum(jnp.where(rowh, ct_ref[0, j], 0.0), axis=0, keepdims=True)
        s = _dot_nt(q, k) + (cq - ck)
        s = jnp.where(qpos >= j * tq + _iota2((tq, tq), 1), s, NEG)
        m_new = jnp.maximum(m, jnp.max(s, axis=-1, keepdims=True))
        p = jnp.exp(s - m_new)
        alpha = jnp.exp(m - m_new)
        return m_new, alpha * l + jnp.sum(p, axis=-1, keepdims=True), alpha * acc + _dot(p, v)

    init = (jnp.full((tq, 1), NEG, F32), jnp.zeros((tq, 1), F32), jnp.zeros((tq, FOX_HD), F32))
    m, l, acc = lax.fori_loop(0, qi + 1, kv_step, init)
    o_ref[...] = (acc / l).astype(o_ref.dtype)


def fox_prompt(qb, kb, vb, c, ct, *, nseq, L, tq):
    nq = L // tq
    return pl.pallas_call(
        functools.partial(_fox_prompt_body, tq=tq),
        grid=(nseq, FOX_HEADS, nq),
        in_specs=[pl.BlockSpec((tq, FOX_HD), lambda b, h, i: (b * nq + i, h)),
                  pl.BlockSpec((L, FOX_HD), lambda b, h, i: (b, h)),
                  pl.BlockSpec((L, FOX_HD), lambda b, h, i: (b, h)),
                  pl.BlockSpec((tq, LANES), lambda b, h, i: (b * nq + i, 0)),
                  pl.BlockSpec((1, nq, SUBLANES, tq), lambda b, h, i: (b, 0, 0, 0))],
        out_specs=pl.BlockSpec((tq, FOX_HD), lambda b, h, i: (b * nq + i, h)),
        out_shape=jax.ShapeDtypeStruct((nseq * L, BRANCH_W), F32),
        compiler_params=_cparams("parallel", "parallel", "arbitrary"),
        name="fox_prompt",
    )(qb, kb, vb, c, ct)


PAGE_ROWS = PAGE_SIZE * FOX_HEADS
NEW_ROWS = 32
LF_ROWS = 24


def _fox_sample_body(pt_ref, q_ref, *refs, n_pages, n_new):
    kc = refs[0:n_pages]
    vc = refs[n_pages:2 * n_pages]
    lfc = refs[2 * n_pages:3 * n_pages]
    kn_ref, vn_ref, lfn_ref, o_ref, s_scr = refs[3 * n_pages:]
    nq = q_ref.shape[1]
    nr = FOX_HEADS * nq
    q_all = jnp.concatenate([q_ref[0, :, h * FOX_HD:(h + 1) * FOX_HD] for h in range(FOX_HEADS)], axis=0)

    r = _iota2((PAGE_ROWS, PAGE_ROWS), 0)
    c = _iota2((PAGE_ROWS, PAGE_ROWS), 1)
    same_head = (r % FOX_HEADS) == (c % FOX_HEADS)
    cum_mat = (same_head & (r // FOX_HEADS <= c // FOX_HEADS)).astype(BF16)
    tot_mat = same_head.astype(BF16)
    lf_rows = jnp.concatenate([lfc[i][...] for i in range(n_pages)] + [lfn_ref[0]]
                              + [jnp.zeros((LF_ROWS - n_pages - 1, PAGE_ROWS), F32)], axis=0)
    earlier = (_iota2((LF_ROWS, LF_ROWS), 0) > _iota2((LF_ROWS, LF_ROWS), 1)).astype(BF16)
    bias = (_dot_hp_exact_rhs(lf_rows, cum_mat)
            + _dot_hp_exact_lhs(earlier, _dot_hp_exact_rhs(lf_rows, tot_mat)))

    head_ok = (_iota2((nr, PAGE_ROWS), 0) // nq) == (_iota2((nr, PAGE_ROWS), 1) % FOX_HEADS)
    m = jnp.full((nr, PAGE_ROWS), NEG, F32)
    for i in range(n_pages):
        s = jnp.where(head_ok, _dot_nt(q_all, kc[i][...]) - bias[i:i + 1, :], NEG)
        s_scr[i] = s
        m = jnp.maximum(m, s)
    kr = _iota2((nr, NEW_ROWS), 1)
    qr = _iota2((nr, NEW_ROWS), 0)
    new_ok = ((qr // nq) == (kr % FOX_HEADS)) & (kr // FOX_HEADS <= qr % nq) & (kr < n_new * FOX_HEADS)
    s_new = jnp.where(new_ok, _dot_nt(q_all, kn_ref[0]) - bias[n_pages:n_pages + 1, 0:NEW_ROWS], NEG)
    m = jnp.maximum(jnp.max(m, axis=-1, keepdims=True), jnp.max(s_new, axis=-1, keepdims=True))

    p_new = jnp.exp(s_new - m)
    l = jnp.sum(p_new, axis=-1, keepdims=True)
    acc = _dot(p_new, vn_ref[0])
    psum = jnp.zeros((nr, PAGE_ROWS), F32)
    for i in range(n_pages):
        p = jnp.exp(s_scr[i] - m)
        psum = psum + p
        acc = acc + _dot(p, vc[i][...])
    out = acc / (l + jnp.sum(psum, axis=-1, keepdims=True))
    for h in range(FOX_HEADS):
        o_ref[0, :, h * FOX_HD:(h + 1) * FOX_HD] = out[h * nq:(h + 1) * nq]


def fox_sample(page_table, qb, cache_k, cache_v, cache_lf, kn, vn, lfn, *, layer, n_new):
    nb, nq, _ = qb.shape
    n_pages = page_table.shape[1]
    page = lambda shape, i: pl.BlockSpec(
        (None, None) + shape, functools.partial(lambda b, pt, i: (layer, pt[b * n_pages + i], 0, 0), i=i))
    per_b = lambda shape: pl.BlockSpec((1,) + shape, lambda b, pt: (b, 0, 0))
    in_specs = ([per_b((nq, BRANCH_W))]
                + [page((PAGE_ROWS, FOX_HD), i) for i in range(n_pages)]
                + [page((PAGE_ROWS, FOX_HD), i) for i in range(n_pages)]
                + [page((1, PAGE_ROWS), i) for i in range(n_pages)]
                + [per_b((NEW_ROWS, FOX_HD)), per_b((NEW_ROWS, FOX_HD)), per_b((1, PAGE_ROWS))])
    grid_spec = pltpu.PrefetchScalarGridSpec(
        num_scalar_prefetch=1, grid=(nb,), in_specs=in_specs, out_specs=per_b((nq, BRANCH_W)),
        scratch_shapes=[pltpu.VMEM((n_pages, FOX_HEADS * nq, PAGE_ROWS), F32)])
    return pl.pallas_call(
        functools.partial(_fox_sample_body, n_pages=n_pages, n_new=n_new),
        grid_spec=grid_spec,
        out_shape=jax.ShapeDtypeStruct((nb, nq, BRANCH_W), F32),
        compiler_params=_cparams("parallel"),
        name="fox_sample",
    )(page_table.reshape(-1), qb, *([cache_k] * n_pages), *([cache_v] * n_pages), *([cache_lf] * n_pages),
      kn, vn, lfn)


def _mem_attn_body(q_ref, k_ref, v_ref, o_ref):
    for h in range(MEM_HEADS):
        ls = slice(h * MEM_HD, (h + 1) * MEM_HD)
        s = _dot_nt(q_ref[:, ls], k_ref[0, :, ls]) * (MEM_HD ** -0.5)
        p = jnp.exp(s - jnp.max(s, axis=-1, keepdims=True))
        o_ref[:, ls] = _dot(p, v_ref[0, :, ls]) / jnp.sum(p, axis=-1, keepdims=True)


def mem_attn(q, mk, mv, *, nseq, seq_len, tq):
    m = nseq * seq_len
    n = seq_len // tq
    kv = pl.BlockSpec((1, MEM_LEN, BRANCH_W), lambda b, i: (b, 0, 0))
    return pl.pallas_call(
        _mem_attn_body,
        grid=(nseq, n),
        in_specs=[pl.BlockSpec((tq, BRANCH_W), lambda b, i: (b * n + i, 0)), kv, kv],
        out_specs=pl.BlockSpec((tq, BRANCH_W), lambda b, i: (b * n + i, 0)),
        out_shape=jax.ShapeDtypeStruct((m, BRANCH_W), F32),
        compiler_params=_cparams("parallel", "parallel"),
        name="mem_attn",
    )(q, mk, mv)


def _mem_attn_rows_body(q_ref, k_ref, v_ref, o_ref):
    nq = q_ref.shape[0]
    nr = MEM_HEADS * nq
    nk = MEM_LEN * MEM_HEADS
    q_all = jnp.concatenate([q_ref[:, h * MEM_HD:(h + 1) * MEM_HD] for h in range(MEM_HEADS)], axis=0)
    head_ok = (_iota2((nr, nk), 0) // nq) == (_iota2((nr, nk), 1) % MEM_HEADS)
    s = jnp.where(head_ok, _dot_nt(q_all, k_ref[...]) * (MEM_HD ** -0.5), NEG)
    p = jnp.exp(s - jnp.max(s, axis=-1, keepdims=True))
    out = _dot(p, v_ref[...]) / jnp.sum(p, axis=-1, keepdims=True)
    for h in range(MEM_HEADS):
        o_ref[:, h * MEM_HD:(h + 1) * MEM_HD] = out[h * nq:(h + 1) * nq]


def mem_attn_rows(q, mk, mv, *, layer, nseq, tq):
    kv = pl.BlockSpec((None, None, MEM_LEN * MEM_HEADS, MEM_HD), lambda b: (layer, b, 0, 0))
    return pl.pallas_call(
        _mem_attn_rows_body,
        grid=(nseq,),
        in_specs=[pl.BlockSpec((tq, BRANCH_W), lambda b: (b, 0)), kv, kv],
        out_specs=pl.BlockSpec((tq, BRANCH_W), lambda b: (b, 0)),
        out_shape=jax.ShapeDtypeStruct((nseq * tq, BRANCH_W), F32),
        compiler_params=_cparams("parallel"),
        name="mem_attn_rows",
    )(q, mk, mv)


def gla_state_to_tiles(s):
    st = jnp.swapaxes(s, -1, -2)
    z = jnp.zeros_like(st)
    even = jnp.concatenate([st, z], axis=-1)
    odd = jnp.concatenate([z, st], axis=-1)
    sel = (jnp.arange(GLA_HEADS) % 2 == 1)[None, :, None, None]
    return jnp.where(sel, odd, even)


def gla_tiles_to_state(t):
    even = t[..., :GLA_DK]
    odd = t[..., GLA_DK:]
    sel = (jnp.arange(GLA_HEADS) % 2 == 1)[None, :, None, None]
    return jnp.swapaxes(jnp.where(sel, odd, even), -1, -2)


ROUTER_W = LANES
MOE_TB = 256


def _router_body(x_ref, g_ref, w1_ref, w2_ref, b_ref, lo_ref, h_ref):
    x = x_ref[...]
    h = x * lax.rsqrt(jnp.mean(x * x, axis=-1, keepdims=True) + EPS) * g_ref[...]
    h1 = h.astype(BF16)
    h2 = (h - h1.astype(F32)).astype(BF16)
    d = lambda a, b: jnp.dot(a, b, preferred_element_type=F32)
    lo_ref[...] = (d(h2, w1_ref[...]) + d(h1, w2_ref[...])) + d(h1, w1_ref[...]) + b_ref[...]
    h_ref[...] = h1


def router(x, g, w_hi, w_lo, b, *, tm):
    m, d = x.shape
    full = lambda a: pl.BlockSpec(a.shape, lambda i: (0,) * a.ndim)
    g2 = g.reshape(1, d)
    return pl.pallas_call(
        _router_body,
        grid=(m // tm,),
        in_specs=[pl.BlockSpec((tm, d), lambda i: (i, 0)), full(g2), full(w_hi), full(w_lo), full(b)],
        out_specs=[pl.BlockSpec((tm, ROUTER_W), lambda i: (i, 0)), pl.BlockSpec((tm, d), lambda i: (i, 0))],
        out_shape=[jax.ShapeDtypeStruct((m, ROUTER_W), F32), jax.ShapeDtypeStruct((m, d), BF16)],
        compiler_params=_cparams("parallel"),
        name="router",
    )(x, g2, w_hi, w_lo, b)


def _moe_body(be_ref, nv_ref, xs_ref, w1_ref, w3_ref, w2_ref, o_ref, w1b, w3b, w2b):
    i = pl.program_id(0)

    @pl.when(i < nv_ref[0])
    def _():
        @pl.when((i == 0) | (be_ref[i] != be_ref[jnp.maximum(i - 1, 0)]))
        def _():
            w1b[...] = w1_ref[...].astype(BF16)
            w3b[...] = w3_ref[...].astype(BF16)
            w2b[...] = w2_ref[...].astype(BF16)

        x = xs_ref[...]
        a = jnp.dot(x, w1b[...], preferred_element_type=F32)
        b = jnp.dot(x, w3b[...], preferred_element_type=F32)
        o_ref[...] = jnp.dot((_silu(a) * b).astype(BF16), w2b[...], preferred_element_type=F32)

    @pl.when(i >= nv_ref[0])
    def _():
        o_ref[...] = jnp.zeros_like(o_ref)


def moe_experts(block_e, n_valid, xs, w1, w3, w2, *, layer):
    n_rows, d = xs.shape
    f = w1.shape[-1]
    grid_spec = pltpu.PrefetchScalarGridSpec(
        num_scalar_prefetch=2, grid=(n_rows // MOE_TB,),
        in_specs=[pl.BlockSpec((MOE_TB, d), lambda i, be, nv: (i, 0)),
                  pl.BlockSpec((None, None, d, f), lambda i, be, nv: (layer, be[i], 0, 0)),
                  pl.BlockSpec((None, None, d, f), lambda i, be, nv: (layer, be[i], 0, 0)),
                  pl.BlockSpec((None, None, f, d), lambda i, be, nv: (layer, be[i], 0, 0))],
        out_specs=pl.BlockSpec((MOE_TB, d), lambda i, be, nv: (i, 0)),
        scratch_shapes=[pltpu.VMEM((d, f), BF16), pltpu.VMEM((d, f), BF16), pltpu.VMEM((f, d), BF16)])
    return pl.pallas_call(
        _moe_body,
        grid_spec=grid_spec,
        out_shape=jax.ShapeDtypeStruct((n_rows, d), F32),
        compiler_params=_cparams("arbitrary"),
        name="moe_experts",
    )(block_e, n_valid, xs, w1, w3, w2)


def _route(logits):
    m = logits.shape[0]
    pg = jax.nn.softmax(logits[:, :N_GROUPS], axis=-1)
    grp = jnp.argmax(pg, axis=-1)
    pg_top = jnp.max(pg, axis=-1)
    le = logits[:, N_GROUPS:N_GROUPS + N_EXPERTS].reshape(m, N_GROUPS, EXPERTS_PER_GROUP)
    pe = jax.nn.softmax(le[jnp.arange(m), grp], axis=-1)
    top_p, top_i = lax.top_k(pe, TOP_K)
    wts = pg_top[:, None] * top_p / jnp.sum(top_p, axis=-1, keepdims=True)
    eid = grp[:, None] * EXPERTS_PER_GROUP + top_i
    return eid, wts


def _sorted_layout(eid, m):
    n_slots = m * TOP_K
    flat_e = eid.reshape(n_slots)
    order = jnp.argsort(flat_e)
    sorted_e = flat_e[order]
    counts = jnp.bincount(flat_e, length=N_EXPERTS)
    padded = (counts + MOE_TB - 1) // MOE_TB * MOE_TB
    pad_end = jnp.cumsum(padded)
    pad_start = pad_end - padded
    start = jnp.cumsum(counts) - counts
    dest = (pad_start[sorted_e] + jnp.arange(n_slots) - start[sorted_e]).astype(jnp.int32)
    n_blocks = -(-n_slots // MOE_TB) + N_EXPERTS
    slot_tok = jnp.full((n_blocks * MOE_TB,), m, jnp.int32).at[dest].set((order // TOP_K).astype(jnp.int32))
    block_e = jnp.minimum(jnp.searchsorted(pad_end, jnp.arange(n_blocks) * MOE_TB, side='right'),
                          N_EXPERTS - 1).astype(jnp.int32)
    n_valid = (pad_end[-1:] // MOE_TB).astype(jnp.int32)
    pos = jnp.zeros((n_slots,), jnp.int32).at[order].set(dest).reshape(m, TOP_K)
    return slot_tok, block_e, n_valid, pos


def _rms_rows_body(x_ref, g_ref, o_ref):
    x = x_ref[...]
    o_ref[...] = x * lax.rsqrt(jnp.mean(x * x, axis=-1, keepdims=True) + EPS) * g_ref[...]


def rms_rows(x, g, *, tm):
    m, d = x.shape
    return pl.pallas_call(
        _rms_rows_body,
        grid=(m // tm,),
        in_specs=[pl.BlockSpec((tm, d), lambda i: (i, 0)), pl.BlockSpec((1, d), lambda i: (0, 0))],
        out_specs=pl.BlockSpec((tm, d), lambda i: (i, 0)),
        out_shape=jax.ShapeDtypeStruct((m, d), F32),
        compiler_params=_cparams("parallel"),
        name="rms_rows",
    )(x, g.reshape(1, d))


def _slab(w, pieces, width):
    cols = []
    for start, stop, padded in pieces:
        cols.append(w[..., start:stop])
        if padded > stop - start:
            cols.append(jnp.zeros(w.shape[:-1] + (padded - (stop - start),), w.dtype))
    out = jnp.concatenate(cols, axis=-1)
    assert out.shape[-1] == width, (out.shape, width)
    return out


_DN0, _RW0, _GLA0, _FOX0 = 0, 2056, 4040, 5592
_DN_PIECES = [(_DN0, _DN0 + 2048, 2048), (_DN0 + 2048, _DN0 + 2056, 256)]
_RW_REL = [(0, 1536, 1536), (1536, 1632, 128), (1632, 1728, 128), (1728, 1984, 256)]
_RW_PIECES = [(_RW0 + a, _RW0 + b, p) for a, b, p in _RW_REL]
_GLA_PIECES = [(_GLA0, _GLA0 + 1536, 1536), (_GLA0 + 1536, _GLA0 + 1552, 256)]
_FOX_PIECES = [(_FOX0, _FOX0 + 1536, 1536), (_FOX0 + 1536, _FOX0 + 1540, 256)]


def _rw_unslab(x):
    return jnp.concatenate([x[..., :1632], x[..., RW_AD0:RW_AD0 + 96], x[..., RW_GD0:]], axis=-1)


def _pad_rows8(x, nseq, seq_len):
    return jnp.pad(x.reshape(nseq, seq_len, x.shape[-1]), ((0, 0), (0, SUBLANES - seq_len), (0, 0)))


def kernel(x_prompt, x_sample, cache_fox_k, cache_fox_v, cache_fox_logf, state_dn_conv, state_dn, state_rw_shift, state_rw, state_gla, cache_mem_k, cache_mem_v, page_table, mem_prompt, norm_mix, w_in, dn_conv_w, dn_A_log, dn_dt_bias, dn_norm_w, rw_mu, rw_w0, rw_w_up, rw_a0, rw_a_up, rw_g_up, rw_k_k, rw_k_a, rw_r_k, rw_ln_w, rw_ln_b, gla_a_up, gla_a_b, gla_norm_w, fox_qn, fox_kn, fox_fb, w_gate, b_gate, w_branch, w_out, norm_mem, norm_memkv, mem_wq, mem_wk, mem_wv, mem_wo, norm_ffn, router_g, router_g_b, router_e, router_e_b, moe_w1, moe_w3, moe_w2, norm_final):
    bp, lp, d = x_prompt.shape
    bs, ls, _ = x_sample.shape
    depth = w_in.shape[0]
    n_p = bp * lp
    n_s = bs * ls
    m = n_p + n_s
    n_phys = cache_fox_k.shape[1]
    n_mem = mem_prompt.shape[1]

    w_dn = _slab(w_in, _DN_PIECES, DN_SLAB).astype(BF16)
    w_rw = _slab(w_in, _RW_PIECES, RW_SLAB).astype(BF16)
    w_gla = _slab(w_in, _GLA_PIECES, GLA_SLAB).astype(BF16)
    w_fox = _slab(w_in, _FOX_PIECES, FOX_SLAB).astype(BF16)
    mu_r = _slab(rw_mu, _RW_REL, RW_SLAB)
    pad_lora = lambda w: jnp.pad(w, ((0, 0), (0, LANES - w.shape[1]), (0, 0))).astype(BF16)
    rw_w_up_b, rw_a_up_b, rw_g_up_b = pad_lora(rw_w_up), pad_lora(rw_a_up), rw_g_up.astype(BF16)
    w_gate_b, w_branch_b, w_out_b = w_gate.astype(BF16), w_branch.astype(BF16), w_out.astype(BF16)
    mem_wq_b, mem_wo_b = mem_wq.astype(BF16), mem_wo.astype(BF16)
    mem_wkv_b = jnp.concatenate([mem_wk, mem_wv], axis=-1).astype(BF16)
    w_router = jnp.pad(jnp.concatenate([router_g, router_e], axis=-1),
                       ((0, 0), (0, 0), (0, ROUTER_W - N_GROUPS - N_EXPERTS)))
    w_router_hi = w_router.astype(BF16)
    w_router_lo = (w_router - w_router_hi.astype(F32)).astype(BF16)
    b_router = jnp.pad(jnp.concatenate([router_g_b, router_e_b], axis=-1),
                       ((0, 0), (0, ROUTER_W - N_GROUPS - N_EXPERTS))).reshape(depth, 1, ROUTER_W)

    x = jnp.concatenate([x_prompt.reshape(n_p, d), x_sample.reshape(n_s, d)], axis=0)
    mem_x = mem_prompt.reshape(bp * n_mem, d)
    zeros = lambda *shape: jnp.zeros(shape, F32)
    rows = lambda a_p, a_s: jnp.concatenate([a_p, a_s], axis=0)
    unpad8 = lambda a: a.reshape(bs, SUBLANES, a.shape[-1])[:, :ls].reshape(n_s, a.shape[-1])
    cache_k_rows = cache_fox_k.reshape(depth, n_phys, PAGE_ROWS, FOX_HD)
    cache_v_rows = cache_fox_v.reshape(depth, n_phys, PAGE_ROWS, FOX_HD)
    cache_lf_rows = cache_fox_logf.reshape(depth, n_phys, 1, PAGE_ROWS)
    mem_k_rows = cache_mem_k.reshape(depth, bs, n_mem * MEM_HEADS, MEM_HD)
    mem_v_rows = cache_mem_v.reshape(depth, bs, n_mem * MEM_HEADS, MEM_HD)
    p_out, s_out = [], []

    for l in range(depth):
        g_mix = norm_mix[l]
        p_dn = rms_matmul(x, g_mix, w_dn[l], tm=512, tn=768, name="in_proj_dn")
        p_rw = rms_matmul(x, g_mix, w_rw[l], tm=512, tn=1024, name="in_proj_rw")
        p_gla = rms_matmul(x, g_mix, w_gla[l], tm=512, tn=896, name="in_proj_gla")
        p_fox = rms_matmul(x, g_mix, w_fox[l], tm=512, tn=896, name="in_proj_fox")

        dn_args = (dn_conv_w[l], dn_A_log[l], dn_dt_bias[l], dn_norm_w[l])
        ab_p = p_dn[:n_p, DN_AB0:DN_AB0 + SUBLANES].reshape(n_p // DN_CHUNK, DN_CHUNK, SUBLANES).transpose(0, 2, 1)
        o_dn_p, dn_p = deltanet(p_dn, ab_p, *dn_args, zeros(bp, SUBLANES, DN_CONV_CH),
                                zeros(bp, DN_HEADS, DN_DK, DN_DV), nseq=bp, seq_len=lp, c=DN_CHUNK, valid=DN_CHUNK,
                                nb=bp)
        dn_slab_s = _pad_rows8(p_dn[n_p:], bs, ls)
        ab_s = dn_slab_s[:, :, DN_AB0:DN_AB0 + SUBLANES].transpose(0, 2, 1)
        conv0_s = jnp.pad(state_dn_conv[l], ((0, 0), (SUBLANES - DN_CONV + 1, 0), (0, 0)))
        o_dn_s, dn_s = deltanet(dn_slab_s.reshape(bs * SUBLANES, DN_SLAB), ab_s, *dn_args, conv0_s, state_dn[l],
                                nseq=bs, seq_len=SUBLANES, c=SUBLANES, valid=ls, nb=8)
        dn_conv_p = p_dn[:n_p].reshape(bp, lp, DN_SLAB)[:, lp - (DN_CONV - 1):, :DN_CONV_CH]
        dn_conv_s = dn_slab_s[:, ls - (DN_CONV - 1):ls, :DN_CONV_CH]

        rw_p = p_rw[:n_p].reshape(bp, lp, RW_SLAB)
        rw_s = p_rw[n_p:].reshape(bs, ls, RW_SLAB)
        prev_p = jnp.concatenate([zeros(bp, 1, RW_SLAB), rw_p[:, :-1]], axis=1).reshape(n_p, RW_SLAB)
        prev_s = jnp.concatenate([_slab(state_rw_shift[l], _RW_REL, RW_SLAB)[:, None], rw_s[:, :-1]],
                                 axis=1).reshape(n_s, RW_SLAB)
        rw_args = (mu_r[l], rw_w0[l], rw_a0[l], rw_k_k[l], rw_k_a[l], rw_r_k[l].reshape(-1), rw_w_up_b[l],
                   rw_a_up_b[l], rw_g_up_b[l])
        tok_p = rwkv_prep(p_rw, prev_p, *rw_args, row0=0, tm=256)
        tok_s = rwkv_prep(p_rw, prev_s, *rw_args, row0=n_p, tm=256)
        y_p, rw_tiles_p = rwkv_scan(*(a.reshape(bp, lp, RW_W) for a in tok_p[:6]),
                                    zeros(bp, RW_W // RW_QW, RW_N, RW_QW), nb=bp, t_blk=64)
        y_s, rw_tiles_s = rwkv_scan(*(a.reshape(bs, ls, RW_W) for a in tok_s[:6]),
                                    rw_state_to_tiles(state_rw[l]), nb=8, t_blk=ls)
        o_rw_p = rwkv_post(y_p.reshape(n_p, RW_W), tok_p[7], tok_p[6], rw_ln_w[l], rw_ln_b[l], tm=512)
        o_rw_s = rwkv_post(y_s.reshape(n_s, RW_W), tok_s[7], tok_s[6], rw_ln_w[l], rw_ln_b[l], tm=512)

        gla_args = (gla_a_up[l], gla_a_b[l], gla_norm_w[l])
        o_gla_p, gla_tiles_p = gla(p_gla, *gla_args, zeros(bp, GLA_HEADS, GLA_DV, LANES),
                                   nseq=bp, seq_len=lp, c=DN_CHUNK, valid=DN_CHUNK)
        gla_slab_s = _pad_rows8(p_gla[n_p:], bs, ls).reshape(bs * SUBLANES, GLA_SLAB)
        o_gla_s, gla_tiles_s = gla(gla_slab_s, *gla_args, gla_state_to_tiles(state_gla[l]),
                                   nseq=bs, seq_len=SUBLANES, c=SUBLANES, valid=ls)

        fox_args = (fox_qn[l], fox_kn[l], fox_fb[l])
        f_cols = p_fox[:, FOX_F0:FOX_F0 + SUBLANES]
        ft_p = f_cols[:n_p].reshape(bp, lp, SUBLANES).transpose(0, 2, 1)
        qb_p, kn_p, kb_p, vb_p, lf_p, c_p, ct_p = fox_prep(p_fox, ft_p, *fox_args, nseq=bp, seq_len=lp, tm=256)
        o_fox_p = fox_prompt(qb_p, kb_p, vb_p, c_p, ct_p, nseq=bp, L=lp, tq=256)
        pf_s = p_fox[n_p:]
        ft_s = f_cols[n_p:].T[None]
        qb_s, kn_s, _, _, lf_s, _, _ = fox_prep(pf_s, ft_s, *fox_args, nseq=1, seq_len=n_s, tm=256)
        new_rows = lambda a: jnp.pad(a.reshape(bs, ls * FOX_HEADS, FOX_HD),
                                     ((0, 0), (0, NEW_ROWS - ls * FOX_HEADS), (0, 0)))
        lfn = jnp.pad(lf_s[:, :FOX_HEADS].reshape(bs, 1, ls * FOX_HEADS),
                      ((0, 0), (0, 0), (0, PAGE_ROWS - ls * FOX_HEADS)))
        o_fox_s = fox_sample(page_table, _pad_rows8(qb_s, bs, ls), cache_k_rows, cache_v_rows, cache_lf_rows,
                             new_rows(kn_s), new_rows(pf_s[:, FOX_V0:FOX_V0 + BRANCH_W]), lfn, layer=l, n_new=ls)

        branch_outs = [rows(o_dn_p, unpad8(o_dn_s)), rows(o_rw_p, o_rw_s), rows(o_gla_p, unpad8(o_gla_s)),
                       rows(o_fox_p, o_fox_s[:, :ls].reshape(n_s, BRANCH_W))]
        x = merge_out(x, g_mix, branch_outs, w_gate_b[l], b_gate[l], w_branch_b[l], w_out_b[l], tm=512, tn=256)

        q_mem = rms_matmul(x, norm_mem[l], mem_wq_b[l], tm=512, tn=512, name="mem_q")
        mkv = rms_matmul(mem_x, norm_memkv[l], mem_wkv_b[l], tm=512, tn=512, name="mem_kv")
        mk = mkv[:, :BRANCH_W].reshape(bp, n_mem, BRANCH_W)
        mv = mkv[:, BRANCH_W:].reshape(bp, n_mem, BRANCH_W)
        att_p = mem_attn(q_mem, mk, mv, nseq=bp, seq_len=lp, tq=512)
        att_s = mem_attn_rows(_pad_rows8(q_mem[n_p:], bs, ls).reshape(bs * SUBLANES, BRANCH_W), mem_k_rows,
                              mem_v_rows, layer=l, nseq=bs, tq=SUBLANES)
        x = matmul_res(rows(att_p, unpad8(att_s)), mem_wo_b[l], x, tm=512, tn=1024, name="mem_out")

        logits, h_ffn = router(x, norm_ffn[l], w_router_hi[l], w_router_lo[l], b_router[l], tm=512)
        eid, wts = _route(logits)
        slot_tok, block_e, n_valid, pos = _sorted_layout(eid, m)
        xs = jnp.concatenate([h_ffn, jnp.zeros((1, d), BF16)], axis=0)[slot_tok]
        ys = moe_experts(block_e, n_valid, xs, moe_w1, moe_w3, moe_w2, layer=l)
        x = x + wts[:, 0:1] * ys[pos[:, 0]] + wts[:, 1:2] * ys[pos[:, 1]]

        p_out.append((dn_conv_p, dn_p, _rw_unslab(rw_p[:, -1]), rw_tiles_to_state(rw_tiles_p),
                      gla_tiles_to_state(gla_tiles_p),
                      kn_p.reshape(bp, lp, FOX_HEADS, FOX_HD),
                      p_fox[:n_p, FOX_V0:FOX_V0 + BRANCH_W].reshape(bp, lp, FOX_HEADS, FOX_HD),
                      lf_p[:, :FOX_HEADS].reshape(bp, lp, FOX_HEADS),
                      mk.reshape(bp, n_mem, MEM_HEADS, MEM_HD), mv.reshape(bp, n_mem, MEM_HEADS, MEM_HD)))
        s_out.append((dn_conv_s, dn_s, _rw_unslab(rw_s[:, -1]), rw_tiles_to_state(rw_tiles_s),
                      gla_tiles_to_state(gla_tiles_s),
                      kn_s.reshape(bs, ls, FOX_HEADS, FOX_HD),
                      pf_s[:, FOX_V0:FOX_V0 + BRANCH_W].reshape(bs, ls, FOX_HEADS, FOX_HD),
                      lf_s[:, :FOX_HEADS].reshape(bs, ls, FOX_HEADS)))

    (p_dn_conv, p_dn_st, p_rw_shift, p_rw_st, p_gla_st, p_fox_k, p_fox_v, p_fox_logf, p_mem_k,
     p_mem_v) = [jnp.stack(r) for r in zip(*p_out)]
    (s_dn_conv, s_dn_st, s_rw_shift, s_rw_st, s_gla_st, s_fox_k, s_fox_v,
     s_fox_logf) = [jnp.stack(r) for r in zip(*s_out)]
    y = rms_rows(x, norm_final, tm=512)
    y_prompt = y[:n_p].reshape(bp, lp, d)
    y_sample = y[n_p:].reshape(bs, ls, d)
    return (y_prompt, y_sample, p_fox_k, p_fox_v, p_fox_logf, p_dn_conv, p_dn_st, p_rw_shift, p_rw_st, p_gla_st,
            p_mem_k, p_mem_v, s_fox_k, s_fox_v, s_fox_logf, s_dn_conv, s_dn_st, s_rw_shift, s_rw_st, s_gla_st)
```

```python
import functools
import math

import jax
import jax.numpy as jnp
from jax import lax
from jax.experimental import pallas as pl
from jax.experimental.pallas import tpu as pltpu

F32 = jnp.float32
BF16 = jnp.bfloat16
EPS = 1e-6
HP = lax.Precision.HIGHEST

D_MODEL = 2048
N_BRANCH = 4
BRANCH_W = 512
PAGE_SIZE = 128

DN_HEADS, DN_DK, DN_DV, DN_CONV, DN_CHUNK = 4, 128, 128, 4, 64
DN_CONV_CH = 1536
RW_HEADS, RW_N, RW_W = 8, 64, 512
RW_LORA_W, RW_LORA_A, RW_LORA_G = 96, 96, 256
RW_GN_EPS = 64e-5
RW_COLS = 1984
GLA_HEADS, GLA_DK, GLA_DV, GLA_LORA, GLA_TAU = 4, 64, 128, 16, 16.0
FOX_HEADS, FOX_HD = 4, 128
MEM_HEADS, MEM_HD, MEM_LEN = 4, 128, 256
N_GROUPS, EXPERTS_PER_GROUP, N_EXPERTS, TOP_K, D_EXPERT = 4, 8, 32, 2, 512

LANES = 128
SUBLANES = 8
VMEM_LIMIT = 56 * 1024 * 1024


def _cparams(*sem):
    return pltpu.CompilerParams(dimension_semantics=sem, vmem_limit_bytes=VMEM_LIMIT)


def _dot(a, b):
    return jnp.dot(a.astype(BF16), b.astype(BF16), preferred_element_type=F32)


def _dot_nt(a, b):
    return lax.dot_general(a.astype(BF16), b.astype(BF16), (((1,), (1,)), ((), ())), preferred_element_type=F32)


def _dot_tn(a, b):
    return lax.dot_general(a.astype(BF16), b.astype(BF16), (((0,), (0,)), ((), ())), preferred_element_type=F32)


def _split3(a):
    a1 = a.astype(BF16)
    r1 = a - a1.astype(F32)
    a2 = r1.astype(BF16)
    a3 = (r1 - a2.astype(F32)).astype(BF16)
    return a1, a2, a3


def _dot_hp3(a, b):
    a1 = a.astype(BF16)
    a2 = (a - a1.astype(F32)).astype(BF16)
    b1 = b.astype(BF16)
    b2 = (b - b1.astype(F32)).astype(BF16)
    d = lambda x, y: jnp.dot(x, y, preferred_element_type=F32)
    return (d(a2, b1) + d(a1, b2)) + d(a1, b1)


def _dot_hp_exact_rhs(a, b_bf16):
    a1, a2, a3 = _split3(a)
    d = lambda x: jnp.dot(x, b_bf16, preferred_element_type=F32)
    return (d(a3) + d(a2)) + d(a1)


def _dot_hp_exact_lhs(a_bf16, b):
    b1, b2, b3 = _split3(b)
    d = lambda y: jnp.dot(a_bf16, y, preferred_element_type=F32)
    return (d(b3) + d(b2)) + d(b1)


def _iota2(shape, axis):
    return lax.broadcasted_iota(jnp.int32, shape, axis)


def _sigmoid(x):
    return 1.0 / (1.0 + jnp.exp(-x))


def _silu(x):
    return x * _sigmoid(x)


def _softplus(x):
    return jnp.maximum(x, 0.0) + jnp.log(1.0 + jnp.exp(-jnp.abs(x)))


def _log_sigmoid(x):
    return -_softplus(-x)


def _rms_matmul_body(x_ref, g_ref, w_ref, o_ref, h_ref):
    @pl.when(pl.program_id(1) == 0)
    def _():
        x = x_ref[...]
        ms = jnp.mean(x * x, axis=-1, keepdims=True)
        h_ref[...] = (x * lax.rsqrt(ms + EPS) * g_ref[...]).astype(BF16)

    o_ref[...] = jnp.dot(h_ref[...], w_ref[...], preferred_element_type=F32).astype(o_ref.dtype)


def rms_matmul(x, g, w, *, tm, tn, out_dtype=F32, name="rms_matmul"):
    m, k = x.shape
    n = w.shape[1]
    assert m % tm == 0 and n % tn == 0
    return pl.pallas_call(
        _rms_matmul_body,
        grid=(m // tm, n // tn),
        in_specs=[pl.BlockSpec((tm, k), lambda i, j: (i, 0)),
                  pl.BlockSpec((1, k), lambda i, j: (0, 0)),
                  pl.BlockSpec((k, tn), lambda i, j: (0, j))],
        out_specs=pl.BlockSpec((tm, tn), lambda i, j: (i, j)),
        out_shape=jax.ShapeDtypeStruct((m, n), out_dtype),
        scratch_shapes=[pltpu.VMEM((tm, k), BF16)],
        compiler_params=_cparams("parallel", "arbitrary"),
        name=name,
    )(x, g.reshape(1, k), w)


def _matmul_res_body(a_ref, w_ref, r_ref, o_ref):
    o_ref[...] = r_ref[...] + jnp.dot(a_ref[...].astype(BF16), w_ref[...], preferred_element_type=F32)


def matmul_res(a, w, res, *, tm, tn, name="matmul_res"):
    m, k = a.shape
    n = w.shape[1]
    assert m % tm == 0 and n % tn == 0
    return pl.pallas_call(
        _matmul_res_body,
        grid=(m // tm, n // tn),
        in_specs=[pl.BlockSpec((tm, k), lambda i, j: (i, 0)),
                  pl.BlockSpec((k, tn), lambda i, j: (0, j)),
                  pl.BlockSpec((tm, tn), lambda i, j: (i, j))],
        out_specs=pl.BlockSpec((tm, tn), lambda i, j: (i, j)),
        out_shape=jax.ShapeDtypeStruct((m, n), F32),
        compiler_params=_cparams("parallel", "parallel"),
        name=name,
    )(a, w, res)


def _merge_body(x_ref, g_ref, o0, o1, o2, o3, wg0, wg1, wg2, wg3, bg_ref, wb_ref, wo_ref, y_ref, h_ref):
    @pl.when(pl.program_id(1) == 0)
    def _():
        x = x_ref[...]
        ms = jnp.mean(x * x, axis=-1, keepdims=True)
        h_ref[...] = (x * lax.rsqrt(ms + EPS) * g_ref[...]).astype(BF16)
        y_ref[...] = x

    h = h_ref[...]
    merged = None
    for n, (o_ref, wg) in enumerate(zip((o0, o1, o2, o3), (wg0, wg1, wg2, wg3))):
        gate = _sigmoid(jnp.dot(h, wg[...], preferred_element_type=F32) + bg_ref[n:n + 1, :])
        br = jnp.dot(o_ref[...].astype(BF16), wb_ref[n], preferred_element_type=F32)
        merged = gate * br if merged is None else merged + gate * br
    y_ref[...] += jnp.dot(merged.astype(BF16), wo_ref[...], preferred_element_type=F32)


def merge_out(x, g, branch_outs, w_gate, b_gate, w_branch, w_out, *, tm, tn):
    m, d = x.shape
    nj = d // tn
    wg_specs = [pl.BlockSpec((d, tn), functools.partial(lambda i, j, n: (0, n * nj + j), n=n))
                for n in range(N_BRANCH)]
    return pl.pallas_call(
        _merge_body,
        grid=(m // tm, nj),
        in_specs=[pl.BlockSpec((tm, d), lambda i, j: (i, 0)),
                  pl.BlockSpec((1, d), lambda i, j: (0, 0))]
                 + [pl.BlockSpec((tm, BRANCH_W), lambda i, j: (i, 0))] * N_BRANCH
                 + wg_specs
                 + [pl.BlockSpec((N_BRANCH, tn), lambda i, j: (0, j)),
                    pl.BlockSpec((N_BRANCH, BRANCH_W, tn), lambda i, j: (0, 0, j)),
                    pl.BlockSpec((tn, d), lambda i, j: (j, 0))],
        out_specs=pl.BlockSpec((tm, d), lambda i, j: (i, 0)),
        out_shape=jax.ShapeDtypeStruct((m, d), F32),
        scratch_shapes=[pltpu.VMEM((tm, d), BF16)],
        compiler_params=_cparams("parallel", "arbitrary"),
        name="merge_out",
    )(x, g.reshape(1, d), *branch_outs, w_gate, w_gate, w_gate, w_gate, b_gate.reshape(N_BRANCH, d), w_branch,
      w_out)


DN_Z0 = DN_CONV_CH
DN_AB0 = DN_CONV_CH + BRANCH_W
DN_SLAB = DN_AB0 + 256
INV_BLOCK = 16
DN_STACK = 256


def _inv_unit_lower(nmat, size, c):
    row = _iota2((size, size), 0)
    col = _iota2((size, size), 1)
    eye = (row == col).astype(F32)
    blk = min(INV_BLOCK, c)
    if c > blk:
        same = (row // blk) == (col // blk)
        d = jnp.where(same, nmat, 0.0)
        r = jnp.where(same, 0.0, nmat)
    else:
        d, r = nmat, None
    t = eye - d
    p = d
    k = 2
    while k < blk:
        p = _dot_hp3(p, p)
        t = _dot_hp3(t, eye + p)
        k *= 2
    if r is None:
        return t
    pm = _dot_hp3(t, r)
    t2 = eye - pm
    q = pm
    k = 2
    while k < c // blk:
        q = _dot_hp3(q, q)
        t2 = _dot_hp3(t2, eye + q)
        k *= 2
    return _dot_hp3(t2, t)


def _dn_body(*refs, nb, c, valid):
    p_refs = refs[0:nb]
    cw_ref, alog_ref, dtb_ref, nw_ref, conv0_ref, s0_ref, o_ref, s1_ref, cbuf, s_scr = refs[nb:]
    ci = pl.program_id(1)
    nch = nb * DN_HEADS
    size = nch * c

    @pl.when(ci == 0)
    def _():
        s_scr[...] = s0_ref[...].reshape(s_scr.shape)
        cbuf[:, 0:SUBLANES, :] = conv0_ref[...]

    vcol = _iota2((c, 1), 0) < valid
    ltri = (_iota2((c, c), 0) >= _iota2((c, c), 1)).astype(BF16)
    q_l, k_l, v_l, z_l, gc_l, beta_l = [], [], [], [], [], []
    for sq in range(nb):
        p_ref, cb = p_refs[sq], cbuf.at[sq]
        cb[SUBLANES:SUBLANES + c, :] = p_ref[:, 0:DN_CONV_CH]
        conv = cw_ref[0:1, :] * cb[5:5 + c, :]
        for j in range(1, DN_CONV):
            conv = conv + cw_ref[j:j + 1, :] * cb[5 + j:5 + j + c, :]
        cb[0:SUBLANES, :] = cb[c:c + SUBLANES, :]
        act = _silu(conv)
        ab = p_ref[:, DN_AB0:DN_AB0 + LANES]
        g_all = jnp.where(vcol, -jnp.exp(alog_ref[...]) * _softplus(ab + dtb_ref[...]), 0.0)
        beta_all = jnp.where(vcol, _sigmoid(ab), 0.0)
        gc_all = _dot_hp_exact_lhs(ltri, g_all)
        for h in range(DN_HEADS):
            q = act[:, h * DN_DK:(h + 1) * DN_DK]
            k = act[:, 512 + h * DN_DK:512 + (h + 1) * DN_DK]
            q_l.append(q * lax.rsqrt(jnp.sum(q * q, axis=-1, keepdims=True) + EPS) * (DN_DK ** -0.5))
            k_l.append(jnp.where(vcol, k * lax.rsqrt(jnp.sum(k * k, axis=-1, keepdims=True) + EPS), 0.0))
            v_l.append(jnp.where(vcol, act[:, 1024 + h * DN_DV:1024 + (h + 1) * DN_DV], 0.0))
            z_l.append(p_ref[:, DN_Z0 + h * DN_DV:DN_Z0 + (h + 1) * DN_DV])
            gc_l.append(gc_all[:, h:h + 1])
            beta_l.append(beta_all[:, DN_HEADS + h:DN_HEADS + h + 1])
    stack = lambda xs: jnp.concatenate(xs, axis=0)
    q, k, v, z, gc, beta = (stack(x) for x in (q_l, k_l, v_l, z_l, gc_l, beta_l))

    row = _iota2((size, size), 0)
    col = _iota2((size, size), 1)
    same = (row // c) == (col // c)
    incl = same & (row >= col)
    strict = same & (row > col)
    gcr = _dot_hp_exact_lhs(jnp.ones((SUBLANES, size), BF16), jnp.where(row == col, gc, 0.0))[0:1, :]
    gamma = jnp.exp(jnp.where(incl, gc - gcr, -jnp.inf))
    tmat = _inv_unit_lower(jnp.where(strict, beta * _dot_nt(k, k) * gamma, 0.0), size, c)
    egc = jnp.exp(gc)
    u = _dot(tmat, beta * v)
    w = _dot(tmat, (beta * egc) * k)
    qk = jnp.where(incl, _dot_nt(q, k) * gamma, 0.0)
    qg = q * egc

    v_new, o_inter = [], []
    for i in range(nch):
        rs = slice(i * c, (i + 1) * c)
        s = s_scr[i]
        gl = gc[(i + 1) * c - 1:(i + 1) * c, :]
        vn = u[rs] - _dot(w[rs], s)
        o_inter.append(_dot(qg[rs], s))
        s_scr[i] = jnp.exp(gl) * s + _dot_tn(k[rs] * jnp.exp(gl - gc[rs]), vn)
        v_new.append(vn)
    o = stack(o_inter) + _dot(qk, stack(v_new))
    on = o * lax.rsqrt(jnp.mean(o * o, axis=-1, keepdims=True) + EPS) * nw_ref[...] * _silu(z)
    for i in range(nch):
        sq, h = divmod(i, DN_HEADS)
        o_ref[sq, :, h * DN_DV:(h + 1) * DN_DV] = on[i * c:(i + 1) * c]

    @pl.when(ci == pl.num_programs(1) - 1)
    def _():
        s1_ref[...] = s_scr[...].reshape(s1_ref.shape)


def deltanet(p_dn, conv_w, a_log, dt_bias, norm_w, conv0, s0, *, nseq, seq_len, c, valid, nb):
    m = nseq * seq_len
    n = seq_len // c
    pad = jnp.zeros((LANES - DN_HEADS,), F32)
    alog = jnp.concatenate([a_log, pad]).reshape(1, LANES)
    dtb = jnp.concatenate([dt_bias, pad]).reshape(1, LANES)
    full = lambda shape: pl.BlockSpec(shape, lambda b, i: (0,) * len(shape))
    seq_rows = lambda sq: functools.partial(lambda b, i, sq: ((b * nb + sq) * n + i, 0), sq=sq)
    o, s1 = pl.pallas_call(
        functools.partial(_dn_body, nb=nb, c=c, valid=valid),
        grid=(nseq // nb, n),
        in_specs=[pl.BlockSpec((c, DN_SLAB), seq_rows(sq)) for sq in range(nb)]
                 + [full((DN_CONV, DN_CONV_CH)), full((1, LANES)), full((1, LANES)), full((1, DN_DV)),
                    pl.BlockSpec((nb, SUBLANES, DN_CONV_CH), lambda b, i: (b, 0, 0)),
                    pl.BlockSpec((nb, DN_HEADS, DN_DK, DN_DV), lambda b, i: (b, 0, 0, 0))],
        out_specs=[pl.BlockSpec((nb, c, BRANCH_W), lambda b, i: (b, i, 0)),
                   pl.BlockSpec((nb, DN_HEADS, DN_DK, DN_DV), lambda b, i: (b, 0, 0, 0))],
        out_shape=[jax.ShapeDtypeStruct((nseq, seq_len, BRANCH_W), F32),
                   jax.ShapeDtypeStruct((nseq, DN_HEADS, DN_DK, DN_DV), F32)],
        scratch_shapes=[pltpu.VMEM((nb, c + 2 * SUBLANES, DN_CONV_CH), F32),
                        pltpu.VMEM((nb * DN_HEADS, DN_DK, DN_DV), F32)],
        compiler_params=_cparams("parallel", "arbitrary"),
        name="deltanet",
    )(*([p_dn] * nb), conv_w, alog, dtb, norm_w.reshape(1, DN_DV), conv0, s0)
    return o.reshape(m, BRANCH_W), s1


GLA_K0, GLA_V0, GLA_G0, GLA_GD0 = 256, 512, 1024, 1536
GLA_SLAB = 1792
GLA_CHUNK = 64
GLA_STACK = 256


def _gla_body(q_ref, k_ref, v_ref, g_ref, gd_ref, aup_ref, ab_ref, nw_ref, s0_ref,
              o_ref, s1_ref, st_scr, *, nb, c, valid):
    ci = pl.program_id(2)
    size = nb * c

    @pl.when(ci == 0)
    def _():
        st_scr[...] = s0_ref[...].reshape(st_scr.shape)

    row = _iota2((size, size), 0)
    col = _iota2((size, size), 1)
    incl = ((row // c) == (col // c)) & (row >= col)
    vcol = (_iota2((size, 1), 0) % c) < valid
    lane = _iota2((1, LANES), 1)
    rows2 = lambda ref, ls: ref[:, :, ls].reshape(size, ls.stop - ls.start)
    full = slice(0, LANES)

    gk = _log_sigmoid(_dot(rows2(gd_ref, full), aup_ref[...]) + ab_ref[...]) * (1.0 / GLA_TAU)
    gk = jnp.where(vcol, gk, 0.0)
    b = _dot_hp_exact_lhs(incl.astype(BF16), gk)
    q = rows2(q_ref, full) * (GLA_DK ** -0.5)
    k = jnp.where(vcol, rows2(k_ref, full), 0.0)
    qt = q * jnp.exp(b)
    kt = k * jnp.exp(-b)
    for hh in range(2):
        lm = (lane >= hh * GLA_DK) & (lane < (hh + 1) * GLA_DK)
        vs = slice(hh * GLA_DV, (hh + 1) * GLA_DV)
        qth = jnp.where(lm, qt, 0.0)
        v = jnp.where(vcol, rows2(v_ref, vs), 0.0)
        att = jnp.where(incl, _dot_nt(qth, kt), 0.0)
        o_inter = []
        for i in range(nb):
            rs = slice(i * c, (i + 1) * c)
            b_last = b[(i + 1) * c - 1:(i + 1) * c, :]
            st = st_scr[2 * i + hh]
            o_inter.append(_dot_nt(qth[rs], st))
            kd = jnp.where(lm, k[rs] * jnp.exp(b_last - b[rs]), 0.0)
            st_scr[2 * i + hh] = st * jnp.exp(b_last) + _dot_tn(v[rs], kd)
        o = _dot(att, v) + jnp.concatenate(o_inter, axis=0)
        on = o * lax.rsqrt(jnp.mean(o * o, axis=-1, keepdims=True) + EPS) * nw_ref[...] * _silu(rows2(g_ref, vs))
        o_ref[:, :, vs] = on.reshape(nb, c, GLA_DV)

    @pl.when(ci == pl.num_programs(2) - 1)
    def _():
        s1_ref[...] = st_scr[...].reshape(s1_ref.shape)


def gla(p_gla, a_up, a_b, norm_w, s0t, *, c, valid, nb):
    nseq, seq_len, _ = p_gla.shape
    n = seq_len // c
    aup = jnp.zeros((LANES, GLA_HEADS * GLA_DK), F32).at[:GLA_LORA].set(a_up).astype(BF16)
    cols = lambda w, off: pl.BlockSpec((nb, c, w), lambda b, p, i: (b, i, off // w + p))
    o, s1 = pl.pallas_call(
        functools.partial(_gla_body, nb=nb, c=c, valid=valid),
        grid=(nseq // nb, 2, n),
        in_specs=[cols(LANES, 0), cols(LANES, GLA_K0), cols(2 * GLA_DV, GLA_V0), cols(2 * GLA_DV, GLA_G0),
                  pl.BlockSpec((nb, c, LANES), lambda b, p, i: (b, i, GLA_GD0 // LANES)),
                  pl.BlockSpec((LANES, LANES), lambda b, p, i: (0, p)),
                  pl.BlockSpec((1, LANES), lambda b, p, i: (0, p)),
                  pl.BlockSpec((1, GLA_DV), lambda b, p, i: (0, 0)),
                  pl.BlockSpec((nb, 2, GLA_DV, LANES), lambda b, p, i: (b, p, 0, 0))],
        out_specs=[pl.BlockSpec((nb, c, 2 * GLA_DV), lambda b, p, i: (b, i, p)),
                   pl.BlockSpec((nb, 2, GLA_DV, LANES), lambda b, p, i: (b, p, 0, 0))],
        out_shape=[jax.ShapeDtypeStruct((nseq, seq_len, BRANCH_W), F32),
                   jax.ShapeDtypeStruct((nseq, GLA_HEADS, GLA_DV, LANES), F32)],
        scratch_shapes=[pltpu.VMEM((nb * 2, GLA_DV, LANES), F32)],
        compiler_params=_cparams("parallel", "parallel", "arbitrary"),
        name="gla",
    )(p_gla, p_gla, p_gla, p_gla, p_gla, aup, a_b.reshape(1, -1), norm_w.reshape(1, GLA_DV), s0t)
    return o.reshape(nseq * seq_len, BRANCH_W), s1


RW_WD0, RW_AD0, RW_GD0 = 1536, 1664, 1792
RW_SLAB = 2048
RW_QW = 256


def _seg_ones(n, seg):
    r = _iota2((n, n), 0) // seg
    c = _iota2((n, n), 1) // seg
    return (r == c).astype(BF16)


def _rw_prep_body(p_ref, prev_ref, mu_ref, w0_ref, a0_ref, kk_ref, ka_ref, rk_ref, wup_ref, aup_ref, gup_ref,
                  r_o, w_o, k_o, v_o, nkk_o, kka_o, gate_o, bonus_o):
    p = p_ref[...]
    xr = p + mu_ref[...] * (prev_ref[...] - p)
    r = xr[:, 0:RW_W]
    k = xr[:, RW_W:2 * RW_W]
    v = xr[:, 2 * RW_W:3 * RW_W]
    wd = xr[:, RW_WD0:RW_WD0 + LANES]
    ad = xr[:, RW_AD0:RW_AD0 + LANES]
    gd = xr[:, RW_GD0:RW_GD0 + RW_LORA_G]
    logw = -jnp.exp(-_softplus(-(w0_ref[...] + _dot(jnp.tanh(wd), wup_ref[...]))) - 0.5)
    a = _sigmoid(a0_ref[...] + _dot(ad, aup_ref[...]))
    seg = _seg_ones(RW_W, RW_N)
    kr = k * kk_ref[...]
    kk = kr * lax.rsqrt(_dot_hp_exact_rhs(kr * kr, seg) + EPS)
    k2 = k * (1.0 + (a - 1.0) * ka_ref[...])
    r_o[...] = r
    w_o[...] = jnp.exp(logw)
    k_o[...] = k2
    v_o[...] = v
    nkk_o[...] = -kk
    kka_o[...] = kk * a
    gate_o[...] = _dot(_sigmoid(gd), gup_ref[...])
    bonus_o[...] = _dot_hp_exact_rhs(r * k2 * rk_ref[...], seg) * v


def rwkv_prep(p_rw, prev, mu, w0, a0, k_k, k_a, r_k, w_up, a_up, g_up, *, row0, tm):
    m = prev.shape[0]
    row = lambda w: pl.BlockSpec((tm, w), lambda i: (i, 0))
    full = lambda a: pl.BlockSpec(a.shape, lambda i: (0,) * a.ndim)
    args = (mu.reshape(1, -1), w0.reshape(1, -1), a0.reshape(1, -1), k_k.reshape(1, -1), k_a.reshape(1, -1),
            r_k.reshape(1, -1), w_up, a_up, g_up)
    return pl.pallas_call(
        _rw_prep_body,
        grid=(m // tm,),
        in_specs=[pl.BlockSpec((tm, RW_SLAB), lambda i: (row0 // tm + i, 0)), row(RW_SLAB)]
                 + [full(a) for a in args],
        out_specs=[row(RW_W)] * 8,
        out_shape=[jax.ShapeDtypeStruct((m, RW_W), F32)] * 8,
        compiler_params=_cparams("parallel"),
        name="rwkv_prep",
    )(p_rw, prev, *args)


def _rw_scan_body(r_ref, w_ref, k_ref, v_ref, nkk_ref, kka_ref, s0_ref, y_ref, s1_ref, s_scr, *, nb, t_blk):
    ti = pl.program_id(1)

    @pl.when(ti == 0)
    def _():
        s_scr[...] = s0_ref[...].reshape(s_scr.shape)

    nq = RW_W // RW_QW
    tiles = [(b, q) for b in range(nb) for q in range(nq)]
    ones = _seg_ones(RW_QW, RW_N)
    diag = _iota2((RW_N, RW_QW), 0) == (_iota2((RW_N, RW_QW), 1) % RW_N)
    diag_b = diag.astype(BF16)
    tg = SUBLANES if t_blk % SUBLANES == 0 else t_blk
    dot = lambda a: jnp.dot(a, ones, preferred_element_type=F32)
    stack = lambda xs: jnp.concatenate(xs, axis=0)
    piece = lambda a, i: a[i * RW_N:(i + 1) * RW_N]

    def group(g, carry):
        base = pl.multiple_of(g * tg, tg)
        ld = lambda ref: [[ref[b, pl.ds(base, tg), q * RW_QW:(q + 1) * RW_QW] for q in range(nq)] for b in range(nb)]
        rb, wb, kb, vb, nkkb, kkab = (ld(ref) for ref in (r_ref, w_ref, k_ref, v_ref, nkk_ref, kka_ref))
        yrows = [[] for _ in tiles]
        for j in range(tg):
            row = lambda blk, b, q: blk[b][q][j:j + 1, :]
            s = [s_scr[i] for i in range(len(tiles))]
            sa = dot(stack([s[i] * row(nkkb, b, q) for i, (b, q) in enumerate(tiles)]).astype(BF16))
            v1 = [row(vb, b, q).astype(BF16) for b, q in tiles]
            v2 = [(row(vb, b, q) - v1[i].astype(F32)).astype(BF16) for i, (b, q) in enumerate(tiles)]
            vcol = dot(stack([diag_b * x for x in v2])) + dot(stack([diag_b * x for x in v1]))
            s = [s[i] * row(wb, b, q) + piece(sa, i) * row(kkab, b, q) + piece(vcol, i) * row(kb, b, q)
                 for i, (b, q) in enumerate(tiles)]
            for i in range(len(tiles)):
                s_scr[i] = s[i]
            yb = dot(stack([s[i] * row(rb, b, q) for i, (b, q) in enumerate(tiles)]).astype(BF16))
            for i in range(len(tiles)):
                yrows[i].append(jnp.sum(jnp.where(diag, piece(yb, i), 0.0), axis=0, keepdims=True))
        for i, (b, q) in enumerate(tiles):
            y_ref[b, pl.ds(base, tg), q * RW_QW:(q + 1) * RW_QW] = stack(yrows[i])
        return carry

    lax.fori_loop(0, t_blk // tg, group, 0)

    @pl.when(ti == pl.num_programs(1) - 1)
    def _():
        s1_ref[...] = s_scr[...].reshape(s1_ref.shape)


def rwkv_scan(r, w, k, v, nkk, kka, s0q, *, nb, t_blk):
    nseq, L, _ = r.shape
    nq = RW_W // RW_QW
    tok = pl.BlockSpec((nb, t_blk, RW_W), lambda b, t: (b, t, 0))
    st = pl.BlockSpec((nb, nq, RW_N, RW_QW), lambda b, t: (b, 0, 0, 0))
    return pl.pallas_call(
        functools.partial(_rw_scan_body, nb=nb, t_blk=t_blk),
        grid=(nseq // nb, L // t_blk),
        in_specs=[tok] * 6 + [st],
        out_specs=[tok, st],
        out_shape=[jax.ShapeDtypeStruct((nseq, L, RW_W), F32),
                   jax.ShapeDtypeStruct((nseq, nq, RW_N, RW_QW), F32)],
        scratch_shapes=[pltpu.VMEM((nb * nq, RW_N, RW_QW), F32)],
        compiler_params=_cparams("parallel", "arbitrary"),
        name="rwkv_scan",
    )(r, w, k, v, nkk, kka, s0q)


def rw_state_to_tiles(s):
    n = s.shape[0]
    nq = RW_W // RW_QW
    hq = RW_QW // RW_N
    return s.reshape(n, nq, hq, RW_N, RW_N).transpose(0, 1, 3, 2, 4).reshape(n, nq, RW_N, RW_QW)


def rw_tiles_to_state(t):
    n = t.shape[0]
    nq = RW_W // RW_QW
    hq = RW_QW // RW_N
    return t.reshape(n, nq, RW_N, hq, RW_N).transpose(0, 1, 3, 2, 4).reshape(n, RW_HEADS, RW_N, RW_N)


def _rw_post_body(y_ref, bonus_ref, gate_ref, lw_ref, lb_ref, o_ref):
    y = y_ref[...]
    seg = _seg_ones(RW_W, RW_N)
    mu = _dot_hp_exact_rhs(y, seg) * (1.0 / RW_N)
    d = y - mu
    var = _dot_hp_exact_rhs(d * d, seg) * (1.0 / RW_N)
    yn = d * lax.rsqrt(var + RW_GN_EPS) * lw_ref[...] + lb_ref[...]
    o_ref[...] = ((yn + bonus_ref[...]) * gate_ref[...]).astype(o_ref.dtype)


def rwkv_post(y, bonus, gate, ln_w, ln_b, *, tm):
    m = y.shape[0]
    row = pl.BlockSpec((tm, RW_W), lambda i: (i, 0))
    par = pl.BlockSpec((1, RW_W), lambda i: (0, 0))
    return pl.pallas_call(
        _rw_post_body,
        grid=(m // tm,),
        in_specs=[row, row, row, par, par],
        out_specs=row,
        out_shape=jax.ShapeDtypeStruct((m, RW_W), F32),
        compiler_params=_cparams("parallel"),
        name="rwkv_post",
    )(y, bonus, gate, ln_w.reshape(1, -1), ln_b.reshape(1, -1))


FOX_K0, FOX_V0, FOX_F0 = 512, 1024, 1536
FOX_SLAB = 1792
FOX_TQ = 512
NEG = -1e30


def _fox_prep_body(p_ref, ft_ref, qn_ref, kn_ref, fb_ref, fbt_ref,
                   q_o, k_o, kb_o, vb_o, lf_o, c_o, ct_o, carry, carry_t, *, tm):
    @pl.when(pl.program_id(1) == 0)
    def _():
        carry[...] = jnp.zeros_like(carry)
        carry_t[...] = jnp.zeros_like(carry_t)

    for h in range(FOX_HEADS):
        ls = slice(h * FOX_HD, (h + 1) * FOX_HD)
        q = p_ref[:, ls]
        k = p_ref[:, FOX_K0 + h * FOX_HD:FOX_K0 + (h + 1) * FOX_HD]
        qn = q * lax.rsqrt(jnp.mean(q * q, axis=-1, keepdims=True) + EPS) * qn_ref[...]
        kn = k * lax.rsqrt(jnp.mean(k * k, axis=-1, keepdims=True) + EPS) * kn_ref[...]
        q_o[:, ls] = (qn * (FOX_HD ** -0.5)).astype(BF16)
        k_o[:, ls] = kn
        kb_o[:, ls] = kn.astype(BF16)
    vb_o[...] = p_ref[:, FOX_V0:FOX_V0 + BRANCH_W].astype(BF16)
    lf = _log_sigmoid(p_ref[:, FOX_F0:FOX_F0 + LANES] + fb_ref[...])
    lf_o[...] = lf
    row = _iota2((tm, tm), 0)
    col = _iota2((tm, tm), 1)
    c = _dot_hp_exact_lhs((row >= col).astype(BF16), lf) + carry[...]
    c_o[...] = c
    carry[...] = c[tm - 1:tm, :]
    lft = _log_sigmoid(ft_ref[0] + fbt_ref[...])
    ct = _dot_hp_exact_rhs(lft, (row <= col).astype(BF16)) + carry_t[:, 0:1]
    ct_o[0, 0] = ct
    carry_t[...] = jnp.broadcast_to(ct[:, tm - 1:tm], carry_t.shape)


def fox_prep(p_fox, ft, qn_w, kn_w, fb, *, nseq, seq_len, tm):
    n = seq_len // tm
    m = nseq * seq_len
    fbp = jnp.concatenate([fb, jnp.zeros((LANES - FOX_HEADS,), F32)]).reshape(1, LANES)
    fbt = jnp.broadcast_to(jnp.concatenate([fb, jnp.zeros((4,), F32)])[:, None], (SUBLANES, tm))
    row = lambda w, dt: (pl.BlockSpec((tm, w), lambda b, i: (b * n + i, 0)), jax.ShapeDtypeStruct((m, w), dt))
    outs = [row(BRANCH_W, BF16), row(BRANCH_W, F32), row(BRANCH_W, BF16), row(BRANCH_W, BF16),
            row(LANES, F32), row(LANES, F32),
            (pl.BlockSpec((1, 1, SUBLANES, tm), lambda b, i: (b, i, 0, 0)),
             jax.ShapeDtypeStruct((nseq, n, SUBLANES, tm), F32))]
    return pl.pallas_call(
        functools.partial(_fox_prep_body, tm=tm),
        grid=(nseq, n),
        in_specs=[pl.BlockSpec((tm, FOX_SLAB), lambda b, i: (b * n + i, 0)),
                  pl.BlockSpec((1, SUBLANES, tm), lambda b, i: (b, 0, i)),
                  pl.BlockSpec((1, FOX_HD), lambda b, i: (0, 0)),
                  pl.BlockSpec((1, FOX_HD), lambda b, i: (0, 0)),
                  pl.BlockSpec((1, LANES), lambda b, i: (0, 0)),
                  pl.BlockSpec((SUBLANES, tm), lambda b, i: (0, 0))],
        out_specs=[o[0] for o in outs],
        out_shape=[o[1] for o in outs],
        scratch_shapes=[pltpu.VMEM((1, LANES), F32), pltpu.VMEM((SUBLANES, LANES), F32)],
        compiler_params=_cparams("parallel", "arbitrary"),
        name="fox_prep",
    )(p_fox, ft, qn_w.reshape(1, FOX_HD), kn_w.reshape(1, FOX_HD), fbp, fbt)


def _fox_prompt_body(q_ref, k_ref, v_ref, c_ref, ct_ref, o_ref, *, tq):
    h = pl.program_id(1)
    qi = pl.program_id(2)
    q = q_ref[...]
    lane = _iota2((1, LANES), 1)
    cq = jnp.sum(jnp.where(lane == h, c_ref[...], 0.0), axis=-1, keepdims=True)
    rowh = _iota2((SUBLANES, 1), 0) == h
    qpos = qi * tq + _iota2((tq, tq), 0)

    def kv_step(j, carry):
        m, l, acc = carry
        k = k_ref[pl.ds(pl.multiple_of(j * tq, tq), tq), :]
        v = v_ref[pl.ds(pl.multiple_of(j * tq, tq), tq), :]
        ck = jnp.sum(jnp.where(rowh, ct_ref[0, j], 0.0), axis=0, keepdims=True)
        s = _dot_nt(q, k) + (cq - ck)
        s = jnp.where(qpos >= j * tq + _iota2((tq, tq), 1), s, NEG)
        m_new = jnp.maximum(m, jnp.max(s, axis=-1, keepdims=True))
        p = jnp.exp(s - m_new)
        alpha = jnp.exp(m - m_new)
        return m_new, alpha * l + jnp.sum(p, axis=-1, keepdims=True), alpha * acc + _dot(p, v)

    init = (jnp.full((tq, 1), NEG, F32), jnp.zeros((tq, 1), F32), jnp.zeros((tq, FOX_HD), F32))
    m, l, acc = lax.fori_loop(0, qi + 1, kv_step, init)
    o_ref[...] = (acc / l).astype(o_ref.dtype)


def fox_prompt(qb, kb, vb, c, ct, *, nseq, L, tq):
    nq = L // tq
    return pl.pallas_call(
        functools.partial(_fox_prompt_body, tq=tq),
        grid=(nseq, FOX_HEADS, nq),
        in_specs=[pl.BlockSpec((tq, FOX_HD), lambda b, h, i: (b * nq + i, h)),
                  pl.BlockSpec((L, FOX_HD), lambda b, h, i: (b, h)),
                  pl.BlockSpec((L, FOX_HD), lambda b, h, i: (b, h)),
                  pl.BlockSpec((tq, LANES), lambda b, h, i: (b * nq + i, 0)),
                  pl.BlockSpec((1, nq, SUBLANES, tq), lambda b, h, i: (b, 0, 0, 0))],
        out_specs=pl.BlockSpec((tq, FOX_HD), lambda b, h, i: (b * nq + i, h)),
        out_shape=jax.ShapeDtypeStruct((nseq * L, BRANCH_W), F32),
        compiler_params=_cparams("parallel", "parallel", "arbitrary"),
        name="fox_prompt",
    )(qb, kb, vb, c, ct)


PAGE_ROWS = PAGE_SIZE * FOX_HEADS
NEW_ROWS = 32
LF_ROWS = 24


def _fox_sample_body(pt_ref, q_ref, *refs, n_pages, n_new):
    kc = refs[0:n_pages]
    vc = refs[n_pages:2 * n_pages]
    lfc = refs[2 * n_pages:3 * n_pages]
    kn_ref, vn_ref, lfn_ref, o_ref, s_scr = refs[3 * n_pages:]
    nq = q_ref.shape[1]
    nr = FOX_HEADS * nq
    q_all = jnp.concatenate([q_ref[0, :, h * FOX_HD:(h + 1) * FOX_HD] for h in range(FOX_HEADS)], axis=0)

    r = _iota2((PAGE_ROWS, PAGE_ROWS), 0)
    c = _iota2((PAGE_ROWS, PAGE_ROWS), 1)
    same_head = (r % FOX_HEADS) == (c % FOX_HEADS)
    cum_mat = (same_head & (r // FOX_HEADS <= c // FOX_HEADS)).astype(BF16)
    tot_mat = same_head.astype(BF16)
    lf_rows = jnp.concatenate([lfc[i][...] for i in range(n_pages)] + [lfn_ref[0]]
                              + [jnp.zeros((LF_ROWS - n_pages - 1, PAGE_ROWS), F32)], axis=0)
    earlier = (_iota2((LF_ROWS, LF_ROWS), 0) > _iota2((LF_ROWS, LF_ROWS), 1)).astype(BF16)
    bias = (_dot_hp_exact_rhs(lf_rows, cum_mat)
            + _dot_hp_exact_lhs(earlier, _dot_hp_exact_rhs(lf_rows, tot_mat)))

    head_ok = (_iota2((nr, PAGE_ROWS), 0) // nq) == (_iota2((nr, PAGE_ROWS), 1) % FOX_HEADS)
    m = jnp.full((nr, PAGE_ROWS), NEG, F32)
    for i in range(n_pages):
        s = jnp.where(head_ok, _dot_nt(q_all, kc[i][...]) - bias[i:i + 1, :], NEG)
        s_scr[i] = s
        m = jnp.maximum(m, s)
    kr = _iota2((nr, NEW_ROWS), 1)
    qr = _iota2((nr, NEW_ROWS), 0)
    new_ok = ((qr // nq) == (kr % FOX_HEADS)) & (kr // FOX_HEADS <= qr % nq) & (kr < n_new * FOX_HEADS)
    s_new = jnp.where(new_ok, _dot_nt(q_all, kn_ref[0]) - bias[n_pages:n_pages + 1, 0:NEW_ROWS], NEG)
    m = jnp.maximum(jnp.max(m, axis=-1, keepdims=True), jnp.max(s_new, axis=-1, keepdims=True))

    p_new = jnp.exp(s_new - m)
    l = jnp.sum(p_new, axis=-1, keepdims=True)
    acc = _dot(p_new, vn_ref[0])
    psum = jnp.zeros((nr, PAGE_ROWS), F32)
    for i in range(n_pages):
        p = jnp.exp(s_scr[i] - m)
        psum = psum + p
        acc = acc + _dot(p, vc[i][...])
    out = acc / (l + jnp.sum(psum, axis=-1, keepdims=True))
    for h in range(FOX_HEADS):
        o_ref[0, :, h * FOX_HD:(h + 1) * FOX_HD] = out[h * nq:(h + 1) * nq]


def fox_sample(page_table, qb, cache_k, cache_v, cache_lf, kn, vn, lfn, *, layer, n_new):
    nb, nq, _ = qb.shape
    n_pages = page_table.shape[1]
    page = lambda shape, i: pl.BlockSpec(
        (None, None) + shape, functools.partial(lambda b, pt, i: (layer, pt[b * n_pages + i], 0, 0), i=i))
    per_b = lambda shape: pl.BlockSpec((1,) + shape, lambda b, pt: (b, 0, 0))
    in_specs = ([per_b((nq, BRANCH_W))]
                + [page((PAGE_ROWS, FOX_HD), i) for i in range(n_pages)]
                + [page((PAGE_ROWS, FOX_HD), i) for i in range(n_pages)]
                + [page((1, PAGE_ROWS), i) for i in range(n_pages)]
                + [per_b((NEW_ROWS, FOX_HD)), per_b((NEW_ROWS, FOX_HD)), per_b((1, PAGE_ROWS))])
    grid_spec = pltpu.PrefetchScalarGridSpec(
        num_scalar_prefetch=1, grid=(nb,), in_specs=in_specs, out_specs=per_b((nq, BRANCH_W)),
        scratch_shapes=[pltpu.VMEM((n_pages, FOX_HEADS * nq, PAGE_ROWS), F32)])
    return pl.pallas_call(
        functools.partial(_fox_sample_body, n_pages=n_pages, n_new=n_new),
        grid_spec=grid_spec,
        out_shape=jax.ShapeDtypeStruct((nb, nq, BRANCH_W), F32),
        compiler_params=_cparams("parallel"),
        name="fox_sample",
    )(page_table.reshape(-1), qb, *([cache_k] * n_pages), *([cache_v] * n_pages), *([cache_lf] * n_pages),
      kn, vn, lfn)


def _mem_attn_body(q_ref, k_ref, v_ref, o_ref):
    for h in range(MEM_HEADS):
        ls = slice(h * MEM_HD, (h + 1) * MEM_HD)
        s = _dot_nt(q_ref[:, ls], k_ref[0, :, ls]) * (MEM_HD ** -0.5)
        p = jnp.exp(s - jnp.max(s, axis=-1, keepdims=True))
        o_ref[:, ls] = _dot(p, v_ref[0, :, ls]) / jnp.sum(p, axis=-1, keepdims=True)


def mem_attn(q, mk, mv, *, nseq, seq_len, tq):
    m = nseq * seq_len
    n = seq_len // tq
    kv = pl.BlockSpec((1, MEM_LEN, BRANCH_W), lambda b, i: (b, 0, 0))
    return pl.pallas_call(
        _mem_attn_body,
        grid=(nseq, n),
        in_specs=[pl.BlockSpec((tq, BRANCH_W), lambda b, i: (b * n + i, 0)), kv, kv],
        out_specs=pl.BlockSpec((tq, BRANCH_W), lambda b, i: (b * n + i, 0)),
        out_shape=jax.ShapeDtypeStruct((m, BRANCH_W), F32),
        compiler_params=_cparams("parallel", "parallel"),
        name="mem_attn",
    )(q, mk, mv)


def _mem_attn_rows_body(q_ref, k_ref, v_ref, o_ref):
    nq = q_ref.shape[0]
    nr = MEM_HEADS * nq
    nk = MEM_LEN * MEM_HEADS
    q_all = jnp.concatenate([q_ref[:, h * MEM_HD:(h + 1) * MEM_HD] for h in range(MEM_HEADS)], axis=0)
    head_ok = (_iota2((nr, nk), 0) // nq) == (_iota2((nr, nk), 1) % MEM_HEADS)
    s = jnp.where(head_ok, _dot_nt(q_all, k_ref[...]) * (MEM_HD ** -0.5), NEG)
    p = jnp.exp(s - jnp.max(s, axis=-1, keepdims=True))
    out = _dot(p, v_ref[...]) / jnp.sum(p, axis=-1, keepdims=True)
    for h in range(MEM_HEADS):
        o_ref[:, h * MEM_HD:(h + 1) * MEM_HD] = out[h * nq:(h + 1) * nq]


def mem_attn_rows(q, mk, mv, *, layer, nseq, tq):
    kv = pl.BlockSpec((None, None, MEM_LEN * MEM_HEADS, MEM_HD), lambda b: (layer, b, 0, 0))
    return pl.pallas_call(
        _mem_attn_rows_body,
        grid=(nseq,),
        in_specs=[pl.BlockSpec((tq, BRANCH_W), lambda b: (b, 0)), kv, kv],
        out_specs=pl.BlockSpec((tq, BRANCH_W), lambda b: (b, 0)),
        out_shape=jax.ShapeDtypeStruct((nseq * tq, BRANCH_W), F32),
        compiler_params=_cparams("parallel"),
        name="mem_attn_rows",
    )(q, mk, mv)


def gla_state_to_tiles(s):
    st = jnp.swapaxes(s, -1, -2)
    z = jnp.zeros_like(st)
    even = jnp.concatenate([st, z], axis=-1)
    odd = jnp.concatenate([z, st], axis=-1)
    sel = (jnp.arange(GLA_HEADS) % 2 == 1)[None, :, None, None]
    return jnp.where(sel, odd, even)


def gla_tiles_to_state(t):
    even = t[..., :GLA_DK]
    odd = t[..., GLA_DK:]
    sel = (jnp.arange(GLA_HEADS) % 2 == 1)[None, :, None, None]
    return jnp.swapaxes(jnp.where(sel, odd, even), -1, -2)


ROUTER_W = LANES
MOE_TB = 256


def _router_body(x_ref, g_ref, w1_ref, w2_ref, b_ref, lo_ref, h_ref):
    x = x_ref[...]
    h = x * lax.rsqrt(jnp.mean(x * x, axis=-1, keepdims=True) + EPS) * g_ref[...]
    h1 = h.astype(BF16)
    h2 = (h - h1.astype(F32)).astype(BF16)
    d = lambda a, b: jnp.dot(a, b, preferred_element_type=F32)
    lo_ref[...] = (d(h2, w1_ref[...]) + d(h1, w2_ref[...])) + d(h1, w1_ref[...]) + b_ref[...]
    h_ref[...] = h


def router(x, g, w_hi, w_lo, b, *, tm):
    m, d = x.shape
    full = lambda a: pl.BlockSpec(a.shape, lambda i: (0,) * a.ndim)
    g2 = g.reshape(1, d)
    return pl.pallas_call(
        _router_body,
        grid=(m // tm,),
        in_specs=[pl.BlockSpec((tm, d), lambda i: (i, 0)), full(g2), full(w_hi), full(w_lo), full(b)],
        out_specs=[pl.BlockSpec((tm, ROUTER_W), lambda i: (i, 0)), pl.BlockSpec((tm, d), lambda i: (i, 0))],
        out_shape=[jax.ShapeDtypeStruct((m, ROUTER_W), F32), jax.ShapeDtypeStruct((m, d), F32)],
        compiler_params=_cparams("parallel"),
        name="router",
    )(x, g2, w_hi, w_lo, b)


def _moe_body(be_ref, nv_ref, xs_ref, w1_ref, w3_ref, w2_ref, o_ref, w1b, w3b, w2b):
    i = pl.program_id(0)

    @pl.when(i < nv_ref[0])
    def _():
        @pl.when((i == 0) | (be_ref[i] != be_ref[jnp.maximum(i - 1, 0)]))
        def _():
            w1b[...] = w1_ref[...].astype(BF16)
            w3b[...] = w3_ref[...].astype(BF16)
            w2b[...] = w2_ref[...].astype(BF16)

        x = xs_ref[...].astype(BF16)
        a = jnp.dot(x, w1b[...], preferred_element_type=F32)
        b = jnp.dot(x, w3b[...], preferred_element_type=F32)
        o_ref[...] = jnp.dot((_silu(a) * b).astype(BF16), w2b[...], preferred_element_type=F32)

    @pl.when(i >= nv_ref[0])
    def _():
        o_ref[...] = jnp.zeros_like(o_ref)


def moe_experts(block_e, n_valid, xs, w1, w3, w2, *, layer):
    n_rows, d = xs.shape
    f = w1.shape[-1]
    grid_spec = pltpu.PrefetchScalarGridSpec(
        num_scalar_prefetch=2, grid=(n_rows // MOE_TB,),
        in_specs=[pl.BlockSpec((MOE_TB, d), lambda i, be, nv: (i, 0)),
                  pl.BlockSpec((None, None, d, f), lambda i, be, nv: (layer, be[i], 0, 0)),
                  pl.BlockSpec((None, None, d, f), lambda i, be, nv: (layer, be[i], 0, 0)),
                  pl.BlockSpec((None, None, f, d), lambda i, be, nv: (layer, be[i], 0, 0))],
        out_specs=pl.BlockSpec((MOE_TB, d), lambda i, be, nv: (i, 0)),
        scratch_shapes=[pltpu.VMEM((d, f), BF16), pltpu.VMEM((d, f), BF16), pltpu.VMEM((f, d), BF16)])
    return pl.pallas_call(
        _moe_body,
        grid_spec=grid_spec,
        out_shape=jax.ShapeDtypeStruct((n_rows, d), F32),
        compiler_params=_cparams("arbitrary"),
        name="moe_experts",
    )(block_e, n_valid, xs, w1, w3, w2)


def _route(logits):
    m = logits.shape[0]
    pg = jax.nn.softmax(logits[:, :N_GROUPS], axis=-1)
    grp = jnp.argmax(pg, axis=-1)
    pg_top = jnp.max(pg, axis=-1)
    le = logits[:, N_GROUPS:N_GROUPS + N_EXPERTS].reshape(m, N_GROUPS, EXPERTS_PER_GROUP)
    pe = jax.nn.softmax(le[jnp.arange(m), grp], axis=-1)
    top_p, top_i = lax.top_k(pe, TOP_K)
    wts = pg_top[:, None] * top_p / jnp.sum(top_p, axis=-1, keepdims=True)
    eid = grp[:, None] * EXPERTS_PER_GROUP + top_i
    return eid, wts


def _sorted_layout(eid, m):
    n_slots = m * TOP_K
    flat_e = eid.reshape(n_slots)
    order = jnp.argsort(flat_e)
    sorted_e = flat_e[order]
    counts = jnp.bincount(flat_e, length=N_EXPERTS)
    padded = (counts + MOE_TB - 1) // MOE_TB * MOE_TB
    pad_end = jnp.cumsum(padded)
    pad_start = pad_end - padded
    start = jnp.cumsum(counts) - counts
    dest = (pad_start[sorted_e] + jnp.arange(n_slots) - start[sorted_e]).astype(jnp.int32)
    n_blocks = -(-n_slots // MOE_TB) + N_EXPERTS
    block_e = jnp.minimum(jnp.searchsorted(pad_end, jnp.arange(n_blocks) * MOE_TB, side='right'),
                          N_EXPERTS - 1).astype(jnp.int32)
    n_valid = (pad_end[-1:] // MOE_TB).astype(jnp.int32)
    row_e = jnp.repeat(block_e, MOE_TB)
    i_in_e = jnp.arange(n_blocks * MOE_TB) - pad_start[row_e]
    src = jnp.clip(start[row_e] + i_in_e, 0, n_slots - 1)
    slot_tok = jnp.where(i_in_e < counts[row_e], order[src] // TOP_K, 0).astype(jnp.int32)
    pos = dest[jnp.argsort(order)].reshape(m, TOP_K)
    return slot_tok, block_e, n_valid, pos


def _rms_rows_body(x_ref, g_ref, o_ref):
    x = x_ref[...]
    o_ref[...] = x * lax.rsqrt(jnp.mean(x * x, axis=-1, keepdims=True) + EPS) * g_ref[...]


def rms_rows(x, g, *, tm):
    m, d = x.shape
    return pl.pallas_call(
        _rms_rows_body,
        grid=(m // tm,),
        in_specs=[pl.BlockSpec((tm, d), lambda i: (i, 0)), pl.BlockSpec((1, d), lambda i: (0, 0))],
        out_specs=pl.BlockSpec((tm, d), lambda i: (i, 0)),
        out_shape=jax.ShapeDtypeStruct((m, d), F32),
        compiler_params=_cparams("parallel"),
        name="rms_rows",
    )(x, g.reshape(1, d))


def _slab(w, pieces, width):
    cols = []
    for start, stop, padded in pieces:
        cols.append(w[..., start:stop])
        if padded > stop - start:
            cols.append(jnp.zeros(w.shape[:-1] + (padded - (stop - start),), w.dtype))
    out = jnp.concatenate(cols, axis=-1)
    assert out.shape[-1] == width, (out.shape, width)
    return out


_DN0, _RW0, _GLA0, _FOX0 = 0, 2056, 4040, 5592
_DN_PIECES = [(_DN0, _DN0 + 2048, 2048), (_DN0 + 2048, _DN0 + 2056, 256)]
_RW_REL = [(0, 1536, 1536), (1536, 1632, 128), (1632, 1728, 128), (1728, 1984, 256)]
_RW_PIECES = [(_RW0 + a, _RW0 + b, p) for a, b, p in _RW_REL]
_GLA_PIECES = [(_GLA0, _GLA0 + 1536, 1536), (_GLA0 + 1536, _GLA0 + 1552, 256)]
_FOX_PIECES = [(_FOX0, _FOX0 + 1536, 1536), (_FOX0 + 1536, _FOX0 + 1540, 256)]


def _rw_unslab(x):
    return jnp.concatenate([x[..., :1632], x[..., RW_AD0:RW_AD0 + 96], x[..., RW_GD0:]], axis=-1)


def _pad_rows8(x, nseq, seq_len):
    return jnp.pad(x.reshape(nseq, seq_len, x.shape[-1]), ((0, 0), (0, SUBLANES - seq_len), (0, 0)))


def kernel(x_prompt, x_sample, cache_fox_k, cache_fox_v, cache_fox_logf, state_dn_conv, state_dn, state_rw_shift, state_rw, state_gla, cache_mem_k, cache_mem_v, page_table, mem_prompt, norm_mix, w_in, dn_conv_w, dn_A_log, dn_dt_bias, dn_norm_w, rw_mu, rw_w0, rw_w_up, rw_a0, rw_a_up, rw_g_up, rw_k_k, rw_k_a, rw_r_k, rw_ln_w, rw_ln_b, gla_a_up, gla_a_b, gla_norm_w, fox_qn, fox_kn, fox_fb, w_gate, b_gate, w_branch, w_out, norm_mem, norm_memkv, mem_wq, mem_wk, mem_wv, mem_wo, norm_ffn, router_g, router_g_b, router_e, router_e_b, moe_w1, moe_w3, moe_w2, norm_final):
    bp, lp, d = x_prompt.shape
    bs, ls, _ = x_sample.shape
    depth = w_in.shape[0]
    n_p = bp * lp
    n_s = bs * ls
    m = n_p + n_s
    n_phys = cache_fox_k.shape[1]
    n_mem = mem_prompt.shape[1]

    w_dn = _slab(w_in, _DN_PIECES, DN_SLAB).astype(BF16)
    w_rw = _slab(w_in, _RW_PIECES, RW_SLAB).astype(BF16)
    w_gla = _slab(w_in, _GLA_PIECES, GLA_SLAB).astype(BF16)
    w_fox = _slab(w_in, _FOX_PIECES, FOX_SLAB).astype(BF16)
    mu_r = _slab(rw_mu, _RW_REL, RW_SLAB)
    pad_lora = lambda w: jnp.pad(w, ((0, 0), (0, LANES - w.shape[1]), (0, 0))).astype(BF16)
    rw_w_up_b, rw_a_up_b, rw_g_up_b = pad_lora(rw_w_up), pad_lora(rw_a_up), rw_g_up.astype(BF16)
    w_gate_b, w_branch_b, w_out_b = w_gate.astype(BF16), w_branch.astype(BF16), w_out.astype(BF16)
    mem_wq_b, mem_wo_b = mem_wq.astype(BF16), mem_wo.astype(BF16)
    mem_wkv_b = jnp.concatenate([mem_wk, mem_wv], axis=-1).astype(BF16)
    w_router = jnp.pad(jnp.concatenate([router_g, router_e], axis=-1),
                       ((0, 0), (0, 0), (0, ROUTER_W - N_GROUPS - N_EXPERTS)))
    w_router_hi = w_router.astype(BF16)
    w_router_lo = (w_router - w_router_hi.astype(F32)).astype(BF16)
    b_router = jnp.pad(jnp.concatenate([router_g_b, router_e_b], axis=-1),
                       ((0, 0), (0, ROUTER_W - N_GROUPS - N_EXPERTS))).reshape(depth, 1, ROUTER_W)

    x = jnp.concatenate([x_prompt.reshape(n_p, d), x_sample.reshape(n_s, d)], axis=0)
    mem_x = mem_prompt.reshape(bp * n_mem, d)
    zeros = lambda *shape: jnp.zeros(shape, F32)
    rows = lambda a_p, a_s: jnp.concatenate([a_p, a_s], axis=0)
    unpad8 = lambda a: a.reshape(bs, SUBLANES, a.shape[-1])[:, :ls].reshape(n_s, a.shape[-1])
    cache_k_rows = cache_fox_k.reshape(depth, n_phys, PAGE_ROWS, FOX_HD)
    cache_v_rows = cache_fox_v.reshape(depth, n_phys, PAGE_ROWS, FOX_HD)
    cache_lf_rows = cache_fox_logf.reshape(depth, n_phys, 1, PAGE_ROWS)
    mem_k_rows = cache_mem_k.reshape(depth, bs, n_mem * MEM_HEADS, MEM_HD)
    mem_v_rows = cache_mem_v.reshape(depth, bs, n_mem * MEM_HEADS, MEM_HD)
    p_out, s_out = [], []

    for l in range(depth):
        g_mix = norm_mix[l]
        p_dn = rms_matmul(x, g_mix, w_dn[l], tm=512, tn=768, name="in_proj_dn")
        p_rw = rms_matmul(x, g_mix, w_rw[l], tm=512, tn=1024, name="in_proj_rw")
        p_gla = rms_matmul(x, g_mix, w_gla[l], tm=512, tn=896, name="in_proj_gla")
        p_fox = rms_matmul(x, g_mix, w_fox[l], tm=512, tn=896, name="in_proj_fox")

        dn_args = (dn_conv_w[l], dn_A_log[l], dn_dt_bias[l], dn_norm_w[l])
        o_dn_p, dn_p = deltanet(p_dn, *dn_args, zeros(bp, SUBLANES, DN_CONV_CH),
                                zeros(bp, DN_HEADS, DN_DK, DN_DV), nseq=bp, seq_len=lp, c=DN_CHUNK, valid=DN_CHUNK,
                                nb=DN_STACK // (DN_HEADS * DN_CHUNK))
        dn_slab_s = _pad_rows8(p_dn[n_p:], bs, ls)
        conv0_s = jnp.pad(state_dn_conv[l], ((0, 0), (SUBLANES - DN_CONV + 1, 0), (0, 0)))
        o_dn_s, dn_s = deltanet(dn_slab_s.reshape(bs * SUBLANES, DN_SLAB), *dn_args, conv0_s, state_dn[l],
                                nseq=bs, seq_len=SUBLANES, c=SUBLANES, valid=ls,
                                nb=DN_STACK // (DN_HEADS * SUBLANES))
        dn_conv_p = p_dn[:n_p].reshape(bp, lp, DN_SLAB)[:, lp - (DN_CONV - 1):, :DN_CONV_CH]
        dn_conv_s = dn_slab_s[:, ls - (DN_CONV - 1):ls, :DN_CONV_CH]

        rw_p = p_rw[:n_p].reshape(bp, lp, RW_SLAB)
        rw_s = p_rw[n_p:].reshape(bs, ls, RW_SLAB)
        prev_p = jnp.concatenate([zeros(bp, 1, RW_SLAB), rw_p[:, :-1]], axis=1).reshape(n_p, RW_SLAB)
        prev_s = jnp.concatenate([_slab(state_rw_shift[l], _RW_REL, RW_SLAB)[:, None], rw_s[:, :-1]],
                                 axis=1).reshape(n_s, RW_SLAB)
        rw_args = (mu_r[l], rw_w0[l], rw_a0[l], rw_k_k[l], rw_k_a[l], rw_r_k[l].reshape(-1), rw_w_up_b[l],
                   rw_a_up_b[l], rw_g_up_b[l])
        tok_p = rwkv_prep(p_rw, prev_p, *rw_args, row0=0, tm=256)
        tok_s = rwkv_prep(p_rw, prev_s, *rw_args, row0=n_p, tm=256)
        y_p, rw_tiles_p = rwkv_scan(*(a.reshape(bp, lp, RW_W) for a in tok_p[:6]),
                                    zeros(bp, RW_W // RW_QW, RW_N, RW_QW), nb=bp, t_blk=64)
        y_s, rw_tiles_s = rwkv_scan(*(a.reshape(bs, ls, RW_W) for a in tok_s[:6]),
                                    rw_state_to_tiles(state_rw[l]), nb=8, t_blk=ls)
        o_rw_p = rwkv_post(y_p.reshape(n_p, RW_W), tok_p[7], tok_p[6], rw_ln_w[l], rw_ln_b[l], tm=512)
        o_rw_s = rwkv_post(y_s.reshape(n_s, RW_W), tok_s[7], tok_s[6], rw_ln_w[l], rw_ln_b[l], tm=512)

        gla_args = (gla_a_up[l], gla_a_b[l], gla_norm_w[l])
        o_gla_p, gla_tiles_p = gla(p_gla[:n_p].reshape(bp, lp, GLA_SLAB), *gla_args,
                                   zeros(bp, GLA_HEADS, GLA_DV, LANES), c=GLA_CHUNK, valid=GLA_CHUNK,
                                   nb=GLA_STACK // GLA_CHUNK)
        o_gla_s, gla_tiles_s = gla(_pad_rows8(p_gla[n_p:], bs, ls), *gla_args, gla_state_to_tiles(state_gla[l]),
                                   c=SUBLANES, valid=ls, nb=GLA_STACK // SUBLANES)

        fox_args = (fox_qn[l], fox_kn[l], fox_fb[l])
        f_cols = p_fox[:, FOX_F0:FOX_F0 + SUBLANES]
        ft_p = f_cols[:n_p].reshape(bp, lp, SUBLANES).transpose(0, 2, 1)
        qb_p, kn_p, kb_p, vb_p, lf_p, c_p, ct_p = fox_prep(p_fox, ft_p, *fox_args, nseq=bp, seq_len=lp, tm=FOX_TQ)
        o_fox_p = fox_prompt(qb_p, kb_p, vb_p, c_p, ct_p, nseq=bp, L=lp, tq=FOX_TQ)
        pf_s = p_fox[n_p:]
        ft_s = f_cols[n_p:].T[None]
        qb_s, kn_s, _, _, lf_s, _, _ = fox_prep(pf_s, ft_s, *fox_args, nseq=1, seq_len=n_s, tm=256)
        new_rows = lambda a: jnp.pad(a.reshape(bs, ls * FOX_HEADS, FOX_HD),
                                     ((0, 0), (0, NEW_ROWS - ls * FOX_HEADS), (0, 0)))
        lfn = jnp.pad(lf_s[:, :FOX_HEADS].reshape(bs, 1, ls * FOX_HEADS),
                      ((0, 0), (0, 0), (0, PAGE_ROWS - ls * FOX_HEADS)))
        o_fox_s = fox_sample(page_table, _pad_rows8(qb_s, bs, ls), cache_k_rows, cache_v_rows, cache_lf_rows,
                             new_rows(kn_s), new_rows(pf_s[:, FOX_V0:FOX_V0 + BRANCH_W]), lfn, layer=l, n_new=ls)

        branch_outs = [rows(o_dn_p, unpad8(o_dn_s)), rows(o_rw_p, o_rw_s), rows(o_gla_p, unpad8(o_gla_s)),
                       rows(o_fox_p, o_fox_s[:, :ls].reshape(n_s, BRANCH_W))]
        x = merge_out(x, g_mix, branch_outs, w_gate_b[l], b_gate[l], w_branch_b[l], w_out_b[l], tm=512, tn=256)

        q_mem = rms_matmul(x, norm_mem[l], mem_wq_b[l], tm=512, tn=512, name="mem_q")
        mkv = rms_matmul(mem_x, norm_memkv[l], mem_wkv_b[l], tm=512, tn=512, name="mem_kv")
        mk = mkv[:, :BRANCH_W].reshape(bp, n_mem, BRANCH_W)
        mv = mkv[:, BRANCH_W:].reshape(bp, n_mem, BRANCH_W)
        att_p = mem_attn(q_mem, mk, mv, nseq=bp, seq_len=lp, tq=512)
        att_s = mem_attn_rows(_pad_rows8(q_mem[n_p:], bs, ls).reshape(bs * SUBLANES, BRANCH_W), mem_k_rows,
                              mem_v_rows, layer=l, nseq=bs, tq=SUBLANES)
        x = matmul_res(rows(att_p, unpad8(att_s)), mem_wo_b[l], x, tm=512, tn=1024, name="mem_out")

        logits, h_ffn = router(x, norm_ffn[l], w_router_hi[l], w_router_lo[l], b_router[l], tm=512)
        eid, wts = _route(logits)
        slot_tok, block_e, n_valid, pos = _sorted_layout(eid, m)
        xs = h_ffn[slot_tok]
        ys = moe_experts(block_e, n_valid, xs, moe_w1, moe_w3, moe_w2, layer=l)
        x = x + wts[:, 0:1] * ys[pos[:, 0]] + wts[:, 1:2] * ys[pos[:, 1]]

        p_out.append((dn_conv_p, dn_p, _rw_unslab(rw_p[:, -1]), rw_tiles_to_state(rw_tiles_p),
                      gla_tiles_to_state(gla_tiles_p),
                      kn_p.reshape(bp, lp, FOX_HEADS, FOX_HD),
                      p_fox[:n_p, FOX_V0:FOX_V0 + BRANCH_W].reshape(bp, lp, FOX_HEADS, FOX_HD),
                      lf_p[:, :FOX_HEADS].reshape(bp, lp, FOX_HEADS),
                      mk.reshape(bp, n_mem, MEM_HEADS, MEM_HD), mv.reshape(bp, n_mem, MEM_HEADS, MEM_HD)))
        s_out.append((dn_conv_s, dn_s, _rw_unslab(rw_s[:, -1]), rw_tiles_to_state(rw_tiles_s),
                      gla_tiles_to_state(gla_tiles_s),
                      kn_s.reshape(bs, ls, FOX_HEADS, FOX_HD),
                      pf_s[:, FOX_V0:FOX_V0 + BRANCH_W].reshape(bs, ls, FOX_HEADS, FOX_HD),
                      lf_s[:, :FOX_HEADS].reshape(bs, ls, FOX_HEADS)))

    (p_dn_conv, p_dn_st, p_rw_shift, p_rw_st, p_gla_st, p_fox_k, p_fox_v, p_fox_logf, p_mem_k,
     p_mem_v) = [jnp.stack(r) for r in zip(*p_out)]
    (s_dn_conv, s_dn_st, s_rw_shift, s_rw_st, s_gla_st, s_fox_k, s_fox_v,
     s_fox_logf) = [jnp.stack(r) for r in zip(*s_out)]
    y = rms_rows(x, norm_final, tm=512)
    y_prompt = y[:n_p].reshape(bp, lp, d)
    y_sample = y[n_p:].reshape(bs, ls, d)
    return (y_prompt, y_sample, p_fox_k, p_fox_v, p_fox_logf, p_dn_conv, p_dn_st, p_rw_shift, p_rw_st, p_gla_st,
            p_mem_k, p_mem_v, s_fox_k, s_fox_v, s_fox_logf, s_dn_conv, s_dn_st, s_rw_shift, s_rw_st, s_gla_st)
```

```python
import functools
import math

import jax
import jax.numpy as jnp
from jax import lax
from jax.experimental import pallas as pl
from jax.experimental.pallas import tpu as pltpu

F32 = jnp.float32
BF16 = jnp.bfloat16
EPS = 1e-6
HP = lax.Precision.HIGHEST

D_MODEL = 2048
N_BRANCH = 4
BRANCH_W = 512
PAGE_SIZE = 128

DN_HEADS, DN_DK, DN_DV, DN_CONV, DN_CHUNK = 4, 128, 128, 4, 64
DN_CONV_CH = 1536
RW_HEADS, RW_N, RW_W = 8, 64, 512
RW_LORA_W, RW_LORA_A, RW_LORA_G = 96, 96, 256
RW_GN_EPS = 64e-5
RW_COLS = 1984
GLA_HEADS, GLA_DK, GLA_DV, GLA_LORA, GLA_TAU = 4, 64, 128, 16, 16.0
FOX_HEADS, FOX_HD = 4, 128
MEM_HEADS, MEM_HD, MEM_LEN = 4, 128, 256
N_GROUPS, EXPERTS_PER_GROUP, N_EXPERTS, TOP_K, D_EXPERT = 4, 8, 32, 2, 512

LANES = 128
SUBLANES = 8
VMEM_LIMIT = 56 * 1024 * 1024


def _cparams(*sem):
    return pltpu.CompilerParams(dimension_semantics=sem, vmem_limit_bytes=VMEM_LIMIT)


def _dot(a, b):
    return jnp.dot(a.astype(BF16), b.astype(BF16), preferred_element_type=F32)


def _dot_nt(a, b):
    return lax.dot_general(a.astype(BF16), b.astype(BF16), (((1,), (1,)), ((), ())), preferred_element_type=F32)


def _dot_tn(a, b):
    return lax.dot_general(a.astype(BF16), b.astype(BF16), (((0,), (0,)), ((), ())), preferred_element_type=F32)


def _split3(a):
    a1 = a.astype(BF16)
    r1 = a - a1.astype(F32)
    a2 = r1.astype(BF16)
    a3 = (r1 - a2.astype(F32)).astype(BF16)
    return a1, a2, a3


def _dot_hp3(a, b):
    a1 = a.astype(BF16)
    a2 = (a - a1.astype(F32)).astype(BF16)
    b1 = b.astype(BF16)
    b2 = (b - b1.astype(F32)).astype(BF16)
    d = lambda x, y: jnp.dot(x, y, preferred_element_type=F32)
    return (d(a2, b1) + d(a1, b2)) + d(a1, b1)


def _dot_hp_exact_rhs(a, b_bf16):
    a1, a2, a3 = _split3(a)
    d = lambda x: jnp.dot(x, b_bf16, preferred_element_type=F32)
    return (d(a3) + d(a2)) + d(a1)


def _dot_hp_exact_lhs(a_bf16, b):
    b1, b2, b3 = _split3(b)
    d = lambda y: jnp.dot(a_bf16, y, preferred_element_type=F32)
    return (d(b3) + d(b2)) + d(b1)


def _iota2(shape, axis):
    return lax.broadcasted_iota(jnp.int32, shape, axis)


def _sigmoid(x):
    return 1.0 / (1.0 + jnp.exp(-x))


def _silu(x):
    return x * _sigmoid(x)


def _softplus(x):
    return jnp.maximum(x, 0.0) + jnp.log(1.0 + jnp.exp(-jnp.abs(x)))


def _log_sigmoid(x):
    return -_softplus(-x)


def _rms_matmul_body(x_ref, g_ref, w_ref, o_ref, h_ref):
    @pl.when(pl.program_id(1) == 0)
    def _():
        x = x_ref[...]
        ms = jnp.mean(x * x, axis=-1, keepdims=True)
        h_ref[...] = (x * lax.rsqrt(ms + EPS) * g_ref[...]).astype(BF16)

    o_ref[...] = jnp.dot(h_ref[...], w_ref[...], preferred_element_type=F32).astype(o_ref.dtype)


def rms_matmul(x, g, w, *, tm, tn, row0=0, rows=None, out_dtype=F32, name="rms_matmul"):
    k = x.shape[1]
    m = x.shape[0] if rows is None else rows
    n = w.shape[1]
    assert m % tm == 0 and row0 % tm == 0 and n % tn == 0
    return pl.pallas_call(
        _rms_matmul_body,
        grid=(m // tm, n // tn),
        in_specs=[pl.BlockSpec((tm, k), lambda i, j: (row0 // tm + i, 0)),
                  pl.BlockSpec((1, k), lambda i, j: (0, 0)),
                  pl.BlockSpec((k, tn), lambda i, j: (0, j))],
        out_specs=pl.BlockSpec((tm, tn), lambda i, j: (i, j)),
        out_shape=jax.ShapeDtypeStruct((m, n), out_dtype),
        scratch_shapes=[pltpu.VMEM((tm, k), BF16)],
        compiler_params=_cparams("parallel", "arbitrary"),
        name=name,
    )(x, g.reshape(1, k), w)


def _matmul_res_body(a_ref, w_ref, r_ref, o_ref):
    o_ref[...] = r_ref[...] + jnp.dot(a_ref[...].astype(BF16), w_ref[...], preferred_element_type=F32)


def matmul_res(a, w, res, *, tm, tn, name="matmul_res"):
    m, k = a.shape
    n = w.shape[1]
    assert m % tm == 0 and n % tn == 0
    return pl.pallas_call(
        _matmul_res_body,
        grid=(m // tm, n // tn),
        in_specs=[pl.BlockSpec((tm, k), lambda i, j: (i, 0)),
                  pl.BlockSpec((k, tn), lambda i, j: (0, j)),
                  pl.BlockSpec((tm, tn), lambda i, j: (i, j))],
        out_specs=pl.BlockSpec((tm, tn), lambda i, j: (i, j)),
        out_shape=jax.ShapeDtypeStruct((m, n), F32),
        compiler_params=_cparams("parallel", "parallel"),
        name=name,
    )(a, w, res)


def _merge_body(x_ref, g_ref, *refs, first_tiles):
    o_first, o_rest = refs[0:N_BRANCH], refs[N_BRANCH:2 * N_BRANCH]
    wgs = refs[2 * N_BRANCH:3 * N_BRANCH]
    bg_ref, wb_ref, wo_ref, y_ref, h_ref, ob_ref = refs[3 * N_BRANCH:]
    i = pl.program_id(0)

    @pl.when(pl.program_id(1) == 0)
    def _():
        x = x_ref[...]
        ms = jnp.mean(x * x, axis=-1, keepdims=True)
        h_ref[...] = (x * lax.rsqrt(ms + EPS) * g_ref[...]).astype(BF16)
        y_ref[...] = x

        @pl.when(i < first_tiles)
        def _():
            for n in range(N_BRANCH):
                ob_ref[n] = o_first[n][...].astype(BF16)

        @pl.when(i >= first_tiles)
        def _():
            for n in range(N_BRANCH):
                ob_ref[n] = o_rest[n][...].astype(BF16)

    h = h_ref[...]
    merged = None
    for n in range(N_BRANCH):
        gate = _sigmoid(jnp.dot(h, wgs[n][...], preferred_element_type=F32) + bg_ref[n:n + 1, :])
        br = jnp.dot(ob_ref[n], wb_ref[n], preferred_element_type=F32)
        merged = gate * br if merged is None else merged + gate * br
    y_ref[...] += jnp.dot(merged.astype(BF16), wo_ref[...], preferred_element_type=F32)


def merge_out(x, g, outs_first, outs_rest, w_gate, b_gate, w_branch, w_out, *, tm, tn):
    m, d = x.shape
    nj = d // tn
    first_tiles = outs_first[0].shape[0] // tm
    assert outs_first[0].shape[0] % tm == 0 and outs_rest[0].shape[0] == m - first_tiles * tm
    wg_specs = [pl.BlockSpec((d, tn), functools.partial(lambda i, j, n: (0, n * nj + j), n=n))
                for n in range(N_BRANCH)]
    return pl.pallas_call(
        functools.partial(_merge_body, first_tiles=first_tiles),
        grid=(m // tm, nj),
        in_specs=[pl.BlockSpec((tm, d), lambda i, j: (i, 0)),
                  pl.BlockSpec((1, d), lambda i, j: (0, 0))]
                 + [pl.BlockSpec((tm, BRANCH_W), lambda i, j: (jnp.minimum(i, first_tiles - 1), 0))] * N_BRANCH
                 + [pl.BlockSpec((tm, BRANCH_W), lambda i, j: (jnp.maximum(i - first_tiles, 0), 0))] * N_BRANCH
                 + wg_specs
                 + [pl.BlockSpec((N_BRANCH, tn), lambda i, j: (0, j)),
                    pl.BlockSpec((N_BRANCH, BRANCH_W, tn), lambda i, j: (0, 0, j)),
                    pl.BlockSpec((tn, d), lambda i, j: (j, 0))],
        out_specs=pl.BlockSpec((tm, d), lambda i, j: (i, 0)),
        out_shape=jax.ShapeDtypeStruct((m, d), F32),
        scratch_shapes=[pltpu.VMEM((tm, d), BF16), pltpu.VMEM((N_BRANCH, tm, BRANCH_W), BF16)],
        compiler_params=_cparams("parallel", "arbitrary"),
        name="merge_out",
    )(x, g.reshape(1, d), *outs_first, *outs_rest, w_gate, w_gate, w_gate, w_gate, b_gate.reshape(N_BRANCH, d),
      w_branch, w_out)


DN_Z0 = DN_CONV_CH
DN_AB0 = DN_CONV_CH + BRANCH_W
DN_SLAB = DN_AB0 + 256
INV_BLOCK = 16
DN_STACK = 256


def _inv_unit_lower(nmat, size, c):
    row = _iota2((size, size), 0)
    col = _iota2((size, size), 1)
    eye = (row == col).astype(F32)
    blk = min(INV_BLOCK, c)
    if c > blk:
        same = (row // blk) == (col // blk)
        d = jnp.where(same, nmat, 0.0)
        r = jnp.where(same, 0.0, nmat)
    else:
        d, r = nmat, None
    t = eye - d
    p = d
    k = 2
    while k < blk:
        p = _dot_hp3(p, p)
        t = _dot_hp3(t, eye + p)
        k *= 2
    if r is None:
        return t
    pm = _dot_hp3(t, r)
    t2 = eye - pm
    q = pm
    k = 2
    while k < c // blk:
        q = _dot_hp3(q, q)
        t2 = _dot_hp3(t2, eye + q)
        k *= 2
    return _dot_hp3(t2, t)


def _dn_body(*refs, nb, c, valid):
    p_refs = refs[0:nb]
    cw_ref, alog_ref, dtb_ref, nw_ref, conv0_ref, s0_ref, o_ref, s1_ref, cbuf, s_scr = refs[nb:]
    ci = pl.program_id(1)
    nch = nb * DN_HEADS
    size = nch * c

    @pl.when(ci == 0)
    def _():
        s_scr[...] = s0_ref[...].reshape(s_scr.shape)
        cbuf[:, 0:SUBLANES, :] = conv0_ref[...]

    vcol = _iota2((c, 1), 0) < valid
    ltri = (_iota2((c, c), 0) >= _iota2((c, c), 1)).astype(BF16)
    q_l, k_l, v_l, z_l, gc_l, beta_l = [], [], [], [], [], []
    for sq in range(nb):
        p_ref, cb = p_refs[sq], cbuf.at[sq]
        cb[SUBLANES:SUBLANES + c, :] = p_ref[:, 0:DN_CONV_CH]
        conv = cw_ref[0:1, :] * cb[5:5 + c, :]
        for j in range(1, DN_CONV):
            conv = conv + cw_ref[j:j + 1, :] * cb[5 + j:5 + j + c, :]
        cb[0:SUBLANES, :] = cb[c:c + SUBLANES, :]
        act = _silu(conv)
        ab = p_ref[:, DN_AB0:DN_AB0 + LANES]
        g_all = jnp.where(vcol, -jnp.exp(alog_ref[...]) * _softplus(ab + dtb_ref[...]), 0.0)
        beta_all = jnp.where(vcol, _sigmoid(ab), 0.0)
        gc_all = _dot_hp_exact_lhs(ltri, g_all)
        for h in range(DN_HEADS):
            q = act[:, h * DN_DK:(h + 1) * DN_DK]
            k = act[:, 512 + h * DN_DK:512 + (h + 1) * DN_DK]
            q_l.append(q * lax.rsqrt(jnp.sum(q * q, axis=-1, keepdims=True) + EPS) * (DN_DK ** -0.5))
            k_l.append(jnp.where(vcol, k * lax.rsqrt(jnp.sum(k * k, axis=-1, keepdims=True) + EPS), 0.0))
            v_l.append(jnp.where(vcol, act[:, 1024 + h * DN_DV:1024 + (h + 1) * DN_DV], 0.0))
            z_l.append(p_ref[:, DN_Z0 + h * DN_DV:DN_Z0 + (h + 1) * DN_DV])
            gc_l.append(gc_all[:, h:h + 1])
            beta_l.append(beta_all[:, DN_HEADS + h:DN_HEADS + h + 1])
    stack = lambda xs: jnp.concatenate(xs, axis=0)
    q, k, v, z, gc, beta = (stack(x) for x in (q_l, k_l, v_l, z_l, gc_l, beta_l))

    row = _iota2((size, size), 0)
    col = _iota2((size, size), 1)
    same = (row // c) == (col // c)
    incl = same & (row >= col)
    strict = same & (row > col)
    gcr = _dot_hp_exact_lhs(jnp.ones((SUBLANES, size), BF16), jnp.where(row == col, gc, 0.0))[0:1, :]
    gamma = jnp.exp(jnp.where(incl, gc - gcr, -jnp.inf))
    tmat = _inv_unit_lower(jnp.where(strict, beta * _dot_nt(k, k) * gamma, 0.0), size, c)
    egc = jnp.exp(gc)
    u = _dot(tmat, beta * v)
    w = _dot(tmat, (beta * egc) * k)
    qk = jnp.where(incl, _dot_nt(q, k) * gamma, 0.0)
    qg = q * egc

    v_new, o_inter = [], []
    for i in range(nch):
        rs = slice(i * c, (i + 1) * c)
        s = s_scr[i]
        gl = gc[(i + 1) * c - 1:(i + 1) * c, :]
        vn = u[rs] - _dot(w[rs], s)
        o_inter.append(_dot(qg[rs], s))
        s_scr[i] = jnp.exp(gl) * s + _dot_tn(k[rs] * jnp.exp(gl - gc[rs]), vn)
        v_new.append(vn)
    o = stack(o_inter) + _dot(qk, stack(v_new))
    on = o * lax.rsqrt(jnp.mean(o * o, axis=-1, keepdims=True) + EPS) * nw_ref[...] * _silu(z)
    for i in range(nch):
        sq, h = divmod(i, DN_HEADS)
        o_ref[sq, :, h * DN_DV:(h + 1) * DN_DV] = on[i * c:(i + 1) * c]

    @pl.when(ci == pl.num_programs(1) - 1)
    def _():
        s1_ref[...] = s_scr[...].reshape(s1_ref.shape)


def deltanet(p_dn, conv_w, a_log, dt_bias, norm_w, conv0, s0, *, nseq, seq_len, c, valid, nb):
    m = nseq * seq_len
    n = seq_len // c
    pad = jnp.zeros((LANES - DN_HEADS,), F32)
    alog = jnp.concatenate([a_log, pad]).reshape(1, LANES)
    dtb = jnp.concatenate([dt_bias, pad]).reshape(1, LANES)
    full = lambda shape: pl.BlockSpec(shape, lambda b, i: (0,) * len(shape))
    seq_rows = lambda sq: functools.partial(lambda b, i, sq: ((b * nb + sq) * n + i, 0), sq=sq)
    o, s1 = pl.pallas_call(
        functools.partial(_dn_body, nb=nb, c=c, valid=valid),
        grid=(nseq // nb, n),
        in_specs=[pl.BlockSpec((c, DN_SLAB), seq_rows(sq)) for sq in range(nb)]
                 + [full((DN_CONV, DN_CONV_CH)), full((1, LANES)), full((1, LANES)), full((1, DN_DV)),
                    pl.BlockSpec((nb, SUBLANES, DN_CONV_CH), lambda b, i: (b, 0, 0)),
                    pl.BlockSpec((nb, DN_HEADS, DN_DK, DN_DV), lambda b, i: (b, 0, 0, 0))],
        out_specs=[pl.BlockSpec((nb, c, BRANCH_W), lambda b, i: (b, i, 0)),
                   pl.BlockSpec((nb, DN_HEADS, DN_DK, DN_DV), lambda b, i: (b, 0, 0, 0))],
        out_shape=[jax.ShapeDtypeStruct((nseq, seq_len, BRANCH_W), F32),
                   jax.ShapeDtypeStruct((nseq, DN_HEADS, DN_DK, DN_DV), F32)],
        scratch_shapes=[pltpu.VMEM((nb, c + 2 * SUBLANES, DN_CONV_CH), F32),
                        pltpu.VMEM((nb * DN_HEADS, DN_DK, DN_DV), F32)],
        compiler_params=_cparams("parallel", "arbitrary"),
        name="deltanet",
    )(*([p_dn] * nb), conv_w, alog, dtb, norm_w.reshape(1, DN_DV), conv0, s0)
    return o.reshape(m, BRANCH_W), s1


GLA_K0, GLA_V0, GLA_G0, GLA_GD0 = 256, 512, 1024, 1536
GLA_SLAB = 1792
GLA_CHUNK = 64
GLA_STACK = 256


def _gla_body(q_ref, k_ref, v_ref, g_ref, gd_ref, aup_ref, ab_ref, nw_ref, s0_ref,
              o_ref, s1_ref, st_scr, *, nb, c, valid):
    ci = pl.program_id(2)
    size = nb * c

    @pl.when(ci == 0)
    def _():
        st_scr[...] = s0_ref[...].reshape(st_scr.shape)

    row = _iota2((size, size), 0)
    col = _iota2((size, size), 1)
    incl = ((row // c) == (col // c)) & (row >= col)
    vcol = (_iota2((size, 1), 0) % c) < valid
    lane = _iota2((1, LANES), 1)
    rows2 = lambda ref, ls: ref[:, :, ls].reshape(size, ls.stop - ls.start)
    full = slice(0, LANES)

    gk = _log_sigmoid(_dot(rows2(gd_ref, full), aup_ref[...]) + ab_ref[...]) * (1.0 / GLA_TAU)
    gk = jnp.where(vcol, gk, 0.0)
    b = _dot_hp_exact_lhs(incl.astype(BF16), gk)
    q = rows2(q_ref, full) * (GLA_DK ** -0.5)
    k = jnp.where(vcol, rows2(k_ref, full), 0.0)
    qt = q * jnp.exp(b)
    kt = k * jnp.exp(-b)
    for hh in range(2):
        lm = (lane >= hh * GLA_DK) & (lane < (hh + 1) * GLA_DK)
        vs = slice(hh * GLA_DV, (hh + 1) * GLA_DV)
        qth = jnp.where(lm, qt, 0.0)
        v = jnp.where(vcol, rows2(v_ref, vs), 0.0)
        att = jnp.where(incl, _dot_nt(qth, kt), 0.0)
        o_inter = []
        for i in range(nb):
            rs = slice(i * c, (i + 1) * c)
            b_last = b[(i + 1) * c - 1:(i + 1) * c, :]
            st = st_scr[2 * i + hh]
            o_inter.append(_dot_nt(qth[rs], st))
            kd = jnp.where(lm, k[rs] * jnp.exp(b_last - b[rs]), 0.0)
            st_scr[2 * i + hh] = st * jnp.exp(b_last) + _dot_tn(v[rs], kd)
        o = _dot(att, v) + jnp.concatenate(o_inter, axis=0)
        on = o * lax.rsqrt(jnp.mean(o * o, axis=-1, keepdims=True) + EPS) * nw_ref[...] * _silu(rows2(g_ref, vs))
        o_ref[:, :, vs] = on.reshape(nb, c, GLA_DV)

    @pl.when(ci == pl.num_programs(2) - 1)
    def _():
        s1_ref[...] = st_scr[...].reshape(s1_ref.shape)


def gla(p_gla, a_up, a_b, norm_w, s0t, *, c, valid, nb):
    nseq, seq_len, _ = p_gla.shape
    n = seq_len // c
    aup = jnp.zeros((LANES, GLA_HEADS * GLA_DK), F32).at[:GLA_LORA].set(a_up).astype(BF16)
    cols = lambda w, off: pl.BlockSpec((nb, c, w), lambda b, p, i: (b, i, off // w + p))
    o, s1 = pl.pallas_call(
        functools.partial(_gla_body, nb=nb, c=c, valid=valid),
        grid=(nseq // nb, 2, n),
        in_specs=[cols(LANES, 0), cols(LANES, GLA_K0), cols(2 * GLA_DV, GLA_V0), cols(2 * GLA_DV, GLA_G0),
                  pl.BlockSpec((nb, c, LANES), lambda b, p, i: (b, i, GLA_GD0 // LANES)),
                  pl.BlockSpec((LANES, LANES), lambda b, p, i: (0, p)),
                  pl.BlockSpec((1, LANES), lambda b, p, i: (0, p)),
                  pl.BlockSpec((1, GLA_DV), lambda b, p, i: (0, 0)),
                  pl.BlockSpec((nb, 2, GLA_DV, LANES), lambda b, p, i: (b, p, 0, 0))],
        out_specs=[pl.BlockSpec((nb, c, 2 * GLA_DV), lambda b, p, i: (b, i, p)),
                   pl.BlockSpec((nb, 2, GLA_DV, LANES), lambda b, p, i: (b, p, 0, 0))],
        out_shape=[jax.ShapeDtypeStruct((nseq, seq_len, BRANCH_W), F32),
                   jax.ShapeDtypeStruct((nseq, GLA_HEADS, GLA_DV, LANES), F32)],
        scratch_shapes=[pltpu.VMEM((nb * 2, GLA_DV, LANES), F32)],
        compiler_params=_cparams("parallel", "parallel", "arbitrary"),
        name="gla",
    )(p_gla, p_gla, p_gla, p_gla, p_gla, aup, a_b.reshape(1, -1), norm_w.reshape(1, GLA_DV), s0t)
    return o.reshape(nseq * seq_len, BRANCH_W), s1


RW_WD0, RW_AD0, RW_GD0 = 1536, 1664, 1792
RW_SLAB = 2048
RW_QW = 256


def _seg_ones(n, seg):
    r = _iota2((n, n), 0) // seg
    c = _iota2((n, n), 1) // seg
    return (r == c).astype(BF16)


def _rw_prep_body(p_ref, prev_ref, mu_ref, w0_ref, a0_ref, kk_ref, ka_ref, rk_ref, wup_ref, aup_ref, gup_ref,
                  r_o, w_o, k_o, v_o, nkk_o, kka_o, gate_o, bonus_o, *scratch, tiles_per_seq):
    p = p_ref[...]
    if tiles_per_seq is None:
        prev = prev_ref[...]
    else:
        (buf,) = scratch
        tm = p.shape[0]

        @pl.when(pl.program_id(0) % tiles_per_seq == 0)
        def _():
            buf[0:SUBLANES, :] = prev_ref[...]

        buf[SUBLANES:SUBLANES + tm, :] = p
        prev = buf[SUBLANES - 1:SUBLANES - 1 + tm, :]
        buf[0:SUBLANES, :] = buf[tm:tm + SUBLANES, :]
    xr = p + mu_ref[...] * (prev - p)
    r = xr[:, 0:RW_W]
    k = xr[:, RW_W:2 * RW_W]
    v = xr[:, 2 * RW_W:3 * RW_W]
    wd = xr[:, RW_WD0:RW_WD0 + LANES]
    ad = xr[:, RW_AD0:RW_AD0 + LANES]
    gd = xr[:, RW_GD0:RW_GD0 + RW_LORA_G]
    logw = -jnp.exp(-_softplus(-(w0_ref[...] + _dot(jnp.tanh(wd), wup_ref[...]))) - 0.5)
    a = _sigmoid(a0_ref[...] + _dot(ad, aup_ref[...]))
    seg = _seg_ones(RW_W, RW_N)
    kr = k * kk_ref[...]
    kk = kr * lax.rsqrt(_dot_hp_exact_rhs(kr * kr, seg) + EPS)
    k2 = k * (1.0 + (a - 1.0) * ka_ref[...])
    r_o[...] = r
    w_o[...] = jnp.exp(logw)
    k_o[...] = k2
    v_o[...] = v
    nkk_o[...] = -kk
    kka_o[...] = kk * a
    gate_o[...] = _dot(_sigmoid(gd), gup_ref[...])
    bonus_o[...] = _dot_hp_exact_rhs(r * k2 * rk_ref[...], seg) * v


def rwkv_prep(p_rw, prev, mu, w0, a0, k_k, k_a, r_k, w_up, a_up, g_up, *, tm, seq_len=None):
    m = p_rw.shape[0]
    row = lambda w: pl.BlockSpec((tm, w), lambda i: (i, 0))
    full = lambda a: pl.BlockSpec(a.shape, lambda i: (0,) * a.ndim)
    args = (mu.reshape(1, -1), w0.reshape(1, -1), a0.reshape(1, -1), k_k.reshape(1, -1), k_a.reshape(1, -1),
            r_k.reshape(1, -1), w_up, a_up, g_up)
    in_kernel = seq_len is not None
    return pl.pallas_call(
        functools.partial(_rw_prep_body, tiles_per_seq=seq_len // tm if in_kernel else None),
        grid=(m // tm,),
        in_specs=[row(RW_SLAB), full(prev) if in_kernel else row(RW_SLAB)] + [full(a) for a in args],
        out_specs=[row(RW_W)] * 8,
        out_shape=[jax.ShapeDtypeStruct((m, RW_W), F32)] * 8,
        scratch_shapes=[pltpu.VMEM((tm + SUBLANES, RW_SLAB), F32)] if in_kernel else [],
        compiler_params=_cparams("arbitrary" if in_kernel else "parallel"),
        name="rwkv_prep",
    )(p_rw, prev, *args)


def _rw_scan_body(r_ref, w_ref, k_ref, v_ref, nkk_ref, kka_ref, s0_ref, y_ref, s1_ref, s_scr, *, nb, t_blk):
    ti = pl.program_id(1)

    @pl.when(ti == 0)
    def _():
        s_scr[...] = s0_ref[...].reshape(s_scr.shape)

    nq = RW_W // RW_QW
    tiles = [(b, q) for b in range(nb) for q in range(nq)]
    ones = _seg_ones(RW_QW, RW_N)
    diag = _iota2((RW_N, RW_QW), 0) == (_iota2((RW_N, RW_QW), 1) % RW_N)
    diag_b = diag.astype(BF16)
    tg = SUBLANES if t_blk % SUBLANES == 0 else t_blk
    dot = lambda a: jnp.dot(a, ones, preferred_element_type=F32)
    stack = lambda xs: jnp.concatenate(xs, axis=0)
    piece = lambda a, i: a[i * RW_N:(i + 1) * RW_N]

    def group(g, carry):
        base = pl.multiple_of(g * tg, tg)
        ld = lambda ref: [[ref[b, pl.ds(base, tg), q * RW_QW:(q + 1) * RW_QW] for q in range(nq)] for b in range(nb)]
        rb, wb, kb, vb, nkkb, kkab = (ld(ref) for ref in (r_ref, w_ref, k_ref, v_ref, nkk_ref, kka_ref))
        yrows = [[] for _ in tiles]
        for j in range(tg):
            row = lambda blk, b, q: blk[b][q][j:j + 1, :]
            s = [s_scr[i] for i in range(len(tiles))]
            sa = dot(stack([s[i] * row(nkkb, b, q) for i, (b, q) in enumerate(tiles)]).astype(BF16))
            v1 = [row(vb, b, q).astype(BF16) for b, q in tiles]
            v2 = [(row(vb, b, q) - v1[i].astype(F32)).astype(BF16) for i, (b, q) in enumerate(tiles)]
            vcol = dot(stack([diag_b * x for x in v2])) + dot(stack([diag_b * x for x in v1]))
            s = [s[i] * row(wb, b, q) + piece(sa, i) * row(kkab, b, q) + piece(vcol, i) * row(kb, b, q)
                 for i, (b, q) in enumerate(tiles)]
            for i in range(len(tiles)):
                s_scr[i] = s[i]
            yb = dot(stack([s[i] * row(rb, b, q) for i, (b, q) in enumerate(tiles)]).astype(BF16))
            for i in range(len(tiles)):
                yrows[i].append(jnp.sum(jnp.where(diag, piece(yb, i), 0.0), axis=0, keepdims=True))
        for i, (b, q) in enumerate(tiles):
            y_ref[b, pl.ds(base, tg), q * RW_QW:(q + 1) * RW_QW] = stack(yrows[i])
        return carry

    lax.fori_loop(0, t_blk // tg, group, 0)

    @pl.when(ti == pl.num_programs(1) - 1)
    def _():
        s1_ref[...] = s_scr[...].reshape(s1_ref.shape)


def rwkv_scan(r, w, k, v, nkk, kka, s0q, *, nb, t_blk):
    nseq, L, _ = r.shape
    nq = RW_W // RW_QW
    tok = pl.BlockSpec((nb, t_blk, RW_W), lambda b, t: (b, t, 0))
    st = pl.BlockSpec((nb, nq, RW_N, RW_QW), lambda b, t: (b, 0, 0, 0))
    return pl.pallas_call(
        functools.partial(_rw_scan_body, nb=nb, t_blk=t_blk),
        grid=(nseq // nb, L // t_blk),
        in_specs=[tok] * 6 + [st],
        out_specs=[tok, st],
        out_shape=[jax.ShapeDtypeStruct((nseq, L, RW_W), F32),
                   jax.ShapeDtypeStruct((nseq, nq, RW_N, RW_QW), F32)],
        scratch_shapes=[pltpu.VMEM((nb * nq, RW_N, RW_QW), F32)],
        compiler_params=_cparams("parallel", "arbitrary"),
        name="rwkv_scan",
    )(r, w, k, v, nkk, kka, s0q)


def rw_state_to_tiles(s):
    n = s.shape[0]
    nq = RW_W // RW_QW
    hq = RW_QW // RW_N
    return s.reshape(n, nq, hq, RW_N, RW_N).transpose(0, 1, 3, 2, 4).reshape(n, nq, RW_N, RW_QW)


def rw_tiles_to_state(t):
    n = t.shape[0]
    nq = RW_W // RW_QW
    hq = RW_QW // RW_N
    return t.reshape(n, nq, RW_N, hq, RW_N).transpose(0, 1, 3, 2, 4).reshape(n, RW_HEADS, RW_N, RW_N)


def _rw_post_body(y_ref, bonus_ref, gate_ref, lw_ref, lb_ref, o_ref):
    y = y_ref[...]
    seg = _seg_ones(RW_W, RW_N)
    mu = _dot_hp_exact_rhs(y, seg) * (1.0 / RW_N)
    d = y - mu
    var = _dot_hp_exact_rhs(d * d, seg) * (1.0 / RW_N)
    yn = d * lax.rsqrt(var + RW_GN_EPS) * lw_ref[...] + lb_ref[...]
    o_ref[...] = ((yn + bonus_ref[...]) * gate_ref[...]).astype(o_ref.dtype)


def rwkv_post(y, bonus, gate, ln_w, ln_b, *, tm):
    m = y.shape[0]
    row = pl.BlockSpec((tm, RW_W), lambda i: (i, 0))
    par = pl.BlockSpec((1, RW_W), lambda i: (0, 0))
    return pl.pallas_call(
        _rw_post_body,
        grid=(m // tm,),
        in_specs=[row, row, row, par, par],
        out_specs=row,
        out_shape=jax.ShapeDtypeStruct((m, RW_W), F32),
        compiler_params=_cparams("parallel"),
        name="rwkv_post",
    )(y, bonus, gate, ln_w.reshape(1, -1), ln_b.reshape(1, -1))


FOX_K0, FOX_V0, FOX_F0 = 512, 1024, 1536
FOX_SLAB = 1792
FOX_TQ = 512
NEG = -1e30


def _fox_prep_body(p_ref, ft_ref, qn_ref, kn_ref, fb_ref, fbt_ref,
                   q_o, k_o, kb_o, vb_o, lf_o, c_o, ct_o, carry, carry_t, *, tm):
    @pl.when(pl.program_id(1) == 0)
    def _():
        carry[...] = jnp.zeros_like(carry)
        carry_t[...] = jnp.zeros_like(carry_t)

    for h in range(FOX_HEADS):
        ls = slice(h * FOX_HD, (h + 1) * FOX_HD)
        q = p_ref[:, ls]
        k = p_ref[:, FOX_K0 + h * FOX_HD:FOX_K0 + (h + 1) * FOX_HD]
        qn = q * lax.rsqrt(jnp.mean(q * q, axis=-1, keepdims=True) + EPS) * qn_ref[...]
        kn = k * lax.rsqrt(jnp.mean(k * k, axis=-1, keepdims=True) + EPS) * kn_ref[...]
        q_o[:, ls] = (qn * (FOX_HD ** -0.5)).astype(BF16)
        k_o[:, ls] = kn
        kb_o[:, ls] = kn.astype(BF16)
    vb_o[...] = p_ref[:, FOX_V0:FOX_V0 + BRANCH_W].astype(BF16)
    lf = _log_sigmoid(p_ref[:, FOX_F0:FOX_F0 + LANES] + fb_ref[...])
    lf_o[...] = lf
    row = _iota2((tm, tm), 0)
    col = _iota2((tm, tm), 1)
    c = _dot_hp_exact_lhs((row >= col).astype(BF16), lf) + carry[...]
    c_o[...] = c
    carry[...] = c[tm - 1:tm, :]
    lft = _log_sigmoid(ft_ref[0] + fbt_ref[...])
    ct = _dot_hp_exact_rhs(lft, (row <= col).astype(BF16)) + carry_t[:, 0:1]
    ct_o[0, 0] = ct
    carry_t[...] = jnp.broadcast_to(ct[:, tm - 1:tm], carry_t.shape)


def fox_prep(p_fox, ft, qn_w, kn_w, fb, *, nseq, seq_len, tm):
    n = seq_len // tm
    m = nseq * seq_len
    fbp = jnp.concatenate([fb, jnp.zeros((LANES - FOX_HEADS,), F32)]).reshape(1, LANES)
    fbt = jnp.broadcast_to(jnp.concatenate([fb, jnp.zeros((4,), F32)])[:, None], (SUBLANES, tm))
    row = lambda w, dt: (pl.BlockSpec((tm, w), lambda b, i: (b * n + i, 0)), jax.ShapeDtypeStruct((m, w), dt))
    outs = [row(BRANCH_W, BF16), row(BRANCH_W, F32), row(BRANCH_W, BF16), row(BRANCH_W, BF16),
            row(LANES, F32), row(LANES, F32),
            (pl.BlockSpec((1, 1, SUBLANES, tm), lambda b, i: (b, i, 0, 0)),
             jax.ShapeDtypeStruct((nseq, n, SUBLANES, tm), F32))]
    return pl.pallas_call(
        functools.partial(_fox_prep_body, tm=tm),
        grid=(nseq, n),
        in_specs=[pl.BlockSpec((tm, FOX_SLAB), lambda b, i: (b * n + i, 0)),
                  pl.BlockSpec((1, SUBLANES, tm), lambda b, i: (b, 0, i)),
                  pl.BlockSpec((1, FOX_HD), lambda b, i: (0, 0)),
                  pl.BlockSpec((1, FOX_HD), lambda b, i: (0, 0)),
                  pl.BlockSpec((1, LANES), lambda b, i: (0, 0)),
                  pl.BlockSpec((SUBLANES, tm), lambda b, i: (0, 0))],
        out_specs=[o[0] for o in outs],
        out_shape=[o[1] for o in outs],
        scratch_shapes=[pltpu.VMEM((1, LANES), F32), pltpu.VMEM((SUBLANES, LANES), F32)],
        compiler_params=_cparams("parallel", "arbitrary"),
        name="fox_prep",
    )(p_fox, ft, qn_w.reshape(1, FOX_HD), kn_w.reshape(1, FOX_HD), fbp, fbt)


def _fox_prompt_body(q_ref, k_ref, v_ref, c_ref, ct_ref, o_ref, *, tq):
    h = pl.program_id(1)
    qi = pl.program_id(2)
    q = q_ref[...]
    lane = _iota2((1, LANES), 1)
    cq = jnp.sum(jnp.where(lane == h, c_ref[...], 0.0), axis=-1, keepdims=True)
    rowh = _iota2((SUBLANES, 1), 0) == h
    qpos = qi * tq + _iota2((tq, tq), 0)

    def kv_step(j, carry):
        m, l, acc = carry
        k = k_ref[pl.ds(pl.multiple_of(j * tq, tq), tq), :]
        v = v_ref[pl.ds(pl.multiple_of(j * tq, tq), tq), :]
        ck = jnp.sum(jnp.where(rowh, ct_ref[0, j], 0.0), axis=0, keepdims=True)
        s = _dot_nt(q, k) + (cq - ck)
        s = jnp.where(qpos >= j * tq + _iota2((tq, tq), 1), s, NEG)
        m_new = jnp.maximum(m, jnp.max(s, axis=-1, keepdims=True))
        p = jnp.exp(s - m_new)
        alpha = jnp.exp(m - m_new)
        return m_new, alpha * l + jnp.sum(p, axis=-1, keepdims=True), alpha * acc + _dot(p, v)

    init = (jnp.full((tq, 1), NEG, F32), jnp.zeros((tq, 1), F32), jnp.zeros((tq, FOX_HD), F32))
    m, l, acc = lax.fori_loop(0, qi + 1, kv_step, init)
    o_ref[...] = (acc / l).astype(o_ref.dtype)


def fox_prompt(qb, kb, vb, c, ct, *, nseq, L, tq):
    nq = L // tq
    return pl.pallas_call(
        functools.partial(_fox_prompt_body, tq=tq),
        grid=(nseq, FOX_HEADS, nq),
        in_specs=[pl.BlockSpec((tq, FOX_HD), lambda b, h, i: (b * nq + i, h)),
                  pl.BlockSpec((L, FOX_HD), lambda b, h, i: (b, h)),
                  pl.BlockSpec((L, FOX_HD), lambda b, h, i: (b, h)),
                  pl.BlockSpec((tq, LANES), lambda b, h, i: (b * nq + i, 0)),
                  pl.BlockSpec((1, nq, SUBLANES, tq), lambda b, h, i: (b, 0, 0, 0))],
        out_specs=pl.BlockSpec((tq, FOX_HD), lambda b, h, i: (b * nq + i, h)),
        out_shape=jax.ShapeDtypeStruct((nseq * L, BRANCH_W), F32),
        compiler_params=_cparams("parallel", "parallel", "arbitrary"),
        name="fox_prompt",
    )(qb, kb, vb, c, ct)


PAGE_ROWS = PAGE_SIZE * FOX_HEADS
NEW_ROWS = 32
LF_ROWS = 24


def _fox_sample_body(pt_ref, q_ref, *refs, n_pages, n_new):
    kc = refs[0:n_pages]
    vc = refs[n_pages:2 * n_pages]
    lfc = refs[2 * n_pages:3 * n_pages]
    kn_ref, vn_ref, lfn_ref, o_ref, s_scr = refs[3 * n_pages:]
    nq = q_ref.shape[1]
    nr = FOX_HEADS * nq
    q_all = jnp.concatenate([q_ref[0, :, h * FOX_HD:(h + 1) * FOX_HD] for h in range(FOX_HEADS)], axis=0)

    r = _iota2((PAGE_ROWS, PAGE_ROWS), 0)
    c = _iota2((PAGE_ROWS, PAGE_ROWS), 1)
    same_head = (r % FOX_HEADS) == (c % FOX_HEADS)
    cum_mat = (same_head & (r // FOX_HEADS <= c // FOX_HEADS)).astype(BF16)
    tot_mat = same_head.astype(BF16)
    lf_rows = jnp.concatenate([lfc[i][...] for i in range(n_pages)] + [lfn_ref[0]]
                              + [jnp.zeros((LF_ROWS - n_pages - 1, PAGE_ROWS), F32)], axis=0)
    earlier = (_iota2((LF_ROWS, LF_ROWS), 0) > _iota2((LF_ROWS, LF_ROWS), 1)).astype(BF16)
    bias = (_dot_hp_exact_rhs(lf_rows, cum_mat)
            + _dot_hp_exact_lhs(earlier, _dot_hp_exact_rhs(lf_rows, tot_mat)))

    head_ok = (_iota2((nr, PAGE_ROWS), 0) // nq) == (_iota2((nr, PAGE_ROWS), 1) % FOX_HEADS)
    m = jnp.full((nr, PAGE_ROWS), NEG, F32)
    for i in range(n_pages):
        s = jnp.where(head_ok, _dot_nt(q_all, kc[i][...]) - bias[i:i + 1, :], NEG)
        s_scr[i] = s
        m = jnp.maximum(m, s)
    kr = _iota2((nr, NEW_ROWS), 1)
    qr = _iota2((nr, NEW_ROWS), 0)
    new_ok = ((qr // nq) == (kr % FOX_HEADS)) & (kr // FOX_HEADS <= qr % nq) & (kr < n_new * FOX_HEADS)
    s_new = jnp.where(new_ok, _dot_nt(q_all, kn_ref[0]) - bias[n_pages:n_pages + 1, 0:NEW_ROWS], NEG)
    m = jnp.maximum(jnp.max(m, axis=-1, keepdims=True), jnp.max(s_new, axis=-1, keepdims=True))

    p_new = jnp.exp(s_new - m)
    l = jnp.sum(p_new, axis=-1, keepdims=True)
    acc = _dot(p_new, vn_ref[0])
    psum = jnp.zeros((nr, PAGE_ROWS), F32)
    for i in range(n_pages):
        p = jnp.exp(s_scr[i] - m)
        psum = psum + p
        acc = acc + _dot(p, vc[i][...])
    out = acc / (l + jnp.sum(psum, axis=-1, keepdims=True))
    for h in range(FOX_HEADS):
        o_ref[0, :, h * FOX_HD:(h + 1) * FOX_HD] = out[h * nq:(h + 1) * nq]


def fox_sample(page_table, qb, cache_k, cache_v, cache_lf, kn, vn, lfn, *, layer, n_new):
    nb, nq, _ = qb.shape
    n_pages = page_table.shape[1]
    page = lambda shape, i: pl.BlockSpec(
        (None, None) + shape, functools.partial(lambda b, pt, i: (layer, pt[b * n_pages + i], 0, 0), i=i))
    per_b = lambda shape: pl.BlockSpec((1,) + shape, lambda b, pt: (b, 0, 0))
    in_specs = ([per_b((nq, BRANCH_W))]
                + [page((PAGE_ROWS, FOX_HD), i) for i in range(n_pages)]
                + [page((PAGE_ROWS, FOX_HD), i) for i in range(n_pages)]
                + [page((1, PAGE_ROWS), i) for i in range(n_pages)]
                + [per_b((NEW_ROWS, FOX_HD)), per_b((NEW_ROWS, FOX_HD)), per_b((1, PAGE_ROWS))])
    grid_spec = pltpu.PrefetchScalarGridSpec(
        num_scalar_prefetch=1, grid=(nb,), in_specs=in_specs, out_specs=per_b((nq, BRANCH_W)),
        scratch_shapes=[pltpu.VMEM((n_pages, FOX_HEADS * nq, PAGE_ROWS), F32)])
    return pl.pallas_call(
        functools.partial(_fox_sample_body, n_pages=n_pages, n_new=n_new),
        grid_spec=grid_spec,
        out_shape=jax.ShapeDtypeStruct((nb, nq, BRANCH_W), F32),
        compiler_params=_cparams("parallel"),
        name="fox_sample",
    )(page_table.reshape(-1), qb, *([cache_k] * n_pages), *([cache_v] * n_pages), *([cache_lf] * n_pages),
      kn, vn, lfn)


def _mem_attn_body(q_ref, k_ref, v_ref, o_ref):
    for h in range(MEM_HEADS):
        ls = slice(h * MEM_HD, (h + 1) * MEM_HD)
        s = _dot_nt(q_ref[:, ls], k_ref[0, :, ls]) * (MEM_HD ** -0.5)
        p = jnp.exp(s - jnp.max(s, axis=-1, keepdims=True))
        o_ref[:, ls] = _dot(p, v_ref[0, :, ls]) / jnp.sum(p, axis=-1, keepdims=True)


def mem_attn(q, mk, mv, *, nseq, seq_len, tq):
    m = nseq * seq_len
    n = seq_len // tq
    kv = pl.BlockSpec((1, MEM_LEN, BRANCH_W), lambda b, i: (b, 0, 0))
    return pl.pallas_call(
        _mem_attn_body,
        grid=(nseq, n),
        in_specs=[pl.BlockSpec((tq, BRANCH_W), lambda b, i: (b * n + i, 0)), kv, kv],
        out_specs=pl.BlockSpec((tq, BRANCH_W), lambda b, i: (b * n + i, 0)),
        out_shape=jax.ShapeDtypeStruct((m, BRANCH_W), F32),
        compiler_params=_cparams("parallel", "parallel"),
        name="mem_attn",
    )(q, mk, mv)


def _mem_attn_rows_body(q_ref, k_ref, v_ref, o_ref):
    nq = q_ref.shape[0]
    nr = MEM_HEADS * nq
    nk = MEM_LEN * MEM_HEADS
    q_all = jnp.concatenate([q_ref[:, h * MEM_HD:(h + 1) * MEM_HD] for h in range(MEM_HEADS)], axis=0)
    head_ok = (_iota2((nr, nk), 0) // nq) == (_iota2((nr, nk), 1) % MEM_HEADS)
    s = jnp.where(head_ok, _dot_nt(q_all, k_ref[...]) * (MEM_HD ** -0.5), NEG)
    p = jnp.exp(s - jnp.max(s, axis=-1, keepdims=True))
    out = _dot(p, v_ref[...]) / jnp.sum(p, axis=-1, keepdims=True)
    for h in range(MEM_HEADS):
        o_ref[:, h * MEM_HD:(h + 1) * MEM_HD] = out[h * nq:(h + 1) * nq]


def mem_attn_rows(q, mk, mv, *, layer, nseq, tq):
    kv = pl.BlockSpec((None, None, MEM_LEN * MEM_HEADS, MEM_HD), lambda b: (layer, b, 0, 0))
    return pl.pallas_call(
        _mem_attn_rows_body,
        grid=(nseq,),
        in_specs=[pl.BlockSpec((tq, BRANCH_W), lambda b: (b, 0)), kv, kv],
        out_specs=pl.BlockSpec((tq, BRANCH_W), lambda b: (b, 0)),
        out_shape=jax.ShapeDtypeStruct((nseq * tq, BRANCH_W), F32),
        compiler_params=_cparams("parallel"),
        name="mem_attn_rows",
    )(q, mk, mv)


def gla_state_to_tiles(s):
    st = jnp.swapaxes(s, -1, -2)
    z = jnp.zeros_like(st)
    even = jnp.concatenate([st, z], axis=-1)
    odd = jnp.concatenate([z, st], axis=-1)
    sel = (jnp.arange(GLA_HEADS) % 2 == 1)[None, :, None, None]
    return jnp.where(sel, odd, even)


def gla_tiles_to_state(t):
    even = t[..., :GLA_DK]
    odd = t[..., GLA_DK:]
    sel = (jnp.arange(GLA_HEADS) % 2 == 1)[None, :, None, None]
    return jnp.swapaxes(jnp.where(sel, odd, even), -1, -2)


ROUTER_W = LANES
MOE_TB = 256


def _router_body(x_ref, g_ref, w1_ref, w2_ref, b_ref, lo_ref, h_ref):
    x = x_ref[...]
    h = x * lax.rsqrt(jnp.mean(x * x, axis=-1, keepdims=True) + EPS) * g_ref[...]
    h1 = h.astype(BF16)
    h2 = (h - h1.astype(F32)).astype(BF16)
    d = lambda a, b: jnp.dot(a, b, preferred_element_type=F32)
    lo_ref[...] = (d(h2, w1_ref[...]) + d(h1, w2_ref[...])) + d(h1, w1_ref[...]) + b_ref[...]
    half = h.shape[1] // 2
    u = pltpu.bitcast(h, jnp.uint32)
    rne = (u + jnp.uint32(0x7FFF) + ((u >> 16) & jnp.uint32(1))) & jnp.uint32(0xFFFF0000)
    h_ref[...] = rne[:, :half] | (rne[:, half:] >> 16)


def _unpack_bf16_pairs(packed):
    hi = pltpu.bitcast(packed & jnp.uint32(0xFFFF0000), F32).astype(BF16)
    lo = pltpu.bitcast(packed << 16, F32).astype(BF16)
    return hi, lo


def router(x, g, w_hi, w_lo, b, *, tm):
    m, d = x.shape
    full = lambda a: pl.BlockSpec(a.shape, lambda i: (0,) * a.ndim)
    g2 = g.reshape(1, d)
    return pl.pallas_call(
        _router_body,
        grid=(m // tm,),
        in_specs=[pl.BlockSpec((tm, d), lambda i: (i, 0)), full(g2), full(w_hi), full(w_lo), full(b)],
        out_specs=[pl.BlockSpec((tm, ROUTER_W), lambda i: (i, 0)), pl.BlockSpec((tm, d // 2), lambda i: (i, 0))],
        out_shape=[jax.ShapeDtypeStruct((m, ROUTER_W), F32), jax.ShapeDtypeStruct((m, d // 2), jnp.uint32)],
        compiler_params=_cparams("parallel"),
        name="router",
    )(x, g2, w_hi, w_lo, b)


def _moe_body(be_ref, nv_ref, xs_ref, w1_ref, w3_ref, w2_ref, o_ref, w1b, w3b, w2b):
    i = pl.program_id(0)

    @pl.when(i < nv_ref[0])
    def _():
        @pl.when((i == 0) | (be_ref[i] != be_ref[jnp.maximum(i - 1, 0)]))
        def _():
            w1b[...] = w1_ref[...].astype(BF16)
            w3b[...] = w3_ref[...].astype(BF16)
            w2b[...] = w2_ref[...].astype(BF16)

        x_hi, x_lo = _unpack_bf16_pairs(xs_ref[...])
        half = x_hi.shape[1]
        up = lambda w: (jnp.dot(x_hi, w[0:half, :], preferred_element_type=F32)
                        + jnp.dot(x_lo, w[half:, :], preferred_element_type=F32))
        o_ref[...] = jnp.dot((_silu(up(w1b)) * up(w3b)).astype(BF16), w2b[...], preferred_element_type=F32)

    @pl.when(i >= nv_ref[0])
    def _():
        o_ref[...] = jnp.zeros_like(o_ref)


def moe_experts(block_e, n_valid, xs, w1, w3, w2, *, layer):
    n_rows = xs.shape[0]
    d, f = w1.shape[-2:]
    grid_spec = pltpu.PrefetchScalarGridSpec(
        num_scalar_prefetch=2, grid=(n_rows // MOE_TB,),
        in_specs=[pl.BlockSpec((MOE_TB, d // 2), lambda i, be, nv: (i, 0)),
                  pl.BlockSpec((None, None, d, f), lambda i, be, nv: (layer, be[i], 0, 0)),
                  pl.BlockSpec((None, None, d, f), lambda i, be, nv: (layer, be[i], 0, 0)),
                  pl.BlockSpec((None, None, f, d), lambda i, be, nv: (layer, be[i], 0, 0))],
        out_specs=pl.BlockSpec((MOE_TB, d), lambda i, be, nv: (i, 0)),
        scratch_shapes=[pltpu.VMEM((d, f), BF16), pltpu.VMEM((d, f), BF16), pltpu.VMEM((f, d), BF16)])
    return pl.pallas_call(
        _moe_body,
        grid_spec=grid_spec,
        out_shape=jax.ShapeDtypeStruct((n_rows, d), F32),
        compiler_params=_cparams("arbitrary"),
        name="moe_experts",
    )(block_e, n_valid, xs, w1, w3, w2)


def _route(logits):
    m = logits.shape[0]
    pg = jax.nn.softmax(logits[:, :N_GROUPS], axis=-1)
    grp = jnp.argmax(pg, axis=-1)
    pg_top = jnp.max(pg, axis=-1)
    le = logits[:, N_GROUPS:N_GROUPS + N_EXPERTS].reshape(m, N_GROUPS, EXPERTS_PER_GROUP)
    pe = jax.nn.softmax(le[jnp.arange(m), grp], axis=-1)
    top_p, top_i = lax.top_k(pe, TOP_K)
    wts = pg_top[:, None] * top_p / jnp.sum(top_p, axis=-1, keepdims=True)
    eid = grp[:, None] * EXPERTS_PER_GROUP + top_i
    return eid, wts


def _sorted_layout(eid, m):
    n_slots = m * TOP_K
    flat_e = eid.reshape(n_slots)
    order = jnp.argsort(flat_e)
    sorted_e = flat_e[order]
    counts = jnp.bincount(flat_e, length=N_EXPERTS)
    padded = (counts + MOE_TB - 1) // MOE_TB * MOE_TB
    pad_end = jnp.cumsum(padded)
    pad_start = pad_end - padded
    start = jnp.cumsum(counts) - counts
    dest = (pad_start[sorted_e] + jnp.arange(n_slots) - start[sorted_e]).astype(jnp.int32)
    n_blocks = -(-n_slots // MOE_TB) + N_EXPERTS
    block_e = jnp.minimum(jnp.searchsorted(pad_end, jnp.arange(n_blocks) * MOE_TB, side='right'),
                          N_EXPERTS - 1).astype(jnp.int32)
    n_valid = (pad_end[-1:] // MOE_TB).astype(jnp.int32)
    rows = jnp.arange(n_blocks * MOE_TB)
    row_e = jnp.repeat(block_e, MOE_TB)
    i_in_e = rows - pad_start[row_e]
    src = jnp.clip(start[row_e] + i_in_e, 0, n_slots - 1)
    slot_tok = jnp.where(i_in_e < counts[row_e], order[src] // TOP_K, rows % m).astype(jnp.int32)
    pos = dest[jnp.argsort(order)].reshape(m, TOP_K)
    return slot_tok, block_e, n_valid, pos


def _rms_rows_body(x_ref, g_ref, o_ref):
    x = x_ref[...]
    o_ref[...] = x * lax.rsqrt(jnp.mean(x * x, axis=-1, keepdims=True) + EPS) * g_ref[...]


def rms_rows(x, g, *, tm):
    m, d = x.shape
    return pl.pallas_call(
        _rms_rows_body,
        grid=(m // tm,),
        in_specs=[pl.BlockSpec((tm, d), lambda i: (i, 0)), pl.BlockSpec((1, d), lambda i: (0, 0))],
        out_specs=pl.BlockSpec((tm, d), lambda i: (i, 0)),
        out_shape=jax.ShapeDtypeStruct((m, d), F32),
        compiler_params=_cparams("parallel"),
        name="rms_rows",
    )(x, g.reshape(1, d))


def _slab(w, pieces, width):
    cols = []
    for start, stop, padded in pieces:
        cols.append(w[..., start:stop])
        if padded > stop - start:
            cols.append(jnp.zeros(w.shape[:-1] + (padded - (stop - start),), w.dtype))
    out = jnp.concatenate(cols, axis=-1)
    assert out.shape[-1] == width, (out.shape, width)
    return out


_DN0, _RW0, _GLA0, _FOX0 = 0, 2056, 4040, 5592
_DN_PIECES = [(_DN0, _DN0 + 2048, 2048), (_DN0 + 2048, _DN0 + 2056, 256)]
_RW_REL = [(0, 1536, 1536), (1536, 1632, 128), (1632, 1728, 128), (1728, 1984, 256)]
_RW_PIECES = [(_RW0 + a, _RW0 + b, p) for a, b, p in _RW_REL]
_GLA_PIECES = [(_GLA0, _GLA0 + 1536, 1536), (_GLA0 + 1536, _GLA0 + 1552, 256)]
_FOX_PIECES = [(_FOX0, _FOX0 + 1536, 1536), (_FOX0 + 1536, _FOX0 + 1540, 256)]


def _rw_unslab(x):
    return jnp.concatenate([x[..., :1632], x[..., RW_AD0:RW_AD0 + 96], x[..., RW_GD0:]], axis=-1)


def _pad_rows8(x, nseq, seq_len):
    return jnp.pad(x.reshape(nseq, seq_len, x.shape[-1]), ((0, 0), (0, SUBLANES - seq_len), (0, 0)))


def kernel(x_prompt, x_sample, cache_fox_k, cache_fox_v, cache_fox_logf, state_dn_conv, state_dn, state_rw_shift, state_rw, state_gla, cache_mem_k, cache_mem_v, page_table, mem_prompt, norm_mix, w_in, dn_conv_w, dn_A_log, dn_dt_bias, dn_norm_w, rw_mu, rw_w0, rw_w_up, rw_a0, rw_a_up, rw_g_up, rw_k_k, rw_k_a, rw_r_k, rw_ln_w, rw_ln_b, gla_a_up, gla_a_b, gla_norm_w, fox_qn, fox_kn, fox_fb, w_gate, b_gate, w_branch, w_out, norm_mem, norm_memkv, mem_wq, mem_wk, mem_wv, mem_wo, norm_ffn, router_g, router_g_b, router_e, router_e_b, moe_w1, moe_w3, moe_w2, norm_final):
    bp, lp, d = x_prompt.shape
    bs, ls, _ = x_sample.shape
    depth = w_in.shape[0]
    n_p = bp * lp
    n_s = bs * ls
    m = n_p + n_s
    n_phys = cache_fox_k.shape[1]
    n_mem = mem_prompt.shape[1]

    w_dn = _slab(w_in, _DN_PIECES, DN_SLAB).astype(BF16)
    w_rw = _slab(w_in, _RW_PIECES, RW_SLAB).astype(BF16)
    w_gla = _slab(w_in, _GLA_PIECES, GLA_SLAB).astype(BF16)
    w_fox = _slab(w_in, _FOX_PIECES, FOX_SLAB).astype(BF16)
    mu_r = _slab(rw_mu, _RW_REL, RW_SLAB)
    pad_lora = lambda w: jnp.pad(w, ((0, 0), (0, LANES - w.shape[1]), (0, 0))).astype(BF16)
    rw_w_up_b, rw_a_up_b, rw_g_up_b = pad_lora(rw_w_up), pad_lora(rw_a_up), rw_g_up.astype(BF16)
    w_gate_b, w_branch_b, w_out_b = w_gate.astype(BF16), w_branch.astype(BF16), w_out.astype(BF16)
    mem_wq_b, mem_wo_b = mem_wq.astype(BF16), mem_wo.astype(BF16)
    mem_wkv_b = jnp.concatenate([mem_wk, mem_wv], axis=-1).astype(BF16)
    w_router = jnp.pad(jnp.concatenate([router_g, router_e], axis=-1),
                       ((0, 0), (0, 0), (0, ROUTER_W - N_GROUPS - N_EXPERTS)))
    w_router_hi = w_router.astype(BF16)
    w_router_lo = (w_router - w_router_hi.astype(F32)).astype(BF16)
    b_router = jnp.pad(jnp.concatenate([router_g_b, router_e_b], axis=-1),
                       ((0, 0), (0, ROUTER_W - N_GROUPS - N_EXPERTS))).reshape(depth, 1, ROUTER_W)

    x = jnp.concatenate([x_prompt.reshape(n_p, d), x_sample.reshape(n_s, d)], axis=0)
    mem_x = mem_prompt.reshape(bp * n_mem, d)
    zeros = lambda *shape: jnp.zeros(shape, F32)
    rows = lambda a_p, a_s: jnp.concatenate([a_p, a_s], axis=0)
    unpad8 = lambda a: a.reshape(bs, SUBLANES, a.shape[-1])[:, :ls].reshape(n_s, a.shape[-1])
    cache_k_rows = cache_fox_k.reshape(depth, n_phys, PAGE_ROWS, FOX_HD)
    cache_v_rows = cache_fox_v.reshape(depth, n_phys, PAGE_ROWS, FOX_HD)
    cache_lf_rows = cache_fox_logf.reshape(depth, n_phys, 1, PAGE_ROWS)
    mem_k_rows = cache_mem_k.reshape(depth, bs, n_mem * MEM_HEADS, MEM_HD)
    mem_v_rows = cache_mem_v.reshape(depth, bs, n_mem * MEM_HEADS, MEM_HD)
    p_out, s_out = [], []

    for l in range(depth):
        g_mix = norm_mix[l]
        proj = lambda w, tn, name: (
            rms_matmul(x, g_mix, w, tm=1024, tn=tn, rows=n_p, name=name),
            rms_matmul(x, g_mix, w, tm=512, tn=tn, row0=n_p, rows=n_s, name=name + "_s"))
        p_dn, p_dn_s = proj(w_dn[l], 768, "in_proj_dn")
        p_rw, p_rw_s = proj(w_rw[l], 1024, "in_proj_rw")
        p_gla, p_gla_s = proj(w_gla[l], 896, "in_proj_gla")
        p_fox, pf_s = proj(w_fox[l], 896, "in_proj_fox")

        dn_args = (dn_conv_w[l], dn_A_log[l], dn_dt_bias[l], dn_norm_w[l])
        o_dn_p, dn_p = deltanet(p_dn, *dn_args, zeros(bp, SUBLANES, DN_CONV_CH),
                                zeros(bp, DN_HEADS, DN_DK, DN_DV), nseq=bp, seq_len=lp, c=DN_CHUNK, valid=DN_CHUNK,
                                nb=DN_STACK // (DN_HEADS * DN_CHUNK))
        dn_slab_s = _pad_rows8(p_dn_s, bs, ls)
        conv0_s = jnp.pad(state_dn_conv[l], ((0, 0), (SUBLANES - DN_CONV + 1, 0), (0, 0)))
        o_dn_s, dn_s = deltanet(dn_slab_s.reshape(bs * SUBLANES, DN_SLAB), *dn_args, conv0_s, state_dn[l],
                                nseq=bs, seq_len=SUBLANES, c=SUBLANES, valid=ls,
                                nb=DN_STACK // (DN_HEADS * SUBLANES))
        dn_conv_p = p_dn.reshape(bp, lp, DN_SLAB)[:, lp - (DN_CONV - 1):, :DN_CONV_CH]
        dn_conv_s = dn_slab_s[:, ls - (DN_CONV - 1):ls, :DN_CONV_CH]

        rw_p = p_rw.reshape(bp, lp, RW_SLAB)
        rw_s = p_rw_s.reshape(bs, ls, RW_SLAB)
        prev_s = jnp.concatenate([_slab(state_rw_shift[l], _RW_REL, RW_SLAB)[:, None], rw_s[:, :-1]],
                                 axis=1).reshape(n_s, RW_SLAB)
        rw_args = (mu_r[l], rw_w0[l], rw_a0[l], rw_k_k[l], rw_k_a[l], rw_r_k[l].reshape(-1), rw_w_up_b[l],
                   rw_a_up_b[l], rw_g_up_b[l])
        tok_p = rwkv_prep(p_rw, zeros(SUBLANES, RW_SLAB), *rw_args, tm=256, seq_len=lp)
        tok_s = rwkv_prep(p_rw_s, prev_s, *rw_args, tm=256)
        y_p, rw_tiles_p = rwkv_scan(*(a.reshape(bp, lp, RW_W) for a in tok_p[:6]),
                                    zeros(bp, RW_W // RW_QW, RW_N, RW_QW), nb=bp, t_blk=64)
        y_s, rw_tiles_s = rwkv_scan(*(a.reshape(bs, ls, RW_W) for a in tok_s[:6]),
                                    rw_state_to_tiles(state_rw[l]), nb=8, t_blk=ls)
        o_rw_p = rwkv_post(y_p.reshape(n_p, RW_W), tok_p[7], tok_p[6], rw_ln_w[l], rw_ln_b[l], tm=512)
        o_rw_s = rwkv_post(y_s.reshape(n_s, RW_W), tok_s[7], tok_s[6], rw_ln_w[l], rw_ln_b[l], tm=512)

        gla_args = (gla_a_up[l], gla_a_b[l], gla_norm_w[l])
        o_gla_p, gla_tiles_p = gla(p_gla.reshape(bp, lp, GLA_SLAB), *gla_args,
                                   zeros(bp, GLA_HEADS, GLA_DV, LANES), c=GLA_CHUNK, valid=GLA_CHUNK,
                                   nb=GLA_STACK // GLA_CHUNK)
        o_gla_s, gla_tiles_s = gla(_pad_rows8(p_gla_s, bs, ls), *gla_args, gla_state_to_tiles(state_gla[l]),
                                   c=SUBLANES, valid=ls, nb=GLA_STACK // SUBLANES)

        fox_args = (fox_qn[l], fox_kn[l], fox_fb[l])
        ft_p = p_fox[:, FOX_F0:FOX_F0 + SUBLANES].reshape(bp, lp, SUBLANES).transpose(0, 2, 1)
        qb_p, kn_p, kb_p, vb_p, lf_p, c_p, ct_p = fox_prep(p_fox, ft_p, *fox_args, nseq=bp, seq_len=lp, tm=FOX_TQ)
        o_fox_p = fox_prompt(qb_p, kb_p, vb_p, c_p, ct_p, nseq=bp, L=lp, tq=FOX_TQ)
        ft_s = pf_s[:, FOX_F0:FOX_F0 + SUBLANES].T[None]
        qb_s, kn_s, _, _, lf_s, _, _ = fox_prep(pf_s, ft_s, *fox_args, nseq=1, seq_len=n_s, tm=256)
        new_rows = lambda a: jnp.pad(a.reshape(bs, ls * FOX_HEADS, FOX_HD),
                                     ((0, 0), (0, NEW_ROWS - ls * FOX_HEADS), (0, 0)))
        lfn = jnp.pad(lf_s[:, :FOX_HEADS].reshape(bs, 1, ls * FOX_HEADS),
                      ((0, 0), (0, 0), (0, PAGE_ROWS - ls * FOX_HEADS)))
        o_fox_s = fox_sample(page_table, _pad_rows8(qb_s, bs, ls), cache_k_rows, cache_v_rows, cache_lf_rows,
                             new_rows(kn_s), new_rows(pf_s[:, FOX_V0:FOX_V0 + BRANCH_W]), lfn, layer=l, n_new=ls)

        outs_p = [o_dn_p, o_rw_p, o_gla_p, o_fox_p]
        outs_s = [unpad8(o_dn_s), o_rw_s, unpad8(o_gla_s), o_fox_s[:, :ls].reshape(n_s, BRANCH_W)]
        x = merge_out(x, g_mix, outs_p, outs_s, w_gate_b[l], b_gate[l], w_branch_b[l], w_out_b[l], tm=512, tn=256)

        q_mem = rms_matmul(x, norm_mem[l], mem_wq_b[l], tm=512, tn=512, name="mem_q")
        mkv = rms_matmul(mem_x, norm_memkv[l], mem_wkv_b[l], tm=512, tn=512, name="mem_kv")
        mk = mkv[:, :BRANCH_W].reshape(bp, n_mem, BRANCH_W)
        mv = mkv[:, BRANCH_W:].reshape(bp, n_mem, BRANCH_W)
        att_p = mem_attn(q_mem, mk, mv, nseq=bp, seq_len=lp, tq=512)
        att_s = mem_attn_rows(_pad_rows8(q_mem[n_p:], bs, ls).reshape(bs * SUBLANES, BRANCH_W), mem_k_rows,
                              mem_v_rows, layer=l, nseq=bs, tq=SUBLANES)
        x = matmul_res(rows(att_p, unpad8(att_s)), mem_wo_b[l], x, tm=512, tn=1024, name="mem_out")

        logits, h_ffn = router(x, norm_ffn[l], w_router_hi[l], w_router_lo[l], b_router[l], tm=512)
        eid, wts = _route(logits)
        slot_tok, block_e, n_valid, pos = _sorted_layout(eid, m)
        xs = h_ffn[slot_tok]
        ys = moe_experts(block_e, n_valid, xs, moe_w1, moe_w3, moe_w2, layer=l)
        x = x + wts[:, 0:1] * ys[pos[:, 0]] + wts[:, 1:2] * ys[pos[:, 1]]

        p_out.append((dn_conv_p, dn_p, _rw_unslab(rw_p[:, -1]), rw_tiles_to_state(rw_tiles_p),
                      gla_tiles_to_state(gla_tiles_p),
                      kn_p.reshape(bp, lp, FOX_HEADS, FOX_HD),
                      p_fox[:, FOX_V0:FOX_V0 + BRANCH_W].reshape(bp, lp, FOX_HEADS, FOX_HD),
                      lf_p[:, :FOX_HEADS].reshape(bp, lp, FOX_HEADS),
                      mk.reshape(bp, n_mem, MEM_HEADS, MEM_HD), mv.reshape(bp, n_mem, MEM_HEADS, MEM_HD)))
        s_out.append((dn_conv_s, dn_s, _rw_unslab(rw_s[:, -1]), rw_tiles_to_state(rw_tiles_s),
                      gla_tiles_to_state(gla_tiles_s),
                      kn_s.reshape(bs, ls, FOX_HEADS, FOX_HD),
                      pf_s[:, FOX_V0:FOX_V0 + BRANCH_W].reshape(bs, ls, FOX_HEADS, FOX_HD),
                      lf_s[:, :FOX_HEADS].reshape(bs, ls, FOX_HEADS)))

    (p_dn_conv, p_dn_st, p_rw_shift, p_rw_st, p_gla_st, p_fox_k, p_fox_v, p_fox_logf, p_mem_k,
     p_mem_v) = [jnp.stack(r) for r in zip(*p_out)]
    (s_dn_conv, s_dn_st, s_rw_shift, s_rw_st, s_gla_st, s_fox_k, s_fox_v,
     s_fox_logf) = [jnp.stack(r) for r in zip(*s_out)]
    y = rms_rows(x, norm_final, tm=512)
    y_prompt = y[:n_p].reshape(bp, lp, d)
    y_sample = y[n_p:].reshape(bs, ls, d)
    return (y_prompt, y_sample, p_fox_k, p_fox_v, p_fox_logf, p_dn_conv, p_dn_st, p_rw_shift, p_rw_st, p_gla_st,
            p_mem_k, p_mem_v, s_fox_k, s_fox_v, s_fox_logf, s_dn_conv, s_dn_st, s_rw_shift, s_rw_st, s_gla_st)
```

```python
import functools
import math

import jax
import jax.numpy as jnp
from jax import lax
from jax.experimental import pallas as pl
from jax.experimental.pallas import tpu as pltpu

F32 = jnp.float32
BF16 = jnp.bfloat16
EPS = 1e-6
HP = lax.Precision.HIGHEST

D_MODEL = 2048
N_BRANCH = 4
BRANCH_W = 512
PAGE_SIZE = 128

DN_HEADS, DN_DK, DN_DV, DN_CONV, DN_CHUNK = 4, 128, 128, 4, 64
DN_CONV_CH = 1536
RW_HEADS, RW_N, RW_W = 8, 64, 512
RW_LORA_W, RW_LORA_A, RW_LORA_G = 96, 96, 256
RW_GN_EPS = 64e-5
RW_COLS = 1984
GLA_HEADS, GLA_DK, GLA_DV, GLA_LORA, GLA_TAU = 4, 64, 128, 16, 16.0
FOX_HEADS, FOX_HD = 4, 128
MEM_HEADS, MEM_HD, MEM_LEN = 4, 128, 256
N_GROUPS, EXPERTS_PER_GROUP, N_EXPERTS, TOP_K, D_EXPERT = 4, 8, 32, 2, 512

LANES = 128
SUBLANES = 8
VMEM_LIMIT = 56 * 1024 * 1024


def _cparams(*sem):
    return pltpu.CompilerParams(dimension_semantics=sem, vmem_limit_bytes=VMEM_LIMIT)


def _dot(a, b):
    return jnp.dot(a.astype(BF16), b.astype(BF16), preferred_element_type=F32)


def _dot_nt(a, b):
    return lax.dot_general(a.astype(BF16), b.astype(BF16), (((1,), (1,)), ((), ())), preferred_element_type=F32)


def _dot_tn(a, b):
    return lax.dot_general(a.astype(BF16), b.astype(BF16), (((0,), (0,)), ((), ())), preferred_element_type=F32)


def _split3(a):
    a1 = a.astype(BF16)
    r1 = a - a1.astype(F32)
    a2 = r1.astype(BF16)
    a3 = (r1 - a2.astype(F32)).astype(BF16)
    return a1, a2, a3


def _dot_hp_exact_rhs(a, b_bf16):
    a1, a2, a3 = _split3(a)
    d = lambda x: jnp.dot(x, b_bf16, preferred_element_type=F32)
    return (d(a3) + d(a2)) + d(a1)


def _dot_hp_exact_lhs(a_bf16, b):
    b1, b2, b3 = _split3(b)
    d = lambda y: jnp.dot(a_bf16, y, preferred_element_type=F32)
    return (d(b3) + d(b2)) + d(b1)


def _iota2(shape, axis):
    return lax.broadcasted_iota(jnp.int32, shape, axis)


def _sigmoid(x):
    return 1.0 / (1.0 + jnp.exp(-x))


def _silu(x):
    return x * _sigmoid(x)


def _softplus(x):
    return jnp.maximum(x, 0.0) + jnp.log(1.0 + jnp.exp(-jnp.abs(x)))


def _log_sigmoid(x):
    return -_softplus(-x)


def _rms_matmul_body(x_ref, g_ref, w_ref, o_ref, h_ref):
    @pl.when(pl.program_id(1) == 0)
    def _():
        x = x_ref[...]
        ms = jnp.mean(x * x, axis=-1, keepdims=True)
        h_ref[...] = (x * lax.rsqrt(ms + EPS) * g_ref[...]).astype(BF16)

    o_ref[...] = jnp.dot(h_ref[...], w_ref[...], preferred_element_type=F32).astype(o_ref.dtype)


def rms_matmul(x, g, w, *, tm, tn, row0=0, rows=None, out_dtype=F32, name="rms_matmul"):
    k = x.shape[1]
    m = x.shape[0] if rows is None else rows
    n = w.shape[1]
    assert m % tm == 0 and row0 % tm == 0 and n % tn == 0
    return pl.pallas_call(
        _rms_matmul_body,
        grid=(m // tm, n // tn),
        in_specs=[pl.BlockSpec((tm, k), lambda i, j: (row0 // tm + i, 0)),
                  pl.BlockSpec((1, k), lambda i, j: (0, 0)),
                  pl.BlockSpec((k, tn), lambda i, j: (0, j))],
        out_specs=pl.BlockSpec((tm, tn), lambda i, j: (i, j)),
        out_shape=jax.ShapeDtypeStruct((m, n), out_dtype),
        scratch_shapes=[pltpu.VMEM((tm, k), BF16)],
        compiler_params=_cparams("parallel", "arbitrary"),
        name=name,
    )(x, g.reshape(1, k), w)


def _matmul_res_body(a_ref, w_ref, r_ref, o_ref):
    o_ref[...] = r_ref[...] + jnp.dot(a_ref[...].astype(BF16), w_ref[...], preferred_element_type=F32)


def matmul_res(a, w, res, *, tm, tn, name="matmul_res"):
    m, k = a.shape
    n = w.shape[1]
    assert m % tm == 0 and n % tn == 0
    return pl.pallas_call(
        _matmul_res_body,
        grid=(m // tm, n // tn),
        in_specs=[pl.BlockSpec((tm, k), lambda i, j: (i, 0)),
                  pl.BlockSpec((k, tn), lambda i, j: (0, j)),
                  pl.BlockSpec((tm, tn), lambda i, j: (i, j))],
        out_specs=pl.BlockSpec((tm, tn), lambda i, j: (i, j)),
        out_shape=jax.ShapeDtypeStruct((m, n), F32),
        compiler_params=_cparams("parallel", "parallel"),
        name=name,
    )(a, w, res)


def _merge_body(x_ref, g_ref, *refs, first_tiles):
    o_first, o_rest = refs[0:N_BRANCH], refs[N_BRANCH:2 * N_BRANCH]
    wgs = refs[2 * N_BRANCH:3 * N_BRANCH]
    bg_ref, wb_ref, wo_ref, y_ref, h_ref, ob_ref = refs[3 * N_BRANCH:]
    i = pl.program_id(0)

    @pl.when(pl.program_id(1) == 0)
    def _():
        x = x_ref[...]
        ms = jnp.mean(x * x, axis=-1, keepdims=True)
        h_ref[...] = (x * lax.rsqrt(ms + EPS) * g_ref[...]).astype(BF16)
        y_ref[...] = x

        @pl.when(i < first_tiles)
        def _():
            for n in range(N_BRANCH):
                ob_ref[n] = o_first[n][...].astype(BF16)

        @pl.when(i >= first_tiles)
        def _():
            for n in range(N_BRANCH):
                ob_ref[n] = o_rest[n][...].astype(BF16)

    h = h_ref[...]
    merged = None
    for n in range(N_BRANCH):
        gate = _sigmoid(jnp.dot(h, wgs[n][...], preferred_element_type=F32) + bg_ref[n:n + 1, :])
        br = jnp.dot(ob_ref[n], wb_ref[n], preferred_element_type=F32)
        merged = gate * br if merged is None else merged + gate * br
    y_ref[...] += jnp.dot(merged.astype(BF16), wo_ref[...], preferred_element_type=F32)


def merge_out(x, g, outs_first, outs_rest, w_gate, b_gate, w_branch, w_out, *, tm, tn):
    m, d = x.shape
    nj = d // tn
    first_tiles = outs_first[0].shape[0] // tm
    assert outs_first[0].shape[0] % tm == 0 and outs_rest[0].shape[0] == m - first_tiles * tm
    wg_specs = [pl.BlockSpec((d, tn), functools.partial(lambda i, j, n: (0, n * nj + j), n=n))
                for n in range(N_BRANCH)]
    return pl.pallas_call(
        functools.partial(_merge_body, first_tiles=first_tiles),
        grid=(m // tm, nj),
        in_specs=[pl.BlockSpec((tm, d), lambda i, j: (i, 0)),
                  pl.BlockSpec((1, d), lambda i, j: (0, 0))]
                 + [pl.BlockSpec((tm, BRANCH_W), lambda i, j: (jnp.minimum(i, first_tiles - 1), 0))] * N_BRANCH
                 + [pl.BlockSpec((tm, BRANCH_W), lambda i, j: (jnp.maximum(i - first_tiles, 0), 0))] * N_BRANCH
                 + wg_specs
                 + [pl.BlockSpec((N_BRANCH, tn), lambda i, j: (0, j)),
                    pl.BlockSpec((N_BRANCH, BRANCH_W, tn), lambda i, j: (0, 0, j)),
                    pl.BlockSpec((tn, d), lambda i, j: (j, 0))],
        out_specs=pl.BlockSpec((tm, d), lambda i, j: (i, 0)),
        out_shape=jax.ShapeDtypeStruct((m, d), F32),
        scratch_shapes=[pltpu.VMEM((tm, d), BF16), pltpu.VMEM((N_BRANCH, tm, BRANCH_W), BF16)],
        compiler_params=_cparams("parallel", "arbitrary"),
        name="merge_out",
    )(x, g.reshape(1, d), *outs_first, *outs_rest, w_gate, w_gate, w_gate, w_gate, b_gate.reshape(N_BRANCH, d),
      w_branch, w_out)


DN_Z0 = DN_CONV_CH
DN_AB0 = DN_CONV_CH + BRANCH_W
DN_SLAB = DN_AB0 + 256
INV_BLOCK = 16
DN_STACK = 256


def _inv_unit_lower(nmat, size, c):
    row = _iota2((size, size), 0)
    col = _iota2((size, size), 1)
    eye = (row == col).astype(F32)
    blk = min(INV_BLOCK, c)
    if c > blk:
        same = (row // blk) == (col // blk)
        d = jnp.where(same, nmat, 0.0)
        r = jnp.where(same, 0.0, nmat)
    else:
        d, r = nmat, None
    t = eye - d
    p = d
    k = 2
    while k < blk:
        p = _dot(p, p)
        t = _dot(t, eye + p)
        k *= 2
    if r is None:
        return t
    pm = _dot(t, r)
    t2 = eye - pm
    q = pm
    k = 2
    while k < c // blk:
        q = _dot(q, q)
        t2 = _dot(t2, eye + q)
        k *= 2
    return _dot(t2, t)


def _dn_body(*refs, nb, c, valid):
    p_refs = refs[0:nb]
    cw_ref, alog_ref, dtb_ref, nw_ref, conv0_ref, s0_ref, o_ref, s1_ref, cbuf, s_scr = refs[nb:]
    ci = pl.program_id(1)
    nch = nb * DN_HEADS
    size = nch * c

    @pl.when(ci == 0)
    def _():
        s_scr[...] = s0_ref[...].reshape(s_scr.shape)
        cbuf[:, 0:SUBLANES, :] = conv0_ref[...]

    vcol = _iota2((c, 1), 0) < valid
    ltri = (_iota2((c, c), 0) >= _iota2((c, c), 1)).astype(BF16)
    q_l, k_l, v_l, z_l, gc_l, beta_l = [], [], [], [], [], []
    for sq in range(nb):
        p_ref, cb = p_refs[sq], cbuf.at[sq]
        cb[SUBLANES:SUBLANES + c, :] = p_ref[:, 0:DN_CONV_CH]
        conv = cw_ref[0:1, :] * cb[5:5 + c, :]
        for j in range(1, DN_CONV):
            conv = conv + cw_ref[j:j + 1, :] * cb[5 + j:5 + j + c, :]
        cb[0:SUBLANES, :] = cb[c:c + SUBLANES, :]
        act = _silu(conv)
        ab = p_ref[:, DN_AB0:DN_AB0 + LANES]
        g_all = jnp.where(vcol, -jnp.exp(alog_ref[...]) * _softplus(ab + dtb_ref[...]), 0.0)
        beta_all = jnp.where(vcol, _sigmoid(ab), 0.0)
        gc_all = _dot_hp_exact_lhs(ltri, g_all)
        for h in range(DN_HEADS):
            q = act[:, h * DN_DK:(h + 1) * DN_DK]
            k = act[:, 512 + h * DN_DK:512 + (h + 1) * DN_DK]
            q_l.append(q * lax.rsqrt(jnp.sum(q * q, axis=-1, keepdims=True) + EPS) * (DN_DK ** -0.5))
            k_l.append(jnp.where(vcol, k * lax.rsqrt(jnp.sum(k * k, axis=-1, keepdims=True) + EPS), 0.0))
            v_l.append(jnp.where(vcol, act[:, 1024 + h * DN_DV:1024 + (h + 1) * DN_DV], 0.0))
            z_l.append(p_ref[:, DN_Z0 + h * DN_DV:DN_Z0 + (h + 1) * DN_DV])
            gc_l.append(gc_all[:, h:h + 1])
            beta_l.append(beta_all[:, DN_HEADS + h:DN_HEADS + h + 1])
    stack = lambda xs: jnp.concatenate(xs, axis=0)
    q, k, v, z, gc, beta = (stack(x) for x in (q_l, k_l, v_l, z_l, gc_l, beta_l))

    row = _iota2((size, size), 0)
    col = _iota2((size, size), 1)
    same = (row // c) == (col // c)
    incl = same & (row >= col)
    strict = same & (row > col)
    gcr = _dot_hp_exact_lhs(jnp.ones((SUBLANES, size), BF16), jnp.where(row == col, gc, 0.0))[0:1, :]
    gamma = jnp.exp(jnp.where(incl, gc - gcr, -jnp.inf))
    tmat = _inv_unit_lower(jnp.where(strict, beta * _dot_nt(k, k) * gamma, 0.0), size, c)
    egc = jnp.exp(gc)
    u = _dot(tmat, beta * v)
    w = _dot(tmat, (beta * egc) * k)
    qk = jnp.where(incl, _dot_nt(q, k) * gamma, 0.0)
    qg = q * egc

    v_new, o_inter = [], []
    for i in range(nch):
        rs = slice(i * c, (i + 1) * c)
        s = s_scr[i]
        gl = gc[(i + 1) * c - 1:(i + 1) * c, :]
        vn = u[rs] - _dot(w[rs], s)
        o_inter.append(_dot(qg[rs], s))
        s_scr[i] = jnp.exp(gl) * s + _dot_tn(k[rs] * jnp.exp(gl - gc[rs]), vn)
        v_new.append(vn)
    o = stack(o_inter) + _dot(qk, stack(v_new))
    on = o * lax.rsqrt(jnp.mean(o * o, axis=-1, keepdims=True) + EPS) * nw_ref[...] * _silu(z)
    for i in range(nch):
        sq, h = divmod(i, DN_HEADS)
        o_ref[sq, :, h * DN_DV:(h + 1) * DN_DV] = on[i * c:(i + 1) * c]

    @pl.when(ci == pl.num_programs(1) - 1)
    def _():
        s1_ref[...] = s_scr[...].reshape(s1_ref.shape)


def deltanet(p_dn, conv_w, a_log, dt_bias, norm_w, conv0, s0, *, layer, nseq, seq_len, c, valid, nb):
    m = nseq * seq_len
    n = seq_len // c
    pad = jnp.zeros((LANES - DN_HEADS,), F32)
    alog = jnp.concatenate([a_log, pad]).reshape(1, LANES)
    dtb = jnp.concatenate([dt_bias, pad]).reshape(1, LANES)
    full = lambda shape: pl.BlockSpec(shape, lambda b, i: (0,) * len(shape))
    seq_rows = lambda sq: functools.partial(lambda b, i, sq: ((b * nb + sq) * n + i, 0), sq=sq)
    o, s1 = pl.pallas_call(
        functools.partial(_dn_body, nb=nb, c=c, valid=valid),
        grid=(nseq // nb, n),
        in_specs=[pl.BlockSpec((c, DN_SLAB), seq_rows(sq)) for sq in range(nb)]
                 + [full((DN_CONV, DN_CONV_CH)), full((1, LANES)), full((1, LANES)), full((1, DN_DV)),
                    pl.BlockSpec((nb, SUBLANES, DN_CONV_CH), lambda b, i: (b, 0, 0)),
                    pl.BlockSpec((None, nb, DN_HEADS, DN_DK, DN_DV), lambda b, i: (layer, b, 0, 0, 0))],
        out_specs=[pl.BlockSpec((nb, c, BRANCH_W), lambda b, i: (b, i, 0)),
                   pl.BlockSpec((nb, DN_HEADS, DN_DK, DN_DV), lambda b, i: (b, 0, 0, 0))],
        out_shape=[jax.ShapeDtypeStruct((nseq, seq_len, BRANCH_W), F32),
                   jax.ShapeDtypeStruct((nseq, DN_HEADS, DN_DK, DN_DV), F32)],
        scratch_shapes=[pltpu.VMEM((nb, c + 2 * SUBLANES, DN_CONV_CH), F32),
                        pltpu.VMEM((nb * DN_HEADS, DN_DK, DN_DV), F32)],
        compiler_params=_cparams("parallel", "arbitrary"),
        name="deltanet",
    )(*([p_dn] * nb), conv_w, alog, dtb, norm_w.reshape(1, DN_DV), conv0, s0)
    return o.reshape(m, BRANCH_W), s1


GLA_K0, GLA_V0, GLA_G0, GLA_GD0 = 256, 512, 1024, 1536
GLA_SLAB = 1792
GLA_CHUNK = 64
GLA_STACK = 256


def _gla_body(q_ref, k_ref, v_ref, g_ref, gd_ref, aup_ref, ab_ref, nw_ref, s0_ref,
              o_ref, s1_ref, st_scr, *, nb, c, valid):
    ci = pl.program_id(2)
    size = nb * c

    @pl.when(ci == 0)
    def _():
        st_scr[...] = s0_ref[...].reshape(st_scr.shape)

    row = _iota2((size, size), 0)
    col = _iota2((size, size), 1)
    incl = ((row // c) == (col // c)) & (row >= col)
    vcol = (_iota2((size, 1), 0) % c) < valid
    lane = _iota2((1, LANES), 1)
    rows2 = lambda ref, ls: ref[:, :, ls].reshape(size, ls.stop - ls.start)
    full = slice(0, LANES)

    gk = _log_sigmoid(_dot(rows2(gd_ref, full), aup_ref[...]) + ab_ref[...]) * (1.0 / GLA_TAU)
    gk = jnp.where(vcol, gk, 0.0)
    b = _dot_hp_exact_lhs(incl.astype(BF16), gk)
    q = rows2(q_ref, full) * (GLA_DK ** -0.5)
    k = jnp.where(vcol, rows2(k_ref, full), 0.0)
    qt = q * jnp.exp(b)
    kt = k * jnp.exp(-b)
    for hh in range(2):
        lm = (lane >= hh * GLA_DK) & (lane < (hh + 1) * GLA_DK)
        vs = slice(hh * GLA_DV, (hh + 1) * GLA_DV)
        qth = jnp.where(lm, qt, 0.0)
        v = jnp.where(vcol, rows2(v_ref, vs), 0.0)
        att = jnp.where(incl, _dot_nt(qth, kt), 0.0)
        o_inter = []
        for i in range(nb):
            rs = slice(i * c, (i + 1) * c)
            b_last = b[(i + 1) * c - 1:(i + 1) * c, :]
            st = st_scr[2 * i + hh]
            o_inter.append(_dot_nt(qth[rs], st))
            kd = jnp.where(lm, k[rs] * jnp.exp(b_last - b[rs]), 0.0)
            st_scr[2 * i + hh] = st * jnp.exp(b_last) + _dot_tn(v[rs], kd)
        o = _dot(att, v) + jnp.concatenate(o_inter, axis=0)
        on = o * lax.rsqrt(jnp.mean(o * o, axis=-1, keepdims=True) + EPS) * nw_ref[...] * _silu(rows2(g_ref, vs))
        o_ref[:, :, vs] = on.reshape(nb, c, GLA_DV)

    @pl.when(ci == pl.num_programs(2) - 1)
    def _():
        s1_ref[...] = st_scr[...].reshape(s1_ref.shape)


def gla(p_gla, a_up, a_b, norm_w, s0t, *, c, valid, nb):
    nseq, seq_len, _ = p_gla.shape
    n = seq_len // c
    aup = jnp.zeros((LANES, GLA_HEADS * GLA_DK), F32).at[:GLA_LORA].set(a_up).astype(BF16)
    cols = lambda w, off: pl.BlockSpec((nb, c, w), lambda b, p, i: (b, i, off // w + p))
    o, s1 = pl.pallas_call(
        functools.partial(_gla_body, nb=nb, c=c, valid=valid),
        grid=(nseq // nb, 2, n),
        in_specs=[cols(LANES, 0), cols(LANES, GLA_K0), cols(2 * GLA_DV, GLA_V0), cols(2 * GLA_DV, GLA_G0),
                  pl.BlockSpec((nb, c, LANES), lambda b, p, i: (b, i, GLA_GD0 // LANES)),
                  pl.BlockSpec((LANES, LANES), lambda b, p, i: (0, p)),
                  pl.BlockSpec((1, LANES), lambda b, p, i: (0, p)),
                  pl.BlockSpec((1, GLA_DV), lambda b, p, i: (0, 0)),
                  pl.BlockSpec((nb, 2, GLA_DV, LANES), lambda b, p, i: (b, p, 0, 0))],
        out_specs=[pl.BlockSpec((nb, c, 2 * GLA_DV), lambda b, p, i: (b, i, p)),
                   pl.BlockSpec((nb, 2, GLA_DV, LANES), lambda b, p, i: (b, p, 0, 0))],
        out_shape=[jax.ShapeDtypeStruct((nseq, seq_len, BRANCH_W), F32),
                   jax.ShapeDtypeStruct((nseq, GLA_HEADS, GLA_DV, LANES), F32)],
        scratch_shapes=[pltpu.VMEM((nb * 2, GLA_DV, LANES), F32)],
        compiler_params=_cparams("parallel", "parallel", "arbitrary"),
        name="gla",
    )(p_gla, p_gla, p_gla, p_gla, p_gla, aup, a_b.reshape(1, -1), norm_w.reshape(1, GLA_DV), s0t)
    return o.reshape(nseq * seq_len, BRANCH_W), s1


RW_WD0, RW_AD0, RW_GD0 = 1536, 1664, 1792
RW_SLAB = 2048
RW_QW = 256


def _seg_ones(n, seg):
    r = _iota2((n, n), 0) // seg
    c = _iota2((n, n), 1) // seg
    return (r == c).astype(BF16)


def _rw_prep_body(p_ref, prev_ref, mu_ref, w0_ref, a0_ref, kk_ref, ka_ref, rk_ref, wup_ref, aup_ref, gup_ref,
                  r_o, w_o, k_o, v_o, nkk_o, kka_o, gate_o, bonus_o, *scratch, tiles_per_seq):
    p = p_ref[...]
    if tiles_per_seq is None:
        prev = prev_ref[...]
    else:
        (buf,) = scratch
        tm = p.shape[0]

        @pl.when(pl.program_id(0) % tiles_per_seq == 0)
        def _():
            buf[0:SUBLANES, :] = prev_ref[...]

        buf[SUBLANES:SUBLANES + tm, :] = p
        prev = buf[SUBLANES - 1:SUBLANES - 1 + tm, :]
        buf[0:SUBLANES, :] = buf[tm:tm + SUBLANES, :]
    xr = p + mu_ref[...] * (prev - p)
    r = xr[:, 0:RW_W]
    k = xr[:, RW_W:2 * RW_W]
    v = xr[:, 2 * RW_W:3 * RW_W]
    wd = xr[:, RW_WD0:RW_WD0 + LANES]
    ad = xr[:, RW_AD0:RW_AD0 + LANES]
    gd = xr[:, RW_GD0:RW_GD0 + RW_LORA_G]
    logw = -jnp.exp(-_softplus(-(w0_ref[...] + _dot(jnp.tanh(wd), wup_ref[...]))) - 0.5)
    a = _sigmoid(a0_ref[...] + _dot(ad, aup_ref[...]))
    seg = _seg_ones(RW_W, RW_N)
    kr = k * kk_ref[...]
    kk = kr * lax.rsqrt(_dot_hp_exact_rhs(kr * kr, seg) + EPS)
    k2 = k * (1.0 + (a - 1.0) * ka_ref[...])
    r_o[...] = r
    w_o[...] = jnp.exp(logw)
    k_o[...] = k2
    v_o[...] = v
    nkk_o[...] = -kk
    kka_o[...] = kk * a
    gate_o[...] = _dot(_sigmoid(gd), gup_ref[...])
    bonus_o[...] = _dot_hp_exact_rhs(r * k2 * rk_ref[...], seg) * v


def rwkv_prep(p_rw, prev, mu, w0, a0, k_k, k_a, r_k, w_up, a_up, g_up, *, tm, seq_len=None):
    m = p_rw.shape[0]
    row = lambda w: pl.BlockSpec((tm, w), lambda i: (i, 0))
    full = lambda a: pl.BlockSpec(a.shape, lambda i: (0,) * a.ndim)
    args = (mu.reshape(1, -1), w0.reshape(1, -1), a0.reshape(1, -1), k_k.reshape(1, -1), k_a.reshape(1, -1),
            r_k.reshape(1, -1), w_up, a_up, g_up)
    in_kernel = seq_len is not None
    return pl.pallas_call(
        functools.partial(_rw_prep_body, tiles_per_seq=seq_len // tm if in_kernel else None),
        grid=(m // tm,),
        in_specs=[row(RW_SLAB), full(prev) if in_kernel else row(RW_SLAB)] + [full(a) for a in args],
        out_specs=[row(RW_W)] * 8,
        out_shape=[jax.ShapeDtypeStruct((m, RW_W), F32)] * 8,
        scratch_shapes=[pltpu.VMEM((tm + SUBLANES, RW_SLAB), F32)] if in_kernel else [],
        compiler_params=_cparams("arbitrary" if in_kernel else "parallel"),
        name="rwkv_prep",
    )(p_rw, prev, *args)


def _rw_scan_body(r_ref, w_ref, k_ref, v_ref, nkk_ref, kka_ref, s0_ref, y_ref, s1_ref, s_scr, *, nb, t_blk):
    ti = pl.program_id(1)

    @pl.when(ti == 0)
    def _():
        s_scr[...] = s0_ref[...].reshape(s_scr.shape)

    nq = RW_W // RW_QW
    tiles = [(b, q) for b in range(nb) for q in range(nq)]
    ones = _seg_ones(RW_QW, RW_N)
    diag = _iota2((RW_N, RW_QW), 0) == (_iota2((RW_N, RW_QW), 1) % RW_N)
    diag_b = diag.astype(BF16)
    tg = SUBLANES if t_blk % SUBLANES == 0 else t_blk
    dot = lambda a: jnp.dot(a, ones, preferred_element_type=F32)
    stack = lambda xs: jnp.concatenate(xs, axis=0)
    piece = lambda a, i: a[i * RW_N:(i + 1) * RW_N]

    def group(g, carry):
        base = pl.multiple_of(g * tg, tg)
        ld = lambda ref: [[ref[b, pl.ds(base, tg), q * RW_QW:(q + 1) * RW_QW] for q in range(nq)] for b in range(nb)]
        rb, wb, kb, vb, nkkb, kkab = (ld(ref) for ref in (r_ref, w_ref, k_ref, v_ref, nkk_ref, kka_ref))
        yrows = [[] for _ in tiles]
        for j in range(tg):
            row = lambda blk, b, q: blk[b][q][j:j + 1, :]
            s = [s_scr[i] for i in range(len(tiles))]
            sa = dot(stack([s[i] * row(nkkb, b, q) for i, (b, q) in enumerate(tiles)]).astype(BF16))
            v1 = [row(vb, b, q).astype(BF16) for b, q in tiles]
            vcol = dot(stack([diag_b * x for x in v1]))
            s = [s[i] * row(wb, b, q) + piece(sa, i) * row(kkab, b, q) + piece(vcol, i) * row(kb, b, q)
                 for i, (b, q) in enumerate(tiles)]
            for i in range(len(tiles)):
                s_scr[i] = s[i]
            yb = dot(stack([s[i] * row(rb, b, q) for i, (b, q) in enumerate(tiles)]).astype(BF16))
            for i in range(len(tiles)):
                yrows[i].append(jnp.sum(jnp.where(diag, piece(yb, i), 0.0), axis=0, keepdims=True))
        for i, (b, q) in enumerate(tiles):
            y_ref[b, pl.ds(base, tg), q * RW_QW:(q + 1) * RW_QW] = stack(yrows[i])
        return carry

    lax.fori_loop(0, t_blk // tg, group, 0)

    @pl.when(ti == pl.num_programs(1) - 1)
    def _():
        s1_ref[...] = s_scr[...].reshape(s1_ref.shape)


def rwkv_scan(r, w, k, v, nkk, kka, s0q, *, nb, t_blk):
    nseq, L, _ = r.shape
    nq = RW_W // RW_QW
    tok = pl.BlockSpec((nb, t_blk, RW_W), lambda b, t: (b, t, 0))
    st = pl.BlockSpec((nb, nq, RW_N, RW_QW), lambda b, t: (b, 0, 0, 0))
    return pl.pallas_call(
        functools.partial(_rw_scan_body, nb=nb, t_blk=t_blk),
        grid=(nseq // nb, L // t_blk),
        in_specs=[tok] * 6 + [st],
        out_specs=[tok, st],
        out_shape=[jax.ShapeDtypeStruct((nseq, L, RW_W), F32),
                   jax.ShapeDtypeStruct((nseq, nq, RW_N, RW_QW), F32)],
        scratch_shapes=[pltpu.VMEM((nb * nq, RW_N, RW_QW), F32)],
        compiler_params=_cparams("parallel", "arbitrary"),
        name="rwkv_scan",
    )(r, w, k, v, nkk, kka, s0q)


def rw_state_to_tiles(s):
    n = s.shape[0]
    nq = RW_W // RW_QW
    hq = RW_QW // RW_N
    return s.reshape(n, nq, hq, RW_N, RW_N).transpose(0, 1, 3, 2, 4).reshape(n, nq, RW_N, RW_QW)


def rw_tiles_to_state(t):
    n = t.shape[0]
    nq = RW_W // RW_QW
    hq = RW_QW // RW_N
    return t.reshape(n, nq, RW_N, hq, RW_N).transpose(0, 1, 3, 2, 4).reshape(n, RW_HEADS, RW_N, RW_N)


def _rw_post_body(y_ref, bonus_ref, gate_ref, lw_ref, lb_ref, o_ref):
    y = y_ref[...]
    seg = _seg_ones(RW_W, RW_N)
    mu = _dot_hp_exact_rhs(y, seg) * (1.0 / RW_N)
    d = y - mu
    var = _dot_hp_exact_rhs(d * d, seg) * (1.0 / RW_N)
    yn = d * lax.rsqrt(var + RW_GN_EPS) * lw_ref[...] + lb_ref[...]
    o_ref[...] = ((yn + bonus_ref[...]) * gate_ref[...]).astype(o_ref.dtype)


def rwkv_post(y, bonus, gate, ln_w, ln_b, *, tm):
    m = y.shape[0]
    row = pl.BlockSpec((tm, RW_W), lambda i: (i, 0))
    par = pl.BlockSpec((1, RW_W), lambda i: (0, 0))
    return pl.pallas_call(
        _rw_post_body,
        grid=(m // tm,),
        in_specs=[row, row, row, par, par],
        out_specs=row,
        out_shape=jax.ShapeDtypeStruct((m, RW_W), F32),
        compiler_params=_cparams("parallel"),
        name="rwkv_post",
    )(y, bonus, gate, ln_w.reshape(1, -1), ln_b.reshape(1, -1))


FOX_K0, FOX_V0, FOX_F0 = 512, 1024, 1536
FOX_SLAB = 1792
FOX_TQ = 512
NEG = -1e30


def _fox_prep_body(p_ref, ft_ref, qn_ref, kn_ref, fb_ref, fbt_ref,
                   q_o, k_o, kb_o, vb_o, lf_o, c_o, ct_o, carry, carry_t, *, tm):
    @pl.when(pl.program_id(1) == 0)
    def _():
        carry[...] = jnp.zeros_like(carry)
        carry_t[...] = jnp.zeros_like(carry_t)

    for h in range(FOX_HEADS):
        ls = slice(h * FOX_HD, (h + 1) * FOX_HD)
        q = p_ref[:, ls]
        k = p_ref[:, FOX_K0 + h * FOX_HD:FOX_K0 + (h + 1) * FOX_HD]
        qn = q * lax.rsqrt(jnp.mean(q * q, axis=-1, keepdims=True) + EPS) * qn_ref[...]
        kn = k * lax.rsqrt(jnp.mean(k * k, axis=-1, keepdims=True) + EPS) * kn_ref[...]
        q_o[:, ls] = (qn * (FOX_HD ** -0.5)).astype(BF16)
        k_o[:, ls] = kn
        kb_o[:, ls] = kn.astype(BF16)
    vb_o[...] = p_ref[:, FOX_V0:FOX_V0 + BRANCH_W].astype(BF16)
    lf = _log_sigmoid(p_ref[:, FOX_F0:FOX_F0 + LANES] + fb_ref[...])
    lf_o[...] = lf
    row = _iota2((tm, tm), 0)
    col = _iota2((tm, tm), 1)
    c = _dot_hp_exact_lhs((row >= col).astype(BF16), lf) + carry[...]
    c_o[...] = c
    carry[...] = c[tm - 1:tm, :]
    lft = _log_sigmoid(ft_ref[0] + fbt_ref[...])
    ct = _dot_hp_exact_rhs(lft, (row <= col).astype(BF16)) + carry_t[:, 0:1]
    ct_o[0, 0] = ct
    carry_t[...] = jnp.broadcast_to(ct[:, tm - 1:tm], carry_t.shape)


def fox_prep(p_fox, ft, qn_w, kn_w, fb, *, nseq, seq_len, tm):
    n = seq_len // tm
    m = nseq * seq_len
    fbp = jnp.concatenate([fb, jnp.zeros((LANES - FOX_HEADS,), F32)]).reshape(1, LANES)
    fbt = jnp.broadcast_to(jnp.concatenate([fb, jnp.zeros((4,), F32)])[:, None], (SUBLANES, tm))
    row = lambda w, dt: (pl.BlockSpec((tm, w), lambda b, i: (b * n + i, 0)), jax.ShapeDtypeStruct((m, w), dt))
    outs = [row(BRANCH_W, BF16), row(BRANCH_W, F32), row(BRANCH_W, BF16), row(BRANCH_W, BF16),
            row(LANES, F32), row(LANES, F32),
            (pl.BlockSpec((1, 1, SUBLANES, tm), lambda b, i: (b, i, 0, 0)),
             jax.ShapeDtypeStruct((nseq, n, SUBLANES, tm), F32))]
    return pl.pallas_call(
        functools.partial(_fox_prep_body, tm=tm),
        grid=(nseq, n),
        in_specs=[pl.BlockSpec((tm, FOX_SLAB), lambda b, i: (b * n + i, 0)),
                  pl.BlockSpec((1, SUBLANES, tm), lambda b, i: (b, 0, i)),
                  pl.BlockSpec((1, FOX_HD), lambda b, i: (0, 0)),
                  pl.BlockSpec((1, FOX_HD), lambda b, i: (0, 0)),
                  pl.BlockSpec((1, LANES), lambda b, i: (0, 0)),
                  pl.BlockSpec((SUBLANES, tm), lambda b, i: (0, 0))],
        out_specs=[o[0] for o in outs],
        out_shape=[o[1] for o in outs],
        scratch_shapes=[pltpu.VMEM((1, LANES), F32), pltpu.VMEM((SUBLANES, LANES), F32)],
        compiler_params=_cparams("parallel", "arbitrary"),
        name="fox_prep",
    )(p_fox, ft, qn_w.reshape(1, FOX_HD), kn_w.reshape(1, FOX_HD), fbp, fbt)


def _fox_prompt_body(q_ref, k_ref, v_ref, c_ref, ct_ref, o_ref, *, tq):
    h = pl.program_id(1)
    qi = pl.program_id(2)
    q = q_ref[...]
    lane = _iota2((1, LANES), 1)
    cq = jnp.sum(jnp.where(lane == h, c_ref[...], 0.0), axis=-1, keepdims=True)
    rowh = _iota2((SUBLANES, 1), 0) == h
    qpos = qi * tq + _iota2((tq, tq), 0)

    def kv_step(j, carry):
        m, l, acc = carry
        k = k_ref[pl.ds(pl.multiple_of(j * tq, tq), tq), :]
        v = v_ref[pl.ds(pl.multiple_of(j * tq, tq), tq), :]
        ck = jnp.sum(jnp.where(rowh, ct_ref[0, j], 0.0), axis=0, keepdims=True)
        s = _dot_nt(q, k) + (cq - ck)
        s = jnp.where(qpos >= j * tq + _iota2((tq, tq), 1), s, NEG)
        m_new = jnp.maximum(m, jnp.max(s, axis=-1, keepdims=True))
        p = jnp.exp(s - m_new)
        alpha = jnp.exp(m - m_new)
        return m_new, alpha * l + jnp.sum(p, axis=-1, keepdims=True), alpha * acc + _dot(p, v)

    init = (jnp.full((tq, 1), NEG, F32), jnp.zeros((tq, 1), F32), jnp.zeros((tq, FOX_HD), F32))
    m, l, acc = lax.fori_loop(0, qi + 1, kv_step, init)
    o_ref[...] = (acc / l).astype(o_ref.dtype)


def fox_prompt(qb, kb, vb, c, ct, *, nseq, L, tq):
    nq = L // tq
    return pl.pallas_call(
        functools.partial(_fox_prompt_body, tq=tq),
        grid=(nseq, FOX_HEADS, nq),
        in_specs=[pl.BlockSpec((tq, FOX_HD), lambda b, h, i: (b * nq + i, h)),
                  pl.BlockSpec((L, FOX_HD), lambda b, h, i: (b, h)),
                  pl.BlockSpec((L, FOX_HD), lambda b, h, i: (b, h)),
                  pl.BlockSpec((tq, LANES), lambda b, h, i: (b * nq + i, 0)),
                  pl.BlockSpec((1, nq, SUBLANES, tq), lambda b, h, i: (b, 0, 0, 0))],
        out_specs=pl.BlockSpec((tq, FOX_HD), lambda b, h, i: (b * nq + i, h)),
        out_shape=jax.ShapeDtypeStruct((nseq * L, BRANCH_W), F32),
        compiler_params=_cparams("parallel", "parallel", "arbitrary"),
        name="fox_prompt",
    )(qb, kb, vb, c, ct)


PAGE_ROWS = PAGE_SIZE * FOX_HEADS
NEW_ROWS = 32
LF_ROWS = 24


def _fox_sample_body(pt_ref, q_ref, *refs, n_pages, n_new):
    kc = refs[0:n_pages]
    vc = refs[n_pages:2 * n_pages]
    lfc = refs[2 * n_pages:3 * n_pages]
    kn_ref, vn_ref, lfn_ref, o_ref, s_scr = refs[3 * n_pages:]
    nq = q_ref.shape[1]
    nr = FOX_HEADS * nq
    q_all = jnp.concatenate([q_ref[0, :, h * FOX_HD:(h + 1) * FOX_HD] for h in range(FOX_HEADS)], axis=0)

    r = _iota2((PAGE_ROWS, PAGE_ROWS), 0)
    c = _iota2((PAGE_ROWS, PAGE_ROWS), 1)
    same_head = (r % FOX_HEADS) == (c % FOX_HEADS)
    cum_mat = (same_head & (r // FOX_HEADS <= c // FOX_HEADS)).astype(BF16)
    tot_mat = same_head.astype(BF16)
    lf_rows = jnp.concatenate([lfc[i][...] for i in range(n_pages)] + [lfn_ref[0]]
                              + [jnp.zeros((LF_ROWS - n_pages - 1, PAGE_ROWS), F32)], axis=0)
    earlier = (_iota2((LF_ROWS, LF_ROWS), 0) > _iota2((LF_ROWS, LF_ROWS), 1)).astype(BF16)
    bias = (_dot_hp_exact_rhs(lf_rows, cum_mat)
            + _dot_hp_exact_lhs(earlier, _dot_hp_exact_rhs(lf_rows, tot_mat)))

    head_ok = (_iota2((nr, PAGE_ROWS), 0) // nq) == (_iota2((nr, PAGE_ROWS), 1) % FOX_HEADS)
    m = jnp.full((nr, PAGE_ROWS), NEG, F32)
    for i in range(n_pages):
        s = jnp.where(head_ok, _dot_nt(q_all, kc[i][...]) - bias[i:i + 1, :], NEG)
        s_scr[i] = s
        m = jnp.maximum(m, s)
    kr = _iota2((nr, NEW_ROWS), 1)
    qr = _iota2((nr, NEW_ROWS), 0)
    new_ok = ((qr // nq) == (kr % FOX_HEADS)) & (kr // FOX_HEADS <= qr % nq) & (kr < n_new * FOX_HEADS)
    s_new = jnp.where(new_ok, _dot_nt(q_all, kn_ref[0]) - bias[n_pages:n_pages + 1, 0:NEW_ROWS], NEG)
    m = jnp.maximum(jnp.max(m, axis=-1, keepdims=True), jnp.max(s_new, axis=-1, keepdims=True))

    p_new = jnp.exp(s_new - m)
    l = jnp.sum(p_new, axis=-1, keepdims=True)
    acc = _dot(p_new, vn_ref[0])
    psum = jnp.zeros((nr, PAGE_ROWS), F32)
    for i in range(n_pages):
        p = jnp.exp(s_scr[i] - m)
        psum = psum + p
        acc = acc + _dot(p, vc[i][...])
    out = acc / (l + jnp.sum(psum, axis=-1, keepdims=True))
    for h in range(FOX_HEADS):
        o_ref[0, :, h * FOX_HD:(h + 1) * FOX_HD] = out[h * nq:(h + 1) * nq]


def fox_sample(page_table, qb, cache_k, cache_v, cache_lf, kn, vn, lfn, *, layer, n_new):
    nb, nq, _ = qb.shape
    n_pages = page_table.shape[1]
    page = lambda shape, i: pl.BlockSpec(
        (None, None) + shape, functools.partial(lambda b, pt, i: (layer, pt[b * n_pages + i], 0, 0), i=i))
    per_b = lambda shape: pl.BlockSpec((1,) + shape, lambda b, pt: (b, 0, 0))
    in_specs = ([per_b((nq, BRANCH_W))]
                + [page((PAGE_ROWS, FOX_HD), i) for i in range(n_pages)]
                + [page((PAGE_ROWS, FOX_HD), i) for i in range(n_pages)]
                + [page((1, PAGE_ROWS), i) for i in range(n_pages)]
                + [per_b((NEW_ROWS, FOX_HD)), per_b((NEW_ROWS, FOX_HD)), per_b((1, PAGE_ROWS))])
    grid_spec = pltpu.PrefetchScalarGridSpec(
        num_scalar_prefetch=1, grid=(nb,), in_specs=in_specs, out_specs=per_b((nq, BRANCH_W)),
        scratch_shapes=[pltpu.VMEM((n_pages, FOX_HEADS * nq, PAGE_ROWS), F32)])
    return pl.pallas_call(
        functools.partial(_fox_sample_body, n_pages=n_pages, n_new=n_new),
        grid_spec=grid_spec,
        out_shape=jax.ShapeDtypeStruct((nb, nq, BRANCH_W), F32),
        compiler_params=_cparams("parallel"),
        name="fox_sample",
    )(page_table.reshape(-1), qb, *([cache_k] * n_pages), *([cache_v] * n_pages), *([cache_lf] * n_pages),
      kn, vn, lfn)


def _mem_attn_body(q_ref, k_ref, v_ref, o_ref):
    for h in range(MEM_HEADS):
        ls = slice(h * MEM_HD, (h + 1) * MEM_HD)
        s = _dot_nt(q_ref[:, ls], k_ref[0, :, ls]) * (MEM_HD ** -0.5)
        p = jnp.exp(s - jnp.max(s, axis=-1, keepdims=True))
        o_ref[:, ls] = _dot(p, v_ref[0, :, ls]) / jnp.sum(p, axis=-1, keepdims=True)


def mem_attn(q, mk, mv, *, nseq, seq_len, tq):
    m = nseq * seq_len
    n = seq_len // tq
    kv = pl.BlockSpec((1, MEM_LEN, BRANCH_W), lambda b, i: (b, 0, 0))
    return pl.pallas_call(
        _mem_attn_body,
        grid=(nseq, n),
        in_specs=[pl.BlockSpec((tq, BRANCH_W), lambda b, i: (b * n + i, 0)), kv, kv],
        out_specs=pl.BlockSpec((tq, BRANCH_W), lambda b, i: (b * n + i, 0)),
        out_shape=jax.ShapeDtypeStruct((m, BRANCH_W), F32),
        compiler_params=_cparams("parallel", "parallel"),
        name="mem_attn",
    )(q, mk, mv)


def _mem_attn_rows_body(q_ref, k_ref, v_ref, o_ref):
    nq = q_ref.shape[0]
    nr = MEM_HEADS * nq
    nk = MEM_LEN * MEM_HEADS
    q_all = jnp.concatenate([q_ref[:, h * MEM_HD:(h + 1) * MEM_HD] for h in range(MEM_HEADS)], axis=0)
    head_ok = (_iota2((nr, nk), 0) // nq) == (_iota2((nr, nk), 1) % MEM_HEADS)
    s = jnp.where(head_ok, _dot_nt(q_all, k_ref[...]) * (MEM_HD ** -0.5), NEG)
    p = jnp.exp(s - jnp.max(s, axis=-1, keepdims=True))
    out = _dot(p, v_ref[...]) / jnp.sum(p, axis=-1, keepdims=True)
    for h in range(MEM_HEADS):
        o_ref[:, h * MEM_HD:(h + 1) * MEM_HD] = out[h * nq:(h + 1) * nq]


def mem_attn_rows(q, mk, mv, *, layer, nseq, tq):
    kv = pl.BlockSpec((None, None, MEM_LEN * MEM_HEADS, MEM_HD), lambda b: (layer, b, 0, 0))
    return pl.pallas_call(
        _mem_attn_rows_body,
        grid=(nseq,),
        in_specs=[pl.BlockSpec((tq, BRANCH_W), lambda b: (b, 0)), kv, kv],
        out_specs=pl.BlockSpec((tq, BRANCH_W), lambda b: (b, 0)),
        out_shape=jax.ShapeDtypeStruct((nseq * tq, BRANCH_W), F32),
        compiler_params=_cparams("parallel"),
        name="mem_attn_rows",
    )(q, mk, mv)


def gla_state_to_tiles(s):
    st = jnp.swapaxes(s, -1, -2)
    z = jnp.zeros_like(st)
    even = jnp.concatenate([st, z], axis=-1)
    odd = jnp.concatenate([z, st], axis=-1)
    sel = (jnp.arange(GLA_HEADS) % 2 == 1)[None, :, None, None]
    return jnp.where(sel, odd, even)


def gla_tiles_to_state(t):
    even = t[..., :GLA_DK]
    odd = t[..., GLA_DK:]
    sel = (jnp.arange(GLA_HEADS) % 2 == 1)[None, :, None, None]
    return jnp.swapaxes(jnp.where(sel, odd, even), -1, -2)


ROUTER_W = LANES
MOE_TB = 256


def _router_body(x_ref, g_ref, w1_ref, w2_ref, b_ref, lo_ref, h_ref):
    x = x_ref[...]
    h = x * lax.rsqrt(jnp.mean(x * x, axis=-1, keepdims=True) + EPS) * g_ref[...]
    h1 = h.astype(BF16)
    h2 = (h - h1.astype(F32)).astype(BF16)
    d = lambda a, b: jnp.dot(a, b, preferred_element_type=F32)
    lo_ref[...] = (d(h2, w1_ref[...]) + d(h1, w2_ref[...])) + d(h1, w1_ref[...]) + b_ref[...]
    half = h.shape[1] // 2
    u = pltpu.bitcast(h, jnp.uint32)
    rne = (u + jnp.uint32(0x7FFF) + ((u >> 16) & jnp.uint32(1))) & jnp.uint32(0xFFFF0000)
    h_ref[...] = rne[:, :half] | (rne[:, half:] >> 16)


def _unpack_bf16_pairs(packed):
    hi = pltpu.bitcast(packed & jnp.uint32(0xFFFF0000), F32).astype(BF16)
    lo = pltpu.bitcast(packed << 16, F32).astype(BF16)
    return hi, lo


def router(x, g, w_hi, w_lo, b, *, tm):
    m, d = x.shape
    full = lambda a: pl.BlockSpec(a.shape, lambda i: (0,) * a.ndim)
    g2 = g.reshape(1, d)
    return pl.pallas_call(
        _router_body,
        grid=(m // tm,),
        in_specs=[pl.BlockSpec((tm, d), lambda i: (i, 0)), full(g2), full(w_hi), full(w_lo), full(b)],
        out_specs=[pl.BlockSpec((tm, ROUTER_W), lambda i: (i, 0)), pl.BlockSpec((tm, d // 2), lambda i: (i, 0))],
        out_shape=[jax.ShapeDtypeStruct((m, ROUTER_W), F32), jax.ShapeDtypeStruct((m, d // 2), jnp.uint32)],
        compiler_params=_cparams("parallel"),
        name="router",
    )(x, g2, w_hi, w_lo, b)


def _moe_body(be_ref, nv_ref, xs_ref, w1_ref, w3_ref, w2_ref, o_ref, w1b, w3b, w2b):
    i = pl.program_id(0)

    @pl.when(i < nv_ref[0])
    def _():
        @pl.when((i == 0) | (be_ref[i] != be_ref[jnp.maximum(i - 1, 0)]))
        def _():
            w1b[...] = w1_ref[...].astype(BF16)
            w3b[...] = w3_ref[...].astype(BF16)
            w2b[...] = w2_ref[...].astype(BF16)

        x_hi, x_lo = _unpack_bf16_pairs(xs_ref[...])
        half = x_hi.shape[1]
        up = lambda w: (jnp.dot(x_hi, w[0:half, :], preferred_element_type=F32)
                        + jnp.dot(x_lo, w[half:, :], preferred_element_type=F32))
        o_ref[...] = jnp.dot((_silu(up(w1b)) * up(w3b)).astype(BF16), w2b[...], preferred_element_type=F32)

    @pl.when(i >= nv_ref[0])
    def _():
        o_ref[...] = jnp.zeros_like(o_ref)


def moe_experts(block_e, n_valid, xs, w1, w3, w2, *, layer):
    n_rows = xs.shape[0]
    d, f = w1.shape[-2:]
    grid_spec = pltpu.PrefetchScalarGridSpec(
        num_scalar_prefetch=2, grid=(n_rows // MOE_TB,),
        in_specs=[pl.BlockSpec((MOE_TB, d // 2), lambda i, be, nv: (i, 0)),
                  pl.BlockSpec((None, None, d, f), lambda i, be, nv: (layer, be[i], 0, 0)),
                  pl.BlockSpec((None, None, d, f), lambda i, be, nv: (layer, be[i], 0, 0)),
                  pl.BlockSpec((None, None, f, d), lambda i, be, nv: (layer, be[i], 0, 0))],
        out_specs=pl.BlockSpec((MOE_TB, d), lambda i, be, nv: (i, 0)),
        scratch_shapes=[pltpu.VMEM((d, f), BF16), pltpu.VMEM((d, f), BF16), pltpu.VMEM((f, d), BF16)])
    return pl.pallas_call(
        _moe_body,
        grid_spec=grid_spec,
        out_shape=jax.ShapeDtypeStruct((n_rows, d), F32),
        compiler_params=_cparams("arbitrary"),
        name="moe_experts",
    )(block_e, n_valid, xs, w1, w3, w2)


def _route(logits):
    m = logits.shape[0]
    pg = jax.nn.softmax(logits[:, :N_GROUPS], axis=-1)
    grp = jnp.argmax(pg, axis=-1)
    pg_top = jnp.max(pg, axis=-1)
    le = logits[:, N_GROUPS:N_GROUPS + N_EXPERTS].reshape(m, N_GROUPS, EXPERTS_PER_GROUP)
    pe = jax.nn.softmax(le[jnp.arange(m), grp], axis=-1)
    top_p, top_i = lax.top_k(pe, TOP_K)
    wts = pg_top[:, None] * top_p / jnp.sum(top_p, axis=-1, keepdims=True)
    eid = grp[:, None] * EXPERTS_PER_GROUP + top_i
    return eid, wts


def _sorted_layout(eid, m):
    n_slots = m * TOP_K
    flat_e = eid.reshape(n_slots)
    order = jnp.argsort(flat_e)
    sorted_e = flat_e[order]
    counts = jnp.bincount(flat_e, length=N_EXPERTS)
    padded = (counts + MOE_TB - 1) // MOE_TB * MOE_TB
    pad_end = jnp.cumsum(padded)
    pad_start = pad_end - padded
    start = jnp.cumsum(counts) - counts
    dest = (pad_start[sorted_e] + jnp.arange(n_slots) - start[sorted_e]).astype(jnp.int32)
    n_blocks = -(-n_slots // MOE_TB) + N_EXPERTS
    block_e = jnp.minimum(jnp.searchsorted(pad_end, jnp.arange(n_blocks) * MOE_TB, side='right'),
                          N_EXPERTS - 1).astype(jnp.int32)
    n_valid = (pad_end[-1:] // MOE_TB).astype(jnp.int32)
    rows = jnp.arange(n_blocks * MOE_TB)
    row_e = jnp.repeat(block_e, MOE_TB)
    i_in_e = rows - pad_start[row_e]
    src = jnp.clip(start[row_e] + i_in_e, 0, n_slots - 1)
    slot_tok = jnp.where(i_in_e < counts[row_e], order[src] // TOP_K, rows % m).astype(jnp.int32)
    pos = dest[jnp.argsort(order)].reshape(m, TOP_K)
    return slot_tok, block_e, n_valid, pos


def _rms_rows_body(x_ref, g_ref, o_ref):
    x = x_ref[...]
    o_ref[...] = x * lax.rsqrt(jnp.mean(x * x, axis=-1, keepdims=True) + EPS) * g_ref[...]


def rms_rows(x, g, *, row0, rows, tm):
    m, d = rows, x.shape[1]
    return pl.pallas_call(
        _rms_rows_body,
        grid=(m // tm,),
        in_specs=[pl.BlockSpec((tm, d), lambda i: (row0 // tm + i, 0)), pl.BlockSpec((1, d), lambda i: (0, 0))],
        out_specs=pl.BlockSpec((tm, d), lambda i: (i, 0)),
        out_shape=jax.ShapeDtypeStruct((m, d), F32),
        compiler_params=_cparams("parallel"),
        name="rms_rows",
    )(x, g.reshape(1, d))


def _slab(w, pieces, width):
    cols = []
    for start, stop, padded in pieces:
        cols.append(w[..., start:stop])
        if padded > stop - start:
            cols.append(jnp.zeros(w.shape[:-1] + (padded - (stop - start),), w.dtype))
    out = jnp.concatenate(cols, axis=-1)
    assert out.shape[-1] == width, (out.shape, width)
    return out


_DN0, _RW0, _GLA0, _FOX0 = 0, 2056, 4040, 5592
_DN_PIECES = [(_DN0, _DN0 + 2048, 2048), (_DN0 + 2048, _DN0 + 2056, 256)]
_RW_REL = [(0, 1536, 1536), (1536, 1632, 128), (1632, 1728, 128), (1728, 1984, 256)]
_RW_PIECES = [(_RW0 + a, _RW0 + b, p) for a, b, p in _RW_REL]
_GLA_PIECES = [(_GLA0, _GLA0 + 1536, 1536), (_GLA0 + 1536, _GLA0 + 1552, 256)]
_FOX_PIECES = [(_FOX0, _FOX0 + 1536, 1536), (_FOX0 + 1536, _FOX0 + 1540, 256)]


def _rw_unslab(x):
    return jnp.concatenate([x[..., :1632], x[..., RW_AD0:RW_AD0 + 96], x[..., RW_GD0:]], axis=-1)


def _pad_rows8(x, nseq, seq_len):
    return jnp.pad(x.reshape(nseq, seq_len, x.shape[-1]), ((0, 0), (0, SUBLANES - seq_len), (0, 0)))


def kernel(x_prompt, x_sample, cache_fox_k, cache_fox_v, cache_fox_logf, state_dn_conv, state_dn, state_rw_shift, state_rw, state_gla, cache_mem_k, cache_mem_v, page_table, mem_prompt, norm_mix, w_in, dn_conv_w, dn_A_log, dn_dt_bias, dn_norm_w, rw_mu, rw_w0, rw_w_up, rw_a0, rw_a_up, rw_g_up, rw_k_k, rw_k_a, rw_r_k, rw_ln_w, rw_ln_b, gla_a_up, gla_a_b, gla_norm_w, fox_qn, fox_kn, fox_fb, w_gate, b_gate, w_branch, w_out, norm_mem, norm_memkv, mem_wq, mem_wk, mem_wv, mem_wo, norm_ffn, router_g, router_g_b, router_e, router_e_b, moe_w1, moe_w3, moe_w2, norm_final):
    bp, lp, d = x_prompt.shape
    bs, ls, _ = x_sample.shape
    depth = w_in.shape[0]
    n_p = bp * lp
    n_s = bs * ls
    m = n_p + n_s
    n_phys = cache_fox_k.shape[1]
    n_mem = mem_prompt.shape[1]

    w_dn = _slab(w_in, _DN_PIECES, DN_SLAB).astype(BF16)
    w_rw = _slab(w_in, _RW_PIECES, RW_SLAB).astype(BF16)
    w_gla = _slab(w_in, _GLA_PIECES, GLA_SLAB).astype(BF16)
    w_fox = _slab(w_in, _FOX_PIECES, FOX_SLAB).astype(BF16)
    mu_r = _slab(rw_mu, _RW_REL, RW_SLAB)
    pad_lora = lambda w: jnp.pad(w, ((0, 0), (0, LANES - w.shape[1]), (0, 0))).astype(BF16)
    rw_w_up_b, rw_a_up_b, rw_g_up_b = pad_lora(rw_w_up), pad_lora(rw_a_up), rw_g_up.astype(BF16)
    w_gate_b, w_branch_b, w_out_b = w_gate.astype(BF16), w_branch.astype(BF16), w_out.astype(BF16)
    mem_wq_b, mem_wo_b = mem_wq.astype(BF16), mem_wo.astype(BF16)
    mem_wkv_b = jnp.concatenate([mem_wk, mem_wv], axis=-1).astype(BF16)
    w_router = jnp.pad(jnp.concatenate([router_g, router_e], axis=-1),
                       ((0, 0), (0, 0), (0, ROUTER_W - N_GROUPS - N_EXPERTS)))
    w_router_hi = w_router.astype(BF16)
    w_router_lo = (w_router - w_router_hi.astype(F32)).astype(BF16)
    b_router = jnp.pad(jnp.concatenate([router_g_b, router_e_b], axis=-1),
                       ((0, 0), (0, ROUTER_W - N_GROUPS - N_EXPERTS))).reshape(depth, 1, ROUTER_W)

    x = jnp.concatenate([x_prompt.reshape(n_p, d), x_sample.reshape(n_s, d)], axis=0)
    mem_x = mem_prompt.reshape(bp * n_mem, d)
    zeros = lambda *shape: jnp.zeros(shape, F32)
    rows = lambda a_p, a_s: jnp.concatenate([a_p, a_s], axis=0)
    unpad8 = lambda a: a.reshape(bs, SUBLANES, a.shape[-1])[:, :ls].reshape(n_s, a.shape[-1])
    cache_k_rows = cache_fox_k.reshape(depth, n_phys, PAGE_ROWS, FOX_HD)
    cache_v_rows = cache_fox_v.reshape(depth, n_phys, PAGE_ROWS, FOX_HD)
    cache_lf_rows = cache_fox_logf.reshape(depth, n_phys, 1, PAGE_ROWS)
    mem_k_rows = cache_mem_k.reshape(depth, bs, n_mem * MEM_HEADS, MEM_HD)
    mem_v_rows = cache_mem_v.reshape(depth, bs, n_mem * MEM_HEADS, MEM_HD)
    p_out, s_out = [], []

    for l in range(depth):
        g_mix = norm_mix[l]
        proj = lambda w, tn, name: (
            rms_matmul(x, g_mix, w, tm=1024, tn=tn, rows=n_p, name=name),
            rms_matmul(x, g_mix, w, tm=512, tn=tn, row0=n_p, rows=n_s, name=name + "_s"))
        p_dn, p_dn_s = proj(w_dn[l], 768, "in_proj_dn")
        p_rw, p_rw_s = proj(w_rw[l], 1024, "in_proj_rw")
        p_gla, p_gla_s = proj(w_gla[l], 896, "in_proj_gla")
        p_fox, pf_s = proj(w_fox[l], 896, "in_proj_fox")

        dn_args = (dn_conv_w[l], dn_A_log[l], dn_dt_bias[l], dn_norm_w[l])
        o_dn_p, dn_p = deltanet(p_dn, *dn_args, zeros(bp, SUBLANES, DN_CONV_CH),
                                zeros(1, bp, DN_HEADS, DN_DK, DN_DV), layer=0, nseq=bp, seq_len=lp, c=DN_CHUNK,
                                valid=DN_CHUNK,
                                nb=DN_STACK // (DN_HEADS * DN_CHUNK))
        dn_slab_s = _pad_rows8(p_dn_s, bs, ls)
        conv0_s = jnp.pad(state_dn_conv[l], ((0, 0), (SUBLANES - DN_CONV + 1, 0), (0, 0)))
        o_dn_s, dn_s = deltanet(dn_slab_s.reshape(bs * SUBLANES, DN_SLAB), *dn_args, conv0_s, state_dn,
                                layer=l, nseq=bs, seq_len=SUBLANES, c=SUBLANES, valid=ls,
                                nb=DN_STACK // (DN_HEADS * SUBLANES))
        dn_conv_p = p_dn.reshape(bp, lp, DN_SLAB)[:, lp - (DN_CONV - 1):, :DN_CONV_CH]
        dn_conv_s = dn_slab_s[:, ls - (DN_CONV - 1):ls, :DN_CONV_CH]

        rw_p = p_rw.reshape(bp, lp, RW_SLAB)
        rw_s = p_rw_s.reshape(bs, ls, RW_SLAB)
        prev_s = jnp.concatenate([_slab(state_rw_shift[l], _RW_REL, RW_SLAB)[:, None], rw_s[:, :-1]],
                                 axis=1).reshape(n_s, RW_SLAB)
        rw_args = (mu_r[l], rw_w0[l], rw_a0[l], rw_k_k[l], rw_k_a[l], rw_r_k[l].reshape(-1), rw_w_up_b[l],
                   rw_a_up_b[l], rw_g_up_b[l])
        tok_p = rwkv_prep(p_rw, zeros(SUBLANES, RW_SLAB), *rw_args, tm=256, seq_len=lp)
        tok_s = rwkv_prep(p_rw_s, prev_s, *rw_args, tm=256)
        y_p, rw_tiles_p = rwkv_scan(*(a.reshape(bp, lp, RW_W) for a in tok_p[:6]),
                                    zeros(bp, RW_W // RW_QW, RW_N, RW_QW), nb=bp, t_blk=64)
        y_s, rw_tiles_s = rwkv_scan(*(a.reshape(bs, ls, RW_W) for a in tok_s[:6]),
                                    rw_state_to_tiles(state_rw[l]), nb=8, t_blk=ls)
        o_rw_p = rwkv_post(y_p.reshape(n_p, RW_W), tok_p[7], tok_p[6], rw_ln_w[l], rw_ln_b[l], tm=512)
        o_rw_s = rwkv_post(y_s.reshape(n_s, RW_W), tok_s[7], tok_s[6], rw_ln_w[l], rw_ln_b[l], tm=512)

        gla_args = (gla_a_up[l], gla_a_b[l], gla_norm_w[l])
        o_gla_p, gla_tiles_p = gla(p_gla.reshape(bp, lp, GLA_SLAB), *gla_args,
                                   zeros(bp, GLA_HEADS, GLA_DV, LANES), c=GLA_CHUNK, valid=GLA_CHUNK,
                                   nb=GLA_STACK // GLA_CHUNK)
        o_gla_s, gla_tiles_s = gla(_pad_rows8(p_gla_s, bs, ls), *gla_args, gla_state_to_tiles(state_gla[l]),
                                   c=SUBLANES, valid=ls, nb=GLA_STACK // SUBLANES)

        fox_args = (fox_qn[l], fox_kn[l], fox_fb[l])
        ft_p = p_fox[:, FOX_F0:FOX_F0 + SUBLANES].reshape(bp, lp, SUBLANES).transpose(0, 2, 1)
        qb_p, kn_p, kb_p, vb_p, lf_p, c_p, ct_p = fox_prep(p_fox, ft_p, *fox_args, nseq=bp, seq_len=lp, tm=FOX_TQ)
        o_fox_p = fox_prompt(qb_p, kb_p, vb_p, c_p, ct_p, nseq=bp, L=lp, tq=FOX_TQ)
        ft_s = pf_s[:, FOX_F0:FOX_F0 + SUBLANES].T[None]
        qb_s, kn_s, _, _, lf_s, _, _ = fox_prep(pf_s, ft_s, *fox_args, nseq=1, seq_len=n_s, tm=256)
        new_rows = lambda a: jnp.pad(a.reshape(bs, ls * FOX_HEADS, FOX_HD),
                                     ((0, 0), (0, NEW_ROWS - ls * FOX_HEADS), (0, 0)))
        lfn = jnp.pad(lf_s[:, :FOX_HEADS].reshape(bs, 1, ls * FOX_HEADS),
                      ((0, 0), (0, 0), (0, PAGE_ROWS - ls * FOX_HEADS)))
        o_fox_s = fox_sample(page_table, _pad_rows8(qb_s, bs, ls), cache_k_rows, cache_v_rows, cache_lf_rows,
                             new_rows(kn_s), new_rows(pf_s[:, FOX_V0:FOX_V0 + BRANCH_W]), lfn, layer=l, n_new=ls)

        outs_p = [o_dn_p, o_rw_p, o_gla_p, o_fox_p]
        outs_s = [unpad8(o_dn_s), o_rw_s, unpad8(o_gla_s), o_fox_s[:, :ls].reshape(n_s, BRANCH_W)]
        x = merge_out(x, g_mix, outs_p, outs_s, w_gate_b[l], b_gate[l], w_branch_b[l], w_out_b[l], tm=512, tn=256)

        q_mem = rms_matmul(x, norm_mem[l], mem_wq_b[l], tm=512, tn=512, name="mem_q")
        mkv = rms_matmul(mem_x, norm_memkv[l], mem_wkv_b[l], tm=512, tn=512, name="mem_kv")
        mk = mkv[:, :BRANCH_W].reshape(bp, n_mem, BRANCH_W)
        mv = mkv[:, BRANCH_W:].reshape(bp, n_mem, BRANCH_W)
        att_p = mem_attn(q_mem, mk, mv, nseq=bp, seq_len=lp, tq=512)
        att_s = mem_attn_rows(_pad_rows8(q_mem[n_p:], bs, ls).reshape(bs * SUBLANES, BRANCH_W), mem_k_rows,
                              mem_v_rows, layer=l, nseq=bs, tq=SUBLANES)
        x = matmul_res(rows(att_p, unpad8(att_s)), mem_wo_b[l], x, tm=512, tn=1024, name="mem_out")

        logits, h_ffn = router(x, norm_ffn[l], w_router_hi[l], w_router_lo[l], b_router[l], tm=512)
        eid, wts = _route(logits)
        slot_tok, block_e, n_valid, pos = _sorted_layout(eid, m)
        xs = h_ffn[slot_tok]
        ys = moe_experts(block_e, n_valid, xs, moe_w1, moe_w3, moe_w2, layer=l)
        x = x + wts[:, 0:1] * ys[pos[:, 0]] + wts[:, 1:2] * ys[pos[:, 1]]

        p_out.append((dn_conv_p, dn_p, _rw_unslab(rw_p[:, -1]), rw_tiles_to_state(rw_tiles_p),
                      gla_tiles_to_state(gla_tiles_p),
                      kn_p.reshape(bp, lp, FOX_HEADS, FOX_HD),
                      p_fox[:, FOX_V0:FOX_V0 + BRANCH_W].reshape(bp, lp, FOX_HEADS, FOX_HD),
                      lf_p[:, :FOX_HEADS].reshape(bp, lp, FOX_HEADS),
                      mk.reshape(bp, n_mem, MEM_HEADS, MEM_HD), mv.reshape(bp, n_mem, MEM_HEADS, MEM_HD)))
        s_out.append((dn_conv_s, dn_s, _rw_unslab(rw_s[:, -1]), rw_tiles_to_state(rw_tiles_s),
                      gla_tiles_to_state(gla_tiles_s),
                      kn_s.reshape(bs, ls, FOX_HEADS, FOX_HD),
                      pf_s[:, FOX_V0:FOX_V0 + BRANCH_W].reshape(bs, ls, FOX_HEADS, FOX_HD),
                      lf_s[:, :FOX_HEADS].reshape(bs, ls, FOX_HEADS)))

    (p_dn_conv, p_dn_st, p_rw_shift, p_rw_st, p_gla_st, p_fox_k, p_fox_v, p_fox_logf, p_mem_k,
     p_mem_v) = [jnp.stack(r) for r in zip(*p_out)]
    (s_dn_conv, s_dn_st, s_rw_shift, s_rw_st, s_gla_st, s_fox_k, s_fox_v,
     s_fox_logf) = [jnp.stack(r) for r in zip(*s_out)]
    y_prompt = rms_rows(x, norm_final, row0=0, rows=n_p, tm=512).reshape(bp, lp, d)
    y_sample = rms_rows(x, norm_final, row0=n_p, rows=n_s, tm=512).reshape(bs, ls, d)
    return (y_prompt, y_sample, p_fox_k, p_fox_v, p_fox_logf, p_dn_conv, p_dn_st, p_rw_shift, p_rw_st, p_gla_st,
            p_mem_k, p_mem_v, s_fox_k, s_fox_v, s_fox_logf, s_dn_conv, s_dn_st, s_rw_shift, s_rw_st, s_gla_st)
```

```python
import functools
import math

import jax
import jax.numpy as jnp
from jax import lax
from jax.experimental import pallas as pl
from jax.experimental.pallas import tpu as pltpu

F32 = jnp.float32
BF16 = jnp.bfloat16
EPS = 1e-6
HP = lax.Precision.HIGHEST

D_MODEL = 2048
N_BRANCH = 4
BRANCH_W = 512
PAGE_SIZE = 128

DN_HEADS, DN_DK, DN_DV, DN_CONV, DN_CHUNK = 4, 128, 128, 4, 64
DN_CONV_CH = 1536
RW_HEADS, RW_N, RW_W = 8, 64, 512
RW_LORA_W, RW_LORA_A, RW_LORA_G = 96, 96, 256
RW_GN_EPS = 64e-5
RW_COLS = 1984
GLA_HEADS, GLA_DK, GLA_DV, GLA_LORA, GLA_TAU = 4, 64, 128, 16, 16.0
FOX_HEADS, FOX_HD = 4, 128
MEM_HEADS, MEM_HD, MEM_LEN = 4, 128, 256
N_GROUPS, EXPERTS_PER_GROUP, N_EXPERTS, TOP_K, D_EXPERT = 4, 8, 32, 2, 512

LANES = 128
SUBLANES = 8
VMEM_LIMIT = 56 * 1024 * 1024


def _cparams(*sem):
    return pltpu.CompilerParams(dimension_semantics=sem, vmem_limit_bytes=VMEM_LIMIT)


def _dot(a, b):
    return jnp.dot(a.astype(BF16), b.astype(BF16), preferred_element_type=F32)


def _dot_nt(a, b):
    return lax.dot_general(a.astype(BF16), b.astype(BF16), (((1,), (1,)), ((), ())), preferred_element_type=F32)


def _dot_tn(a, b):
    return lax.dot_general(a.astype(BF16), b.astype(BF16), (((0,), (0,)), ((), ())), preferred_element_type=F32)


def _split3(a):
    a1 = a.astype(BF16)
    r1 = a - a1.astype(F32)
    a2 = r1.astype(BF16)
    a3 = (r1 - a2.astype(F32)).astype(BF16)
    return a1, a2, a3


def _dot_hp_exact_rhs(a, b_bf16):
    a1, a2, a3 = _split3(a)
    d = lambda x: jnp.dot(x, b_bf16, preferred_element_type=F32)
    return (d(a3) + d(a2)) + d(a1)


def _dot_hp_exact_lhs(a_bf16, b):
    b1, b2, b3 = _split3(b)
    d = lambda y: jnp.dot(a_bf16, y, preferred_element_type=F32)
    return (d(b3) + d(b2)) + d(b1)


def _iota2(shape, axis):
    return lax.broadcasted_iota(jnp.int32, shape, axis)


def _sigmoid(x):
    return 1.0 / (1.0 + jnp.exp(-x))


def _silu(x):
    return x * _sigmoid(x)


def _softplus(x):
    return jnp.maximum(x, 0.0) + jnp.log(1.0 + jnp.exp(-jnp.abs(x)))


def _log_sigmoid(x):
    return -_softplus(-x)


def _rms_matmul_body(x_ref, g_ref, w_ref, o_ref, h_ref):
    @pl.when(pl.program_id(1) == 0)
    def _():
        x = x_ref[...]
        ms = jnp.mean(x * x, axis=-1, keepdims=True)
        h_ref[...] = (x * lax.rsqrt(ms + EPS) * g_ref[...]).astype(BF16)

    o_ref[...] = jnp.dot(h_ref[...], w_ref[...], preferred_element_type=F32).astype(o_ref.dtype)


def rms_matmul(x, g, w, *, tm, tn, row0=0, rows=None, out_dtype=F32, name="rms_matmul"):
    k = x.shape[1]
    m = x.shape[0] if rows is None else rows
    n = w.shape[1]
    assert m % tm == 0 and row0 % tm == 0 and n % tn == 0
    return pl.pallas_call(
        _rms_matmul_body,
        grid=(m // tm, n // tn),
        in_specs=[pl.BlockSpec((tm, k), lambda i, j: (row0 // tm + i, 0)),
                  pl.BlockSpec((1, k), lambda i, j: (0, 0)),
                  pl.BlockSpec((k, tn), lambda i, j: (0, j))],
        out_specs=pl.BlockSpec((tm, tn), lambda i, j: (i, j)),
        out_shape=jax.ShapeDtypeStruct((m, n), out_dtype),
        scratch_shapes=[pltpu.VMEM((tm, k), BF16)],
        compiler_params=_cparams("parallel", "arbitrary"),
        name=name,
    )(x, g.reshape(1, k), w)


def _matmul_res_body(a_ref, w_ref, r_ref, o_ref):
    o_ref[...] = r_ref[...] + jnp.dot(a_ref[...].astype(BF16), w_ref[...], preferred_element_type=F32)


def matmul_res(a, w, res, *, tm, tn, name="matmul_res"):
    m, k = a.shape
    n = w.shape[1]
    assert m % tm == 0 and n % tn == 0
    return pl.pallas_call(
        _matmul_res_body,
        grid=(m // tm, n // tn),
        in_specs=[pl.BlockSpec((tm, k), lambda i, j: (i, 0)),
                  pl.BlockSpec((k, tn), lambda i, j: (0, j)),
                  pl.BlockSpec((tm, tn), lambda i, j: (i, j))],
        out_specs=pl.BlockSpec((tm, tn), lambda i, j: (i, j)),
        out_shape=jax.ShapeDtypeStruct((m, n), F32),
        compiler_params=_cparams("parallel", "parallel"),
        name=name,
    )(a, w, res)


def _merge_body(x_ref, g_ref, *refs, first_tiles):
    o_first, o_rest = refs[0:N_BRANCH], refs[N_BRANCH:2 * N_BRANCH]
    wgs = refs[2 * N_BRANCH:3 * N_BRANCH]
    bg_ref, wb_ref, wo_ref, y_ref, h_ref, ob_ref = refs[3 * N_BRANCH:]
    i = pl.program_id(0)

    @pl.when(pl.program_id(1) == 0)
    def _():
        x = x_ref[...]
        ms = jnp.mean(x * x, axis=-1, keepdims=True)
        h_ref[...] = (x * lax.rsqrt(ms + EPS) * g_ref[...]).astype(BF16)
        y_ref[...] = x

        @pl.when(i < first_tiles)
        def _():
            for n in range(N_BRANCH):
                ob_ref[n] = o_first[n][...].astype(BF16)

        @pl.when(i >= first_tiles)
        def _():
            for n in range(N_BRANCH):
                ob_ref[n] = o_rest[n][...].astype(BF16)

    h = h_ref[...]
    merged = None
    for n in range(N_BRANCH):
        gate = _sigmoid(jnp.dot(h, wgs[n][...], preferred_element_type=F32) + bg_ref[n:n + 1, :])
        br = jnp.dot(ob_ref[n], wb_ref[n], preferred_element_type=F32)
        merged = gate * br if merged is None else merged + gate * br
    y_ref[...] += jnp.dot(merged.astype(BF16), wo_ref[...], preferred_element_type=F32)


def merge_out(x, g, outs_first, outs_rest, w_gate, b_gate, w_branch, w_out, *, tm, tn):
    m, d = x.shape
    nj = d // tn
    first_tiles = outs_first[0].shape[0] // tm
    assert outs_first[0].shape[0] % tm == 0 and outs_rest[0].shape[0] == m - first_tiles * tm
    wg_specs = [pl.BlockSpec((d, tn), functools.partial(lambda i, j, n: (0, n * nj + j), n=n))
                for n in range(N_BRANCH)]
    return pl.pallas_call(
        functools.partial(_merge_body, first_tiles=first_tiles),
        grid=(m // tm, nj),
        in_specs=[pl.BlockSpec((tm, d), lambda i, j: (i, 0)),
                  pl.BlockSpec((1, d), lambda i, j: (0, 0))]
                 + [pl.BlockSpec((tm, BRANCH_W), lambda i, j: (jnp.minimum(i, first_tiles - 1), 0))] * N_BRANCH
                 + [pl.BlockSpec((tm, BRANCH_W), lambda i, j: (jnp.maximum(i - first_tiles, 0), 0))] * N_BRANCH
                 + wg_specs
                 + [pl.BlockSpec((N_BRANCH, tn), lambda i, j: (0, j)),
                    pl.BlockSpec((N_BRANCH, BRANCH_W, tn), lambda i, j: (0, 0, j)),
                    pl.BlockSpec((tn, d), lambda i, j: (j, 0))],
        out_specs=pl.BlockSpec((tm, d), lambda i, j: (i, 0)),
        out_shape=jax.ShapeDtypeStruct((m, d), F32),
        scratch_shapes=[pltpu.VMEM((tm, d), BF16), pltpu.VMEM((N_BRANCH, tm, BRANCH_W), BF16)],
        compiler_params=_cparams("parallel", "arbitrary"),
        name="merge_out",
    )(x, g.reshape(1, d), *outs_first, *outs_rest, w_gate, w_gate, w_gate, w_gate, b_gate.reshape(N_BRANCH, d),
      w_branch, w_out)


DN_Z0 = DN_CONV_CH
DN_AB0 = DN_CONV_CH + BRANCH_W
DN_SLAB = DN_AB0 + 256
INV_BLOCK = 16
DN_STACK = 256


def _inv_unit_lower(nmat, size, c):
    row = _iota2((size, size), 0)
    col = _iota2((size, size), 1)
    eye = (row == col).astype(F32)
    blk = min(INV_BLOCK, c)
    if c > blk:
        same = (row // blk) == (col // blk)
        d = jnp.where(same, nmat, 0.0)
        r = jnp.where(same, 0.0, nmat)
    else:
        d, r = nmat, None
    t = eye - d
    p = d
    k = 2
    while k < blk:
        p = _dot(p, p)
        t = _dot(t, eye + p)
        k *= 2
    if r is None:
        return t
    pm = _dot(t, r)
    t2 = eye - pm
    q = pm
    k = 2
    while k < c // blk:
        q = _dot(q, q)
        t2 = _dot(t2, eye + q)
        k *= 2
    return _dot(t2, t)


def _dn_body(*refs, nb, c, valid):
    p_refs = refs[0:nb]
    cw_ref, alog_ref, dtb_ref, nw_ref, conv0_ref, s0_ref, o_ref, s1_ref, cbuf, s_scr = refs[nb:]
    ci = pl.program_id(1)
    nch = nb * DN_HEADS
    size = nch * c

    @pl.when(ci == 0)
    def _():
        s_scr[...] = s0_ref[...].reshape(s_scr.shape)
        cbuf[:, 0:SUBLANES, :] = conv0_ref[...]

    vcol = _iota2((c, 1), 0) < valid
    ltri = (_iota2((c, c), 0) >= _iota2((c, c), 1)).astype(BF16)
    q_l, k_l, v_l, z_l, gc_l, beta_l = [], [], [], [], [], []
    for sq in range(nb):
        p_ref, cb = p_refs[sq], cbuf.at[sq]
        cb[SUBLANES:SUBLANES + c, :] = p_ref[:, 0:DN_CONV_CH]
        conv = cw_ref[0:1, :] * cb[5:5 + c, :]
        for j in range(1, DN_CONV):
            conv = conv + cw_ref[j:j + 1, :] * cb[5 + j:5 + j + c, :]
        cb[0:SUBLANES, :] = cb[c:c + SUBLANES, :]
        act = _silu(conv)
        ab = p_ref[:, DN_AB0:DN_AB0 + LANES]
        g_all = jnp.where(vcol, -jnp.exp(alog_ref[...]) * _softplus(ab + dtb_ref[...]), 0.0)
        beta_all = jnp.where(vcol, _sigmoid(ab), 0.0)
        gc_all = _dot_hp_exact_lhs(ltri, g_all)
        for h in range(DN_HEADS):
            q = act[:, h * DN_DK:(h + 1) * DN_DK]
            k = act[:, 512 + h * DN_DK:512 + (h + 1) * DN_DK]
            q_l.append(q * lax.rsqrt(jnp.sum(q * q, axis=-1, keepdims=True) + EPS) * (DN_DK ** -0.5))
            k_l.append(jnp.where(vcol, k * lax.rsqrt(jnp.sum(k * k, axis=-1, keepdims=True) + EPS), 0.0))
            v_l.append(jnp.where(vcol, act[:, 1024 + h * DN_DV:1024 + (h + 1) * DN_DV], 0.0))
            z_l.append(p_ref[:, DN_Z0 + h * DN_DV:DN_Z0 + (h + 1) * DN_DV])
            gc_l.append(gc_all[:, h:h + 1])
            beta_l.append(beta_all[:, DN_HEADS + h:DN_HEADS + h + 1])
    stack = lambda xs: jnp.concatenate(xs, axis=0)
    q, k, v, z, gc, beta = (stack(x) for x in (q_l, k_l, v_l, z_l, gc_l, beta_l))

    row = _iota2((size, size), 0)
    col = _iota2((size, size), 1)
    same = (row // c) == (col // c)
    incl = same & (row >= col)
    strict = same & (row > col)
    gcr = _dot_hp_exact_lhs(jnp.ones((SUBLANES, size), BF16), jnp.where(row == col, gc, 0.0))[0:1, :]
    gamma = jnp.exp(jnp.where(incl, gc - gcr, -jnp.inf))
    tmat = _inv_unit_lower(jnp.where(strict, beta * _dot_nt(k, k) * gamma, 0.0), size, c)
    egc = jnp.exp(gc)
    u = _dot(tmat, beta * v)
    w = _dot(tmat, (beta * egc) * k)
    qk = jnp.where(incl, _dot_nt(q, k) * gamma, 0.0)
    qg = q * egc

    v_new, o_inter = [], []
    for i in range(nch):
        rs = slice(i * c, (i + 1) * c)
        s = s_scr[i]
        gl = gc[(i + 1) * c - 1:(i + 1) * c, :]
        vn = u[rs] - _dot(w[rs], s)
        o_inter.append(_dot(qg[rs], s))
        s_scr[i] = jnp.exp(gl) * s + _dot_tn(k[rs] * jnp.exp(gl - gc[rs]), vn)
        v_new.append(vn)
    o = stack(o_inter) + _dot(qk, stack(v_new))
    on = o * lax.rsqrt(jnp.mean(o * o, axis=-1, keepdims=True) + EPS) * nw_ref[...] * _silu(z)
    for i in range(nch):
        sq, h = divmod(i, DN_HEADS)
        o_ref[sq, :, h * DN_DV:(h + 1) * DN_DV] = on[i * c:(i + 1) * c]

    @pl.when(ci == pl.num_programs(1) - 1)
    def _():
        s1_ref[...] = s_scr[...].reshape(s1_ref.shape)


def deltanet(p_dn, conv_w, a_log, dt_bias, norm_w, conv0, s0, *, layer, nseq, seq_len, c, valid, nb):
    m = nseq * seq_len
    n = seq_len // c
    pad = jnp.zeros((LANES - DN_HEADS,), F32)
    alog = jnp.concatenate([a_log, pad]).reshape(1, LANES)
    dtb = jnp.concatenate([dt_bias, pad]).reshape(1, LANES)
    full = lambda shape: pl.BlockSpec(shape, lambda b, i: (0,) * len(shape))
    seq_rows = lambda sq: functools.partial(lambda b, i, sq: ((b * nb + sq) * n + i, 0), sq=sq)
    o, s1 = pl.pallas_call(
        functools.partial(_dn_body, nb=nb, c=c, valid=valid),
        grid=(nseq // nb, n),
        in_specs=[pl.BlockSpec((c, DN_SLAB), seq_rows(sq)) for sq in range(nb)]
                 + [full((DN_CONV, DN_CONV_CH)), full((1, LANES)), full((1, LANES)), full((1, DN_DV)),
                    pl.BlockSpec((nb, SUBLANES, DN_CONV_CH), lambda b, i: (b, 0, 0)),
                    pl.BlockSpec((None, nb, DN_HEADS, DN_DK, DN_DV), lambda b, i: (layer, b, 0, 0, 0))],
        out_specs=[pl.BlockSpec((nb, c, BRANCH_W), lambda b, i: (b, i, 0)),
                   pl.BlockSpec((nb, DN_HEADS, DN_DK, DN_DV), lambda b, i: (b, 0, 0, 0))],
        out_shape=[jax.ShapeDtypeStruct((nseq, seq_len, BRANCH_W), F32),
                   jax.ShapeDtypeStruct((nseq, DN_HEADS, DN_DK, DN_DV), F32)],
        scratch_shapes=[pltpu.VMEM((nb, c + 2 * SUBLANES, DN_CONV_CH), F32),
                        pltpu.VMEM((nb * DN_HEADS, DN_DK, DN_DV), F32)],
        compiler_params=_cparams("parallel", "arbitrary"),
        name="deltanet",
    )(*([p_dn] * nb), conv_w, alog, dtb, norm_w.reshape(1, DN_DV), conv0, s0)
    return o.reshape(m, BRANCH_W), s1


GLA_K0, GLA_V0, GLA_G0, GLA_GD0 = 256, 512, 1024, 1536
GLA_SLAB = 1792
GLA_CHUNK = 64
GLA_STACK = 256


def _gla_body(q_ref, k_ref, v_ref, g_ref, gd_ref, aup_ref, ab_ref, nw_ref, s0_ref,
              o_ref, s1_ref, st_scr, *, nb, c, valid):
    ci = pl.program_id(2)
    size = nb * c

    @pl.when(ci == 0)
    def _():
        st_scr[...] = s0_ref[...].reshape(st_scr.shape)

    row = _iota2((size, size), 0)
    col = _iota2((size, size), 1)
    incl = ((row // c) == (col // c)) & (row >= col)
    vcol = (_iota2((size, 1), 0) % c) < valid
    lane = _iota2((1, LANES), 1)
    rows2 = lambda ref, ls: ref[:, :, ls].reshape(size, ls.stop - ls.start)
    full = slice(0, LANES)

    gk = _log_sigmoid(_dot(rows2(gd_ref, full), aup_ref[...]) + ab_ref[...]) * (1.0 / GLA_TAU)
    gk = jnp.where(vcol, gk, 0.0)
    b = _dot_hp_exact_lhs(incl.astype(BF16), gk)
    q = rows2(q_ref, full) * (GLA_DK ** -0.5)
    k = jnp.where(vcol, rows2(k_ref, full), 0.0)
    qt = q * jnp.exp(b)
    kt = k * jnp.exp(-b)
    for hh in range(2):
        lm = (lane >= hh * GLA_DK) & (lane < (hh + 1) * GLA_DK)
        vs = slice(hh * GLA_DV, (hh + 1) * GLA_DV)
        qth = jnp.where(lm, qt, 0.0)
        v = jnp.where(vcol, rows2(v_ref, vs), 0.0)
        att = jnp.where(incl, _dot_nt(qth, kt), 0.0)
        o_inter = []
        for i in range(nb):
            rs = slice(i * c, (i + 1) * c)
            b_last = b[(i + 1) * c - 1:(i + 1) * c, :]
            st = st_scr[2 * i + hh]
            o_inter.append(_dot_nt(qth[rs], st))
            kd = jnp.where(lm, k[rs] * jnp.exp(b_last - b[rs]), 0.0)
            st_scr[2 * i + hh] = st * jnp.exp(b_last) + _dot_tn(v[rs], kd)
        o = _dot(att, v) + jnp.concatenate(o_inter, axis=0)
        on = o * lax.rsqrt(jnp.mean(o * o, axis=-1, keepdims=True) + EPS) * nw_ref[...] * _silu(rows2(g_ref, vs))
        o_ref[:, :, vs] = on.reshape(nb, c, GLA_DV)

    @pl.when(ci == pl.num_programs(2) - 1)
    def _():
        s1_ref[...] = st_scr[...].reshape(s1_ref.shape)


def gla(p_gla, a_up, a_b, norm_w, s0t, *, c, valid, nb):
    nseq, seq_len, _ = p_gla.shape
    n = seq_len // c
    aup = jnp.zeros((LANES, GLA_HEADS * GLA_DK), F32).at[:GLA_LORA].set(a_up).astype(BF16)
    cols = lambda w, off: pl.BlockSpec((nb, c, w), lambda b, p, i: (b, i, off // w + p))
    o, s1 = pl.pallas_call(
        functools.partial(_gla_body, nb=nb, c=c, valid=valid),
        grid=(nseq // nb, 2, n),
        in_specs=[cols(LANES, 0), cols(LANES, GLA_K0), cols(2 * GLA_DV, GLA_V0), cols(2 * GLA_DV, GLA_G0),
                  pl.BlockSpec((nb, c, LANES), lambda b, p, i: (b, i, GLA_GD0 // LANES)),
                  pl.BlockSpec((LANES, LANES), lambda b, p, i: (0, p)),
                  pl.BlockSpec((1, LANES), lambda b, p, i: (0, p)),
                  pl.BlockSpec((1, GLA_DV), lambda b, p, i: (0, 0)),
                  pl.BlockSpec((nb, 2, GLA_DV, LANES), lambda b, p, i: (b, p, 0, 0))],
        out_specs=[pl.BlockSpec((nb, c, 2 * GLA_DV), lambda b, p, i: (b, i, p)),
                   pl.BlockSpec((nb, 2, GLA_DV, LANES), lambda b, p, i: (b, p, 0, 0))],
        out_shape=[jax.ShapeDtypeStruct((nseq, seq_len, BRANCH_W), F32),
                   jax.ShapeDtypeStruct((nseq, GLA_HEADS, GLA_DV, LANES), F32)],
        scratch_shapes=[pltpu.VMEM((nb * 2, GLA_DV, LANES), F32)],
        compiler_params=_cparams("parallel", "parallel", "arbitrary"),
        name="gla",
    )(p_gla, p_gla, p_gla, p_gla, p_gla, aup, a_b.reshape(1, -1), norm_w.reshape(1, GLA_DV), s0t)
    return o.reshape(nseq * seq_len, BRANCH_W), s1


RW_WD0, RW_AD0, RW_GD0 = 1536, 1664, 1792
RW_SLAB = 2048
RW_QW = 256


def _seg_ones(n, seg):
    r = _iota2((n, n), 0) // seg
    c = _iota2((n, n), 1) // seg
    return (r == c).astype(BF16)


def _rw_prep_body(p_ref, prev_ref, mu_ref, w0_ref, a0_ref, kk_ref, ka_ref, rk_ref, wup_ref, aup_ref, gup_ref,
                  r_o, w_o, k_o, v_o, nkk_o, kka_o, gate_o, bonus_o, *scratch, tiles_per_seq):
    p = p_ref[...]
    if tiles_per_seq is None:
        prev = prev_ref[...]
    else:
        (buf,) = scratch
        tm = p.shape[0]

        @pl.when(pl.program_id(0) % tiles_per_seq == 0)
        def _():
            buf[0:SUBLANES, :] = prev_ref[...]

        buf[SUBLANES:SUBLANES + tm, :] = p
        prev = buf[SUBLANES - 1:SUBLANES - 1 + tm, :]
        buf[0:SUBLANES, :] = buf[tm:tm + SUBLANES, :]
    xr = p + mu_ref[...] * (prev - p)
    r = xr[:, 0:RW_W]
    k = xr[:, RW_W:2 * RW_W]
    v = xr[:, 2 * RW_W:3 * RW_W]
    wd = xr[:, RW_WD0:RW_WD0 + LANES]
    ad = xr[:, RW_AD0:RW_AD0 + LANES]
    gd = xr[:, RW_GD0:RW_GD0 + RW_LORA_G]
    logw = -jnp.exp(-_softplus(-(w0_ref[...] + _dot(jnp.tanh(wd), wup_ref[...]))) - 0.5)
    a = _sigmoid(a0_ref[...] + _dot(ad, aup_ref[...]))
    seg = _seg_ones(RW_W, RW_N)
    kr = k * kk_ref[...]
    kk = kr * lax.rsqrt(_dot_hp_exact_rhs(kr * kr, seg) + EPS)
    k2 = k * (1.0 + (a - 1.0) * ka_ref[...])
    r_o[...] = r
    w_o[...] = jnp.exp(logw)
    k_o[...] = k2
    v_o[...] = v
    nkk_o[...] = -kk
    kka_o[...] = kk * a
    gate_o[...] = _dot(_sigmoid(gd), gup_ref[...])
    bonus_o[...] = _dot_hp_exact_rhs(r * k2 * rk_ref[...], seg) * v


def rwkv_prep(p_rw, prev, mu, w0, a0, k_k, k_a, r_k, w_up, a_up, g_up, *, tm, seq_len=None):
    m = p_rw.shape[0]
    row = lambda w: pl.BlockSpec((tm, w), lambda i: (i, 0))
    full = lambda a: pl.BlockSpec(a.shape, lambda i: (0,) * a.ndim)
    args = (mu.reshape(1, -1), w0.reshape(1, -1), a0.reshape(1, -1), k_k.reshape(1, -1), k_a.reshape(1, -1),
            r_k.reshape(1, -1), w_up, a_up, g_up)
    in_kernel = seq_len is not None
    return pl.pallas_call(
        functools.partial(_rw_prep_body, tiles_per_seq=seq_len // tm if in_kernel else None),
        grid=(m // tm,),
        in_specs=[row(RW_SLAB), full(prev) if in_kernel else row(RW_SLAB)] + [full(a) for a in args],
        out_specs=[row(RW_W)] * 8,
        out_shape=[jax.ShapeDtypeStruct((m, RW_W), F32)] * 8,
        scratch_shapes=[pltpu.VMEM((tm + SUBLANES, RW_SLAB), F32)] if in_kernel else [],
        compiler_params=_cparams("arbitrary" if in_kernel else "parallel"),
        name="rwkv_prep",
    )(p_rw, prev, *args)


def _rw_scan_body(r_ref, w_ref, k_ref, v_ref, nkk_ref, kka_ref, s0_ref, y_ref, s1_ref, s_scr, *, nb, t_blk):
    ti = pl.program_id(1)

    nq = RW_W // RW_QW
    hq = RW_QW // RW_N
    tiles = [(b, q) for b in range(nb) for q in range(nq)]
    nt = len(tiles)

    @pl.when(ti == 0)
    def _():
        for i, (b, q) in enumerate(tiles):
            s_scr[i] = jnp.concatenate([s0_ref[b, q * hq + h] for h in range(hq)], axis=1)

    ones = _seg_ones(RW_QW, RW_N)
    diag = _iota2((RW_N, RW_QW), 0) == (_iota2((RW_N, RW_QW), 1) % RW_N)
    diag_b = diag.astype(BF16)
    tg = SUBLANES if t_blk % SUBLANES == 0 else t_blk
    dot = lambda a: jnp.dot(a, ones, preferred_element_type=F32)
    stack = lambda xs: jnp.concatenate(xs, axis=0)
    piece = lambda a, i: a[i * RW_N:(i + 1) * RW_N]

    def group(g, carry):
        base = pl.multiple_of(g * tg, tg)
        ld = lambda ref: [[ref[b, pl.ds(base, tg), q * RW_QW:(q + 1) * RW_QW] for q in range(nq)] for b in range(nb)]
        rb, wb, kb, vb, nkkb, kkab = (ld(ref) for ref in (r_ref, w_ref, k_ref, v_ref, nkk_ref, kka_ref))
        yrows = [[] for _ in tiles]
        for j in range(tg):
            row = lambda blk, b, q: blk[b][q][j:j + 1, :]
            s = [s_scr[i] for i in range(len(tiles))]
            sa = dot(stack([s[i] * row(nkkb, b, q) for i, (b, q) in enumerate(tiles)]).astype(BF16))
            v1 = [row(vb, b, q).astype(BF16) for b, q in tiles]
            vcol = dot(stack([diag_b * x for x in v1]))
            s = [s[i] * row(wb, b, q) + piece(sa, i) * row(kkab, b, q) + piece(vcol, i) * row(kb, b, q)
                 for i, (b, q) in enumerate(tiles)]
            for i in range(len(tiles)):
                s_scr[i] = s[i]
            yb = dot(stack([s[i] * row(rb, b, q) for i, (b, q) in enumerate(tiles)]).astype(BF16))
            for i in range(nt):
                yrows[i].append(jnp.sum(jnp.where(diag, piece(yb, i), 0.0), axis=0, keepdims=True))
        for i, (b, q) in enumerate(tiles):
            y_ref[b, pl.ds(base, tg), q * RW_QW:(q + 1) * RW_QW] = stack(yrows[i])
        return carry

    lax.fori_loop(0, t_blk // tg, group, 0)

    @pl.when(ti == pl.num_programs(1) - 1)
    def _():
        for i, (b, q) in enumerate(tiles):
            for h in range(hq):
                s1_ref[b, q * hq + h] = s_scr[i][:, h * RW_N:(h + 1) * RW_N]


def rwkv_scan(r, w, k, v, nkk, kka, s0, *, layer, nb, t_blk):
    nseq, L, _ = r.shape
    nq = RW_W // RW_QW
    tok = pl.BlockSpec((nb, t_blk, RW_W), lambda b, t: (b, t, 0))
    return pl.pallas_call(
        functools.partial(_rw_scan_body, nb=nb, t_blk=t_blk),
        grid=(nseq // nb, L // t_blk),
        in_specs=[tok] * 6 + [pl.BlockSpec((None, nb, RW_HEADS, RW_N, RW_N), lambda b, t: (layer, b, 0, 0, 0))],
        out_specs=[tok, pl.BlockSpec((nb, RW_HEADS, RW_N, RW_N), lambda b, t: (b, 0, 0, 0))],
        out_shape=[jax.ShapeDtypeStruct((nseq, L, RW_W), F32),
                   jax.ShapeDtypeStruct((nseq, RW_HEADS, RW_N, RW_N), F32)],
        scratch_shapes=[pltpu.VMEM((nb * nq, RW_N, RW_QW), F32)],
        compiler_params=_cparams("parallel", "arbitrary"),
        name="rwkv_scan",
    )(r, w, k, v, nkk, kka, s0)


def _rw_post_body(y_ref, bonus_ref, gate_ref, lw_ref, lb_ref, o_ref):
    y = y_ref[...]
    seg = _seg_ones(RW_W, RW_N)
    mu = _dot_hp_exact_rhs(y, seg) * (1.0 / RW_N)
    d = y - mu
    var = _dot_hp_exact_rhs(d * d, seg) * (1.0 / RW_N)
    yn = d * lax.rsqrt(var + RW_GN_EPS) * lw_ref[...] + lb_ref[...]
    o_ref[...] = ((yn + bonus_ref[...]) * gate_ref[...]).astype(o_ref.dtype)


def rwkv_post(y, bonus, gate, ln_w, ln_b, *, tm):
    m = y.shape[0]
    row = pl.BlockSpec((tm, RW_W), lambda i: (i, 0))
    par = pl.BlockSpec((1, RW_W), lambda i: (0, 0))
    return pl.pallas_call(
        _rw_post_body,
        grid=(m // tm,),
        in_specs=[row, row, row, par, par],
        out_specs=row,
        out_shape=jax.ShapeDtypeStruct((m, RW_W), F32),
        compiler_params=_cparams("parallel"),
        name="rwkv_post",
    )(y, bonus, gate, ln_w.reshape(1, -1), ln_b.reshape(1, -1))


FOX_K0, FOX_V0, FOX_F0 = 512, 1024, 1536
FOX_SLAB = 1792
FOX_TQ = 512
NEG = -1e30


def _fox_prep_body(p_ref, ft_ref, qn_ref, kn_ref, fb_ref, fbt_ref,
                   q_o, k_o, kb_o, vb_o, lf_o, c_o, ct_o, carry, carry_t, *, tm):
    @pl.when(pl.program_id(1) == 0)
    def _():
        carry[...] = jnp.zeros_like(carry)
        carry_t[...] = jnp.zeros_like(carry_t)

    for h in range(FOX_HEADS):
        ls = slice(h * FOX_HD, (h + 1) * FOX_HD)
        q = p_ref[:, ls]
        k = p_ref[:, FOX_K0 + h * FOX_HD:FOX_K0 + (h + 1) * FOX_HD]
        qn = q * lax.rsqrt(jnp.mean(q * q, axis=-1, keepdims=True) + EPS) * qn_ref[...]
        kn = k * lax.rsqrt(jnp.mean(k * k, axis=-1, keepdims=True) + EPS) * kn_ref[...]
        q_o[:, ls] = (qn * (FOX_HD ** -0.5)).astype(BF16)
        k_o[:, ls] = kn
        kb_o[:, ls] = kn.astype(BF16)
    vb_o[...] = p_ref[:, FOX_V0:FOX_V0 + BRANCH_W].astype(BF16)
    lf = _log_sigmoid(p_ref[:, FOX_F0:FOX_F0 + LANES] + fb_ref[...])
    lf_o[...] = lf
    row = _iota2((tm, tm), 0)
    col = _iota2((tm, tm), 1)
    c = _dot_hp_exact_lhs((row >= col).astype(BF16), lf) + carry[...]
    c_o[...] = c
    carry[...] = c[tm - 1:tm, :]
    lft = _log_sigmoid(ft_ref[0] + fbt_ref[...])
    ct = _dot_hp_exact_rhs(lft, (row <= col).astype(BF16)) + carry_t[:, 0:1]
    ct_o[0, 0] = ct
    carry_t[...] = jnp.broadcast_to(ct[:, tm - 1:tm], carry_t.shape)


def fox_prep(p_fox, ft, qn_w, kn_w, fb, *, nseq, seq_len, tm):
    n = seq_len // tm
    m = nseq * seq_len
    fbp = jnp.concatenate([fb, jnp.zeros((LANES - FOX_HEADS,), F32)]).reshape(1, LANES)
    fbt = jnp.broadcast_to(jnp.concatenate([fb, jnp.zeros((4,), F32)])[:, None], (SUBLANES, tm))
    row = lambda w, dt: (pl.BlockSpec((tm, w), lambda b, i: (b * n + i, 0)), jax.ShapeDtypeStruct((m, w), dt))
    outs = [row(BRANCH_W, BF16), row(BRANCH_W, F32), row(BRANCH_W, BF16), row(BRANCH_W, BF16),
            row(LANES, F32), row(LANES, F32),
            (pl.BlockSpec((1, 1, SUBLANES, tm), lambda b, i: (b, i, 0, 0)),
             jax.ShapeDtypeStruct((nseq, n, SUBLANES, tm), F32))]
    return pl.pallas_call(
        functools.partial(_fox_prep_body, tm=tm),
        grid=(nseq, n),
        in_specs=[pl.BlockSpec((tm, FOX_SLAB), lambda b, i: (b * n + i, 0)),
                  pl.BlockSpec((1, SUBLANES, tm), lambda b, i: (b, 0, i)),
                  pl.BlockSpec((1, FOX_HD), lambda b, i: (0, 0)),
                  pl.BlockSpec((1, FOX_HD), lambda b, i: (0, 0)),
                  pl.BlockSpec((1, LANES), lambda b, i: (0, 0)),
                  pl.BlockSpec((SUBLANES, tm), lambda b, i: (0, 0))],
        out_specs=[o[0] for o in outs],
        out_shape=[o[1] for o in outs],
        scratch_shapes=[pltpu.VMEM((1, LANES), F32), pltpu.VMEM((SUBLANES, LANES), F32)],
        compiler_params=_cparams("parallel", "arbitrary"),
        name="fox_prep",
    )(p_fox, ft, qn_w.reshape(1, FOX_HD), kn_w.reshape(1, FOX_HD), fbp, fbt)


def _fox_prompt_body(q_ref, k_ref, v_ref, c_ref, ct_ref, o_ref, *, tq):
    h = pl.program_id(1)
    qi = pl.program_id(2)
    q = q_ref[...]
    lane = _iota2((1, LANES), 1)
    cq = jnp.sum(jnp.where(lane == h, c_ref[...], 0.0), axis=-1, keepdims=True)
    rowh = _iota2((SUBLANES, 1), 0) == h
    qpos = qi * tq + _iota2((tq, tq), 0)

    def kv_step(j, carry):
        m, l, acc = carry
        k = k_ref[pl.ds(pl.multiple_of(j * tq, tq), tq), :]
        v = v_ref[pl.ds(pl.multiple_of(j * tq, tq), tq), :]
        ck = jnp.sum(jnp.where(rowh, ct_ref[0, j], 0.0), axis=0, keepdims=True)
        s = _dot_nt(q, k) + (cq - ck)
        s = jnp.where(qpos >= j * tq + _iota2((tq, tq), 1), s, NEG)
        m_new = jnp.maximum(m, jnp.max(s, axis=-1, keepdims=True))
        p = jnp.exp(s - m_new)
        alpha = jnp.exp(m - m_new)
        return m_new, alpha * l + jnp.sum(p, axis=-1, keepdims=True), alpha * acc + _dot(p, v)

    init = (jnp.full((tq, 1), NEG, F32), jnp.zeros((tq, 1), F32), jnp.zeros((tq, FOX_HD), F32))
    m, l, acc = lax.fori_loop(0, qi + 1, kv_step, init)
    o_ref[...] = (acc / l).astype(o_ref.dtype)


def fox_prompt(qb, kb, vb, c, ct, *, nseq, L, tq):
    nq = L // tq
    return pl.pallas_call(
        functools.partial(_fox_prompt_body, tq=tq),
        grid=(nseq, FOX_HEADS, nq),
        in_specs=[pl.BlockSpec((tq, FOX_HD), lambda b, h, i: (b * nq + i, h)),
                  pl.BlockSpec((L, FOX_HD), lambda b, h, i: (b, h)),
                  pl.BlockSpec((L, FOX_HD), lambda b, h, i: (b, h)),
                  pl.BlockSpec((tq, LANES), lambda b, h, i: (b * nq + i, 0)),
                  pl.BlockSpec((1, nq, SUBLANES, tq), lambda b, h, i: (b, 0, 0, 0))],
        out_specs=pl.BlockSpec((tq, FOX_HD), lambda b, h, i: (b * nq + i, h)),
        out_shape=jax.ShapeDtypeStruct((nseq * L, BRANCH_W), F32),
        compiler_params=_cparams("parallel", "parallel", "arbitrary"),
        name="fox_prompt",
    )(qb, kb, vb, c, ct)


PAGE_ROWS = PAGE_SIZE * FOX_HEADS
NEW_ROWS = 32
LF_ROWS = 24


def _fox_sample_body(pt_ref, q_ref, *refs, n_pages, n_new):
    kc = refs[0:n_pages]
    vc = refs[n_pages:2 * n_pages]
    lfc = refs[2 * n_pages:3 * n_pages]
    kn_ref, vn_ref, lfn_ref, o_ref, s_scr = refs[3 * n_pages:]
    nq = q_ref.shape[1]
    nr = FOX_HEADS * nq
    q_all = jnp.concatenate([q_ref[0, :, h * FOX_HD:(h + 1) * FOX_HD] for h in range(FOX_HEADS)], axis=0)

    r = _iota2((PAGE_ROWS, PAGE_ROWS), 0)
    c = _iota2((PAGE_ROWS, PAGE_ROWS), 1)
    same_head = (r % FOX_HEADS) == (c % FOX_HEADS)
    cum_mat = (same_head & (r // FOX_HEADS <= c // FOX_HEADS)).astype(BF16)
    tot_mat = same_head.astype(BF16)
    lf_rows = jnp.concatenate([lfc[i][...] for i in range(n_pages)] + [lfn_ref[0]]
                              + [jnp.zeros((LF_ROWS - n_pages - 1, PAGE_ROWS), F32)], axis=0)
    earlier = (_iota2((LF_ROWS, LF_ROWS), 0) > _iota2((LF_ROWS, LF_ROWS), 1)).astype(BF16)
    bias = (_dot_hp_exact_rhs(lf_rows, cum_mat)
            + _dot_hp_exact_lhs(earlier, _dot_hp_exact_rhs(lf_rows, tot_mat)))

    head_ok = (_iota2((nr, PAGE_ROWS), 0) // nq) == (_iota2((nr, PAGE_ROWS), 1) % FOX_HEADS)
    m = jnp.full((nr, PAGE_ROWS), NEG, F32)
    for i in range(n_pages):
        s = jnp.where(head_ok, _dot_nt(q_all, kc[i][...]) - bias[i:i + 1, :], NEG)
        s_scr[i] = s
        m = jnp.maximum(m, s)
    kr = _iota2((nr, NEW_ROWS), 1)
    qr = _iota2((nr, NEW_ROWS), 0)
    new_ok = ((qr // nq) == (kr % FOX_HEADS)) & (kr // FOX_HEADS <= qr % nq) & (kr < n_new * FOX_HEADS)
    s_new = jnp.where(new_ok, _dot_nt(q_all, kn_ref[0]) - bias[n_pages:n_pages + 1, 0:NEW_ROWS], NEG)
    m = jnp.maximum(jnp.max(m, axis=-1, keepdims=True), jnp.max(s_new, axis=-1, keepdims=True))

    p_new = jnp.exp(s_new - m)
    l = jnp.sum(p_new, axis=-1, keepdims=True)
    acc = _dot(p_new, vn_ref[0])
    psum = jnp.zeros((nr, PAGE_ROWS), F32)
    for i in range(n_pages):
        p = jnp.exp(s_scr[i] - m)
        psum = psum + p
        acc = acc + _dot(p, vc[i][...])
    out = acc / (l + jnp.sum(psum, axis=-1, keepdims=True))
    for h in range(FOX_HEADS):
        o_ref[0, :, h * FOX_HD:(h + 1) * FOX_HD] = out[h * nq:(h + 1) * nq]


def fox_sample(page_table, qb, cache_k, cache_v, cache_lf, kn, vn, lfn, *, layer, n_new):
    nb, nq, _ = qb.shape
    n_pages = page_table.shape[1]
    page = lambda shape, i: pl.BlockSpec(
        (None, None) + shape, functools.partial(lambda b, pt, i: (layer, pt[b * n_pages + i], 0, 0), i=i))
    per_b = lambda shape: pl.BlockSpec((1,) + shape, lambda b, pt: (b, 0, 0))
    in_specs = ([per_b((nq, BRANCH_W))]
                + [page((PAGE_ROWS, FOX_HD), i) for i in range(n_pages)]
                + [page((PAGE_ROWS, FOX_HD), i) for i in range(n_pages)]
                + [page((1, PAGE_ROWS), i) for i in range(n_pages)]
                + [per_b((NEW_ROWS, FOX_HD)), per_b((NEW_ROWS, FOX_HD)), per_b((1, PAGE_ROWS))])
    grid_spec = pltpu.PrefetchScalarGridSpec(
        num_scalar_prefetch=1, grid=(nb,), in_specs=in_specs, out_specs=per_b((nq, BRANCH_W)),
        scratch_shapes=[pltpu.VMEM((n_pages, FOX_HEADS * nq, PAGE_ROWS), F32)])
    return pl.pallas_call(
        functools.partial(_fox_sample_body, n_pages=n_pages, n_new=n_new),
        grid_spec=grid_spec,
        out_shape=jax.ShapeDtypeStruct((nb, nq, BRANCH_W), F32),
        compiler_params=_cparams("parallel"),
        name="fox_sample",
    )(page_table.reshape(-1), qb, *([cache_k] * n_pages), *([cache_v] * n_pages), *([cache_lf] * n_pages),
      kn, vn, lfn)


def _mem_attn_body(q_ref, k_ref, v_ref, o_ref):
    for h in range(MEM_HEADS):
        ls = slice(h * MEM_HD, (h + 1) * MEM_HD)
        s = _dot_nt(q_ref[:, ls], k_ref[0, :, ls]) * (MEM_HD ** -0.5)
        p = jnp.exp(s - jnp.max(s, axis=-1, keepdims=True))
        o_ref[:, ls] = _dot(p, v_ref[0, :, ls]) / jnp.sum(p, axis=-1, keepdims=True)


def mem_attn(q, mk, mv, *, nseq, seq_len, tq):
    m = nseq * seq_len
    n = seq_len // tq
    kv = pl.BlockSpec((1, MEM_LEN, BRANCH_W), lambda b, i: (b, 0, 0))
    return pl.pallas_call(
        _mem_attn_body,
        grid=(nseq, n),
        in_specs=[pl.BlockSpec((tq, BRANCH_W), lambda b, i: (b * n + i, 0)), kv, kv],
        out_specs=pl.BlockSpec((tq, BRANCH_W), lambda b, i: (b * n + i, 0)),
        out_shape=jax.ShapeDtypeStruct((m, BRANCH_W), F32),
        compiler_params=_cparams("parallel", "parallel"),
        name="mem_attn",
    )(q, mk, mv)


def _mem_attn_rows_body(q_ref, k_ref, v_ref, o_ref, *, nb, nq):
    nr = MEM_HEADS * nq
    nk = MEM_LEN * MEM_HEADS
    head_ok = (_iota2((nr, nk), 0) // nq) == (_iota2((nr, nk), 1) % MEM_HEADS)
    for sq in range(nb):
        rs = slice(sq * nq, (sq + 1) * nq)
        q_all = jnp.concatenate([q_ref[rs, h * MEM_HD:(h + 1) * MEM_HD] for h in range(MEM_HEADS)], axis=0)
        s = jnp.where(head_ok, _dot_nt(q_all, k_ref[sq]) * (MEM_HD ** -0.5), NEG)
        p = jnp.exp(s - jnp.max(s, axis=-1, keepdims=True))
        out = _dot(p, v_ref[sq]) / jnp.sum(p, axis=-1, keepdims=True)
        for h in range(MEM_HEADS):
            o_ref[rs, h * MEM_HD:(h + 1) * MEM_HD] = out[h * nq:(h + 1) * nq]


def mem_attn_rows(q, mk, mv, *, layer, nseq, tq, nb=4):
    kv = pl.BlockSpec((None, nb, MEM_LEN * MEM_HEADS, MEM_HD), lambda b: (layer, b, 0, 0))
    return pl.pallas_call(
        functools.partial(_mem_attn_rows_body, nb=nb, nq=tq),
        grid=(nseq // nb,),
        in_specs=[pl.BlockSpec((nb * tq, BRANCH_W), lambda b: (b, 0)), kv, kv],
        out_specs=pl.BlockSpec((nb * tq, BRANCH_W), lambda b: (b, 0)),
        out_shape=jax.ShapeDtypeStruct((nseq * tq, BRANCH_W), F32),
        compiler_params=_cparams("parallel"),
        name="mem_attn_rows",
    )(q, mk, mv)


def gla_state_to_tiles(s):
    st = jnp.swapaxes(s, -1, -2)
    z = jnp.zeros_like(st)
    even = jnp.concatenate([st, z], axis=-1)
    odd = jnp.concatenate([z, st], axis=-1)
    sel = (jnp.arange(GLA_HEADS) % 2 == 1)[None, :, None, None]
    return jnp.where(sel, odd, even)


def gla_tiles_to_state(t):
    even = t[..., :GLA_DK]
    odd = t[..., GLA_DK:]
    sel = (jnp.arange(GLA_HEADS) % 2 == 1)[None, :, None, None]
    return jnp.swapaxes(jnp.where(sel, odd, even), -1, -2)


ROUTER_W = LANES
MOE_TB = 256


def _router_body(x_ref, g_ref, w1_ref, w2_ref, b_ref, lo_ref, h_ref):
    x = x_ref[...]
    h = x * lax.rsqrt(jnp.mean(x * x, axis=-1, keepdims=True) + EPS) * g_ref[...]
    h1 = h.astype(BF16)
    h2 = (h - h1.astype(F32)).astype(BF16)
    d = lambda a, b: jnp.dot(a, b, preferred_element_type=F32)
    lo_ref[...] = (d(h2, w1_ref[...]) + d(h1, w2_ref[...])) + d(h1, w1_ref[...]) + b_ref[...]
    half = h.shape[1] // 2
    u = pltpu.bitcast(h, jnp.uint32)
    rne = (u + jnp.uint32(0x7FFF) + ((u >> 16) & jnp.uint32(1))) & jnp.uint32(0xFFFF0000)
    h_ref[...] = rne[:, :half] | (rne[:, half:] >> 16)


def _unpack_bf16_pairs(packed):
    hi = pltpu.bitcast(packed & jnp.uint32(0xFFFF0000), F32).astype(BF16)
    lo = pltpu.bitcast(packed << 16, F32).astype(BF16)
    return hi, lo


def router(x, g, w_hi, w_lo, b, *, tm):
    m, d = x.shape
    full = lambda a: pl.BlockSpec(a.shape, lambda i: (0,) * a.ndim)
    g2 = g.reshape(1, d)
    return pl.pallas_call(
        _router_body,
        grid=(m // tm,),
        in_specs=[pl.BlockSpec((tm, d), lambda i: (i, 0)), full(g2), full(w_hi), full(w_lo), full(b)],
        out_specs=[pl.BlockSpec((tm, ROUTER_W), lambda i: (i, 0)), pl.BlockSpec((tm, d // 2), lambda i: (i, 0))],
        out_shape=[jax.ShapeDtypeStruct((m, ROUTER_W), F32), jax.ShapeDtypeStruct((m, d // 2), jnp.uint32)],
        compiler_params=_cparams("parallel"),
        name="router",
    )(x, g2, w_hi, w_lo, b)


def _moe_body(be_ref, nv_ref, xs_ref, w1_ref, w3_ref, w2_ref, o_ref, w1b, w3b, w2b):
    i = pl.program_id(0)

    @pl.when(i < nv_ref[0])
    def _():
        @pl.when((i == 0) | (be_ref[i] != be_ref[jnp.maximum(i - 1, 0)]))
        def _():
            w1b[...] = w1_ref[...].astype(BF16)
            w3b[...] = w3_ref[...].astype(BF16)
            w2b[...] = w2_ref[...].astype(BF16)

        x_hi, x_lo = _unpack_bf16_pairs(xs_ref[...])
        half = x_hi.shape[1]
        up = lambda w: (jnp.dot(x_hi, w[0:half, :], preferred_element_type=F32)
                        + jnp.dot(x_lo, w[half:, :], preferred_element_type=F32))
        o_ref[...] = jnp.dot((_silu(up(w1b)) * up(w3b)).astype(BF16), w2b[...], preferred_element_type=F32)

    @pl.when(i >= nv_ref[0])
    def _():
        o_ref[...] = jnp.zeros_like(o_ref)


def moe_experts(block_e, n_valid, xs, w1, w3, w2, *, layer):
    n_rows = xs.shape[0]
    d, f = w1.shape[-2:]
    grid_spec = pltpu.PrefetchScalarGridSpec(
        num_scalar_prefetch=2, grid=(n_rows // MOE_TB,),
        in_specs=[pl.BlockSpec((MOE_TB, d // 2), lambda i, be, nv: (i, 0)),
                  pl.BlockSpec((None, None, d, f), lambda i, be, nv: (layer, be[i], 0, 0)),
                  pl.BlockSpec((None, None, d, f), lambda i, be, nv: (layer, be[i], 0, 0)),
                  pl.BlockSpec((None, None, f, d), lambda i, be, nv: (layer, be[i], 0, 0))],
        out_specs=pl.BlockSpec((MOE_TB, d), lambda i, be, nv: (i, 0)),
        scratch_shapes=[pltpu.VMEM((d, f), BF16), pltpu.VMEM((d, f), BF16), pltpu.VMEM((f, d), BF16)])
    return pl.pallas_call(
        _moe_body,
        grid_spec=grid_spec,
        out_shape=jax.ShapeDtypeStruct((n_rows, d), F32),
        compiler_params=_cparams("arbitrary"),
        name="moe_experts",
    )(block_e, n_valid, xs, w1, w3, w2)


def _route(logits):
    m = logits.shape[0]
    pg = jax.nn.softmax(logits[:, :N_GROUPS], axis=-1)
    grp = jnp.argmax(pg, axis=-1)
    pg_top = jnp.max(pg, axis=-1)
    le = logits[:, N_GROUPS:N_GROUPS + N_EXPERTS].reshape(m, N_GROUPS, EXPERTS_PER_GROUP)
    pe = jax.nn.softmax(le[jnp.arange(m), grp], axis=-1)
    top_p, top_i = lax.top_k(pe, TOP_K)
    wts = pg_top[:, None] * top_p / jnp.sum(top_p, axis=-1, keepdims=True)
    eid = grp[:, None] * EXPERTS_PER_GROUP + top_i
    return eid, wts


def _sorted_layout(eid, m):
    n_slots = m * TOP_K
    flat_e = eid.reshape(n_slots)
    order = jnp.argsort(flat_e)
    sorted_e = flat_e[order]
    counts = jnp.bincount(flat_e, length=N_EXPERTS)
    padded = (counts + MOE_TB - 1) // MOE_TB * MOE_TB
    pad_end = jnp.cumsum(padded)
    pad_start = pad_end - padded
    start = jnp.cumsum(counts) - counts
    dest = (pad_start[sorted_e] + jnp.arange(n_slots) - start[sorted_e]).astype(jnp.int32)
    n_blocks = -(-n_slots // MOE_TB) + N_EXPERTS
    block_e = jnp.minimum(jnp.searchsorted(pad_end, jnp.arange(n_blocks) * MOE_TB, side='right'),
                          N_EXPERTS - 1).astype(jnp.int32)
    n_valid = (pad_end[-1:] // MOE_TB).astype(jnp.int32)
    rows = jnp.arange(n_blocks * MOE_TB)
    row_e = jnp.repeat(block_e, MOE_TB)
    i_in_e = rows - pad_start[row_e]
    src = jnp.clip(start[row_e] + i_in_e, 0, n_slots - 1)
    slot_tok = jnp.where(i_in_e < counts[row_e], order[src] // TOP_K, rows % m).astype(jnp.int32)
    pos = dest[jnp.argsort(order)].reshape(m, TOP_K)
    return slot_tok, block_e, n_valid, pos


def _rms_rows_body(x_ref, g_ref, o_ref):
    x = x_ref[...]
    o_ref[...] = x * lax.rsqrt(jnp.mean(x * x, axis=-1, keepdims=True) + EPS) * g_ref[...]


def rms_rows(x, g, *, row0, rows, tm):
    m, d = rows, x.shape[1]
    return pl.pallas_call(
        _rms_rows_body,
        grid=(m // tm,),
        in_specs=[pl.BlockSpec((tm, d), lambda i: (row0 // tm + i, 0)), pl.BlockSpec((1, d), lambda i: (0, 0))],
        out_specs=pl.BlockSpec((tm, d), lambda i: (i, 0)),
        out_shape=jax.ShapeDtypeStruct((m, d), F32),
        compiler_params=_cparams("parallel"),
        name="rms_rows",
    )(x, g.reshape(1, d))


def _slab(w, pieces, width):
    cols = []
    for start, stop, padded in pieces:
        cols.append(w[..., start:stop])
        if padded > stop - start:
            cols.append(jnp.zeros(w.shape[:-1] + (padded - (stop - start),), w.dtype))
    out = jnp.concatenate(cols, axis=-1)
    assert out.shape[-1] == width, (out.shape, width)
    return out


_DN0, _RW0, _GLA0, _FOX0 = 0, 2056, 4040, 5592
_DN_PIECES = [(_DN0, _DN0 + 2048, 2048), (_DN0 + 2048, _DN0 + 2056, 256)]
_RW_REL = [(0, 1536, 1536), (1536, 1632, 128), (1632, 1728, 128), (1728, 1984, 256)]
_RW_PIECES = [(_RW0 + a, _RW0 + b, p) for a, b, p in _RW_REL]
_GLA_PIECES = [(_GLA0, _GLA0 + 1536, 1536), (_GLA0 + 1536, _GLA0 + 1552, 256)]
_FOX_PIECES = [(_FOX0, _FOX0 + 1536, 1536), (_FOX0 + 1536, _FOX0 + 1540, 256)]


def _rw_unslab(x):
    return jnp.concatenate([x[..., :1632], x[..., RW_AD0:RW_AD0 + 96], x[..., RW_GD0:]], axis=-1)


def _pad_rows8(x, nseq, seq_len):
    return jnp.pad(x.reshape(nseq, seq_len, x.shape[-1]), ((0, 0), (0, SUBLANES - seq_len), (0, 0)))


def kernel(x_prompt, x_sample, cache_fox_k, cache_fox_v, cache_fox_logf, state_dn_conv, state_dn, state_rw_shift, state_rw, state_gla, cache_mem_k, cache_mem_v, page_table, mem_prompt, norm_mix, w_in, dn_conv_w, dn_A_log, dn_dt_bias, dn_norm_w, rw_mu, rw_w0, rw_w_up, rw_a0, rw_a_up, rw_g_up, rw_k_k, rw_k_a, rw_r_k, rw_ln_w, rw_ln_b, gla_a_up, gla_a_b, gla_norm_w, fox_qn, fox_kn, fox_fb, w_gate, b_gate, w_branch, w_out, norm_mem, norm_memkv, mem_wq, mem_wk, mem_wv, mem_wo, norm_ffn, router_g, router_g_b, router_e, router_e_b, moe_w1, moe_w3, moe_w2, norm_final):
    bp, lp, d = x_prompt.shape
    bs, ls, _ = x_sample.shape
    depth = w_in.shape[0]
    n_p = bp * lp
    n_s = bs * ls
    m = n_p + n_s
    n_phys = cache_fox_k.shape[1]
    n_mem = mem_prompt.shape[1]

    w_dn = _slab(w_in, _DN_PIECES, DN_SLAB).astype(BF16)
    w_rw = _slab(w_in, _RW_PIECES, RW_SLAB).astype(BF16)
    w_gla = _slab(w_in, _GLA_PIECES, GLA_SLAB).astype(BF16)
    w_fox = _slab(w_in, _FOX_PIECES, FOX_SLAB).astype(BF16)
    mu_r = _slab(rw_mu, _RW_REL, RW_SLAB)
    pad_lora = lambda w: jnp.pad(w, ((0, 0), (0, LANES - w.shape[1]), (0, 0))).astype(BF16)
    rw_w_up_b, rw_a_up_b, rw_g_up_b = pad_lora(rw_w_up), pad_lora(rw_a_up), rw_g_up.astype(BF16)
    w_gate_b, w_branch_b, w_out_b = w_gate.astype(BF16), w_branch.astype(BF16), w_out.astype(BF16)
    mem_wq_b, mem_wo_b = mem_wq.astype(BF16), mem_wo.astype(BF16)
    mem_wkv_b = jnp.concatenate([mem_wk, mem_wv], axis=-1).astype(BF16)
    w_router = jnp.pad(jnp.concatenate([router_g, router_e], axis=-1),
                       ((0, 0), (0, 0), (0, ROUTER_W - N_GROUPS - N_EXPERTS)))
    w_router_hi = w_router.astype(BF16)
    w_router_lo = (w_router - w_router_hi.astype(F32)).astype(BF16)
    b_router = jnp.pad(jnp.concatenate([router_g_b, router_e_b], axis=-1),
                       ((0, 0), (0, ROUTER_W - N_GROUPS - N_EXPERTS))).reshape(depth, 1, ROUTER_W)

    x = jnp.concatenate([x_prompt.reshape(n_p, d), x_sample.reshape(n_s, d)], axis=0)
    mem_x = mem_prompt.reshape(bp * n_mem, d)
    zeros = lambda *shape: jnp.zeros(shape, F32)
    rows = lambda a_p, a_s: jnp.concatenate([a_p, a_s], axis=0)
    unpad8 = lambda a: a.reshape(bs, SUBLANES, a.shape[-1])[:, :ls].reshape(n_s, a.shape[-1])
    cache_k_rows = cache_fox_k.reshape(depth, n_phys, PAGE_ROWS, FOX_HD)
    cache_v_rows = cache_fox_v.reshape(depth, n_phys, PAGE_ROWS, FOX_HD)
    cache_lf_rows = cache_fox_logf.reshape(depth, n_phys, 1, PAGE_ROWS)
    mem_k_rows = cache_mem_k.reshape(depth, bs, n_mem * MEM_HEADS, MEM_HD)
    mem_v_rows = cache_mem_v.reshape(depth, bs, n_mem * MEM_HEADS, MEM_HD)
    p_out, s_out = [], []

    for l in range(depth):
        g_mix = norm_mix[l]
        proj = lambda w, tn, name: (
            rms_matmul(x, g_mix, w, tm=1024, tn=tn, rows=n_p, name=name),
            rms_matmul(x, g_mix, w, tm=512, tn=tn, row0=n_p, rows=n_s, name=name + "_s"))
        p_dn, p_dn_s = proj(w_dn[l], 768, "in_proj_dn")
        p_rw, p_rw_s = proj(w_rw[l], 1024, "in_proj_rw")
        p_gla, p_gla_s = proj(w_gla[l], 896, "in_proj_gla")
        p_fox, pf_s = proj(w_fox[l], 896, "in_proj_fox")

        dn_args = (dn_conv_w[l], dn_A_log[l], dn_dt_bias[l], dn_norm_w[l])
        o_dn_p, dn_p = deltanet(p_dn, *dn_args, zeros(bp, SUBLANES, DN_CONV_CH),
                                zeros(1, bp, DN_HEADS, DN_DK, DN_DV), layer=0, nseq=bp, seq_len=lp, c=DN_CHUNK,
                                valid=DN_CHUNK,
                                nb=DN_STACK // (DN_HEADS * DN_CHUNK))
        dn_slab_s = _pad_rows8(p_dn_s, bs, ls)
        conv0_s = jnp.pad(state_dn_conv[l], ((0, 0), (SUBLANES - DN_CONV + 1, 0), (0, 0)))
        o_dn_s, dn_s = deltanet(dn_slab_s.reshape(bs * SUBLANES, DN_SLAB), *dn_args, conv0_s, state_dn,
                                layer=l, nseq=bs, seq_len=SUBLANES, c=SUBLANES, valid=ls,
                                nb=DN_STACK // (DN_HEADS * SUBLANES))
        dn_conv_p = p_dn.reshape(bp, lp, DN_SLAB)[:, lp - (DN_CONV - 1):, :DN_CONV_CH]
        dn_conv_s = dn_slab_s[:, ls - (DN_CONV - 1):ls, :DN_CONV_CH]

        rw_p = p_rw.reshape(bp, lp, RW_SLAB)
        rw_s = p_rw_s.reshape(bs, ls, RW_SLAB)
        prev_s = jnp.concatenate([_slab(state_rw_shift[l], _RW_REL, RW_SLAB)[:, None], rw_s[:, :-1]],
                                 axis=1).reshape(n_s, RW_SLAB)
        rw_args = (mu_r[l], rw_w0[l], rw_a0[l], rw_k_k[l], rw_k_a[l], rw_r_k[l].reshape(-1), rw_w_up_b[l],
                   rw_a_up_b[l], rw_g_up_b[l])
        tok_p = rwkv_prep(p_rw, zeros(SUBLANES, RW_SLAB), *rw_args, tm=256, seq_len=lp)
        tok_s = rwkv_prep(p_rw_s, prev_s, *rw_args, tm=256)
        y_p, rw_st_p = rwkv_scan(*(a.reshape(bp, lp, RW_W) for a in tok_p[:6]),
                                 zeros(1, bp, RW_HEADS, RW_N, RW_N), layer=0, nb=bp, t_blk=64)
        y_s, rw_st_s = rwkv_scan(*(a.reshape(bs, ls, RW_W) for a in tok_s[:6]), state_rw, layer=l, nb=8, t_blk=ls)
        o_rw_p = rwkv_post(y_p.reshape(n_p, RW_W), tok_p[7], tok_p[6], rw_ln_w[l], rw_ln_b[l], tm=512)
        o_rw_s = rwkv_post(y_s.reshape(n_s, RW_W), tok_s[7], tok_s[6], rw_ln_w[l], rw_ln_b[l], tm=512)

        gla_args = (gla_a_up[l], gla_a_b[l], gla_norm_w[l])
        o_gla_p, gla_tiles_p = gla(p_gla.reshape(bp, lp, GLA_SLAB), *gla_args,
                                   zeros(bp, GLA_HEADS, GLA_DV, LANES), c=GLA_CHUNK, valid=GLA_CHUNK,
                                   nb=GLA_STACK // GLA_CHUNK)
        o_gla_s, gla_tiles_s = gla(_pad_rows8(p_gla_s, bs, ls), *gla_args, gla_state_to_tiles(state_gla[l]),
                                   c=SUBLANES, valid=ls, nb=GLA_STACK // SUBLANES)

        fox_args = (fox_qn[l], fox_kn[l], fox_fb[l])
        ft_p = p_fox[:, FOX_F0:FOX_F0 + SUBLANES].reshape(bp, lp, SUBLANES).transpose(0, 2, 1)
        qb_p, kn_p, kb_p, vb_p, lf_p, c_p, ct_p = fox_prep(p_fox, ft_p, *fox_args, nseq=bp, seq_len=lp, tm=FOX_TQ)
        o_fox_p = fox_prompt(qb_p, kb_p, vb_p, c_p, ct_p, nseq=bp, L=lp, tq=FOX_TQ)
        ft_s = pf_s[:, FOX_F0:FOX_F0 + SUBLANES].T[None]
        qb_s, kn_s, _, _, lf_s, _, _ = fox_prep(pf_s, ft_s, *fox_args, nseq=1, seq_len=n_s, tm=256)
        new_rows = lambda a: jnp.pad(a.reshape(bs, ls * FOX_HEADS, FOX_HD),
                                     ((0, 0), (0, NEW_ROWS - ls * FOX_HEADS), (0, 0)))
        lfn = jnp.pad(lf_s[:, :FOX_HEADS].reshape(bs, 1, ls * FOX_HEADS),
                      ((0, 0), (0, 0), (0, PAGE_ROWS - ls * FOX_HEADS)))
        o_fox_s = fox_sample(page_table, _pad_rows8(qb_s, bs, ls), cache_k_rows, cache_v_rows, cache_lf_rows,
                             new_rows(kn_s), new_rows(pf_s[:, FOX_V0:FOX_V0 + BRANCH_W]), lfn, layer=l, n_new=ls)

        outs_p = [o_dn_p, o_rw_p, o_gla_p, o_fox_p]
        outs_s = [unpad8(o_dn_s), o_rw_s, unpad8(o_gla_s), o_fox_s[:, :ls].reshape(n_s, BRANCH_W)]
        x = merge_out(x, g_mix, outs_p, outs_s, w_gate_b[l], b_gate[l], w_branch_b[l], w_out_b[l], tm=512, tn=256)

        q_mem = rms_matmul(x, norm_mem[l], mem_wq_b[l], tm=512, tn=512, name="mem_q")
        mkv = rms_matmul(mem_x, norm_memkv[l], mem_wkv_b[l], tm=512, tn=512, name="mem_kv")
        mk = mkv[:, :BRANCH_W].reshape(bp, n_mem, BRANCH_W)
        mv = mkv[:, BRANCH_W:].reshape(bp, n_mem, BRANCH_W)
        att_p = mem_attn(q_mem, mk, mv, nseq=bp, seq_len=lp, tq=512)
        att_s = mem_attn_rows(_pad_rows8(q_mem[n_p:], bs, ls).reshape(bs * SUBLANES, BRANCH_W), mem_k_rows,
                              mem_v_rows, layer=l, nseq=bs, tq=SUBLANES)
        x = matmul_res(rows(att_p, unpad8(att_s)), mem_wo_b[l], x, tm=512, tn=1024, name="mem_out")

        logits, h_ffn = router(x, norm_ffn[l], w_router_hi[l], w_router_lo[l], b_router[l], tm=512)
        eid, wts = _route(logits)
        slot_tok, block_e, n_valid, pos = _sorted_layout(eid, m)
        xs = h_ffn[slot_tok]
        ys = moe_experts(block_e, n_valid, xs, moe_w1, moe_w3, moe_w2, layer=l)
        x = x + wts[:, 0:1] * ys[pos[:, 0]] + wts[:, 1:2] * ys[pos[:, 1]]

        p_out.append((dn_conv_p, dn_p, _rw_unslab(rw_p[:, -1]), rw_st_p,
                      gla_tiles_to_state(gla_tiles_p),
                      kn_p.reshape(bp, lp, FOX_HEADS, FOX_HD),
                      p_fox[:, FOX_V0:FOX_V0 + BRANCH_W].reshape(bp, lp, FOX_HEADS, FOX_HD),
                      lf_p[:, :FOX_HEADS].reshape(bp, lp, FOX_HEADS),
                      mk.reshape(bp, n_mem, MEM_HEADS, MEM_HD), mv.reshape(bp, n_mem, MEM_HEADS, MEM_HD)))
        s_out.append((dn_conv_s, dn_s, _rw_unslab(rw_s[:, -1]), rw_st_s,
                      gla_tiles_to_state(gla_tiles_s),
                      kn_s.reshape(bs, ls, FOX_HEADS, FOX_HD),
                      pf_s[:, FOX_V0:FOX_V0 + BRANCH_W].reshape(bs, ls, FOX_HEADS, FOX_HD),
                      lf_s[:, :FOX_HEADS].reshape(bs, ls, FOX_HEADS)))

    (p_dn_conv, p_dn_st, p_rw_shift, p_rw_st, p_gla_st, p_fox_k, p_fox_v, p_fox_logf, p_mem_k,
     p_mem_v) = [jnp.stack(r) for r in zip(*p_out)]
    (s_dn_conv, s_dn_st, s_rw_shift, s_rw_st, s_gla_st, s_fox_k, s_fox_v,
     s_fox_logf) = [jnp.stack(r) for r in zip(*s_out)]
    y_prompt = rms_rows(x, norm_final, row0=0, rows=n_p, tm=512).reshape(bp, lp, d)
    y_sample = rms_rows(x, norm_final, row0=n_p, rows=n_s, tm=512).reshape(bs, ls, d)
    return (y_prompt, y_sample, p_fox_k, p_fox_v, p_fox_logf, p_dn_conv, p_dn_st, p_rw_shift, p_rw_st, p_gla_st,
            p_mem_k, p_mem_v, s_fox_k, s_fox_v, s_fox_logf, s_dn_conv, s_dn_st, s_rw_shift, s_rw_st, s_gla_st)
```
